```python
import math
import jax, jax.numpy as jnp
from jax import lax
import numpy as np

D_MODEL = 1024
BATCH = 4
SEQ = 4096
DEPTH = 4

GLA_HEADS = 4
GLA_DK = 32
GLA_DV = 64
GLA_RANK = 16
GLA_GATE_NORM = 16.0
GLA_CHUNK = 64
DIL_PATTERNS = ((128, 1), (512, 4), (2048, 16))
DIL_HEADS = 4
DIL_DIM = 64
DIL_STEPS = 128
GM_GROUPS = 4
GM_DIM = 64
GM_CHUNK = 128
SSD_HEADS = 8
SSD_P = 64
SSD_GROUPS = 2
SSD_N = 128
SSD_CONV = 4
SSD_CHUNK = 128
N_BRANCH = 4
MLP_HIDDEN = 4 * D_MODEL
EPS = 1e-6

GLA_QK = GLA_HEADS * GLA_DK
GLA_V = GLA_HEADS * GLA_DV
DIL_W = len(DIL_PATTERNS) * DIL_HEADS * DIL_DIM
DIL_OUT = DIL_HEADS * DIL_DIM
GM_W = GM_GROUPS * GM_DIM
SSD_INNER = SSD_HEADS * SSD_P
SSD_BC = SSD_GROUPS * SSD_N
SSD_XBC = SSD_INNER + 2 * SSD_BC
IN_SPLITS = (GLA_QK, GLA_QK, GLA_V, GLA_V, GLA_RANK,
             DIL_W, DIL_W, DIL_W,
             GM_W, GM_W,
             SSD_INNER, SSD_XBC, SSD_HEADS,
             N_BRANCH * D_MODEL)
IN_WIDTH = sum(IN_SPLITS)
SPLIT_POINTS = tuple(int(v) for v in np.cumsum(IN_SPLITS)[:-1])

kernel_name = 'hybrid_gated_parallel_mixer_block'


def rms_norm(x, w):
    xf = x.astype(jnp.float32)
    y = xf * lax.rsqrt(jnp.mean(xf * xf, axis=-1, keepdims=True) + EPS)
    return (y * w.astype(jnp.float32)).astype(x.dtype)


def alibi_slopes(n):
    return (2.0 ** (-8.0 * np.arange(1, n + 1) / n)).astype(np.float32)


def gla_mixer(q, k, v, g, lr, w_a2, b_a, norm_w):
    B, S, _ = q.shape
    H, K, V, C = GLA_HEADS, GLA_DK, GLA_DV, GLA_CHUNK
    n = S // C
    f32 = jnp.float32
    log_a = jax.nn.log_sigmoid((lr @ w_a2 + b_a).astype(f32)) / GLA_GATE_NORM

    def chunks(t, d):
        return t.astype(f32).reshape(B, n, C, H, d).transpose(0, 3, 1, 2, 4)

    qc = chunks(q, K) * (K ** -0.5)
    kc = chunks(k, K)
    vc = chunks(v, V)
    b = jnp.cumsum(chunks(log_a, K), axis=3)
    b_last = b[:, :, :, -1:, :]
    q_t = qc * jnp.exp(b)
    k_t = kc * jnp.exp(-b)
    causal = jnp.tril(jnp.ones((C, C), bool))
    att = jnp.where(causal, jnp.einsum('bhnik,bhnjk->bhnij', q_t, k_t), 0.0)
    o = jnp.einsum('bhnij,bhnjv->bhniv', att, vc)
    kv = jnp.einsum('bhnjk,bhnjv->bhnkv', kc * jnp.exp(b_last - b), vc)
    decay = jnp.exp(b_last[:, :, :, 0, :])

    def step(state, inp):
        d, u = inp
        return d[..., None] * state + u, state

    _, prev = lax.scan(step, jnp.zeros((B, H, K, V), f32),
                       (jnp.moveaxis(decay, 2, 0), jnp.moveaxis(kv, 2, 0)))
    prev = jnp.moveaxis(prev, 0, 2)
    o = o + jnp.einsum('bhnik,bhnkv->bhniv', q_t, prev)
    o = o.transpose(0, 2, 3, 1, 4).reshape(B, S, H, V)
    o = o * lax.rsqrt(jnp.mean(o * o, axis=-1, keepdims=True) + EPS) * norm_w.astype(f32)
    o = o * jax.nn.silu(g.astype(f32)).reshape(B, S, H, V)
    return o.reshape(B, S, H * V).astype(q.dtype)


def dilated_group(q, k, v, window, dil, slopes):
    B, S, H, E = q.shape
    BLK = DIL_STEPS
    n = S // dil
    nb = -(-n // BLK)
    npad = nb * BLK
    f32 = jnp.float32

    def to_blocks(t):
        t = t.astype(f32).reshape(B, n, dil, H, E).transpose(0, 2, 1, 3, 4)
        t = jnp.pad(t, ((0, 0), (0, 0), (0, npad - n), (0, 0), (0, 0)))
        return t.reshape(B, dil, nb, BLK, H, E)

    def with_prev(t):
        prev = jnp.pad(t, ((0, 0), (0, 0), (1, 0), (0, 0), (0, 0), (0, 0)))[:, :, :-1]
        return jnp.concatenate([prev, t], axis=3)

    qb = to_blocks(q)
    kk = with_prev(to_blocks(k))
    vv = with_prev(to_blocks(v))
    s = jnp.einsum('brnqhe,brnkhe->brnhqk', qb, kk) * (E ** -0.5)
    steps = np.arange(BLK)[:, None] + BLK - np.arange(2 * BLK)[None, :]
    key_abs = np.arange(nb)[:, None] * BLK + np.arange(2 * BLK)[None, :] - BLK
    mask = ((steps >= 0) & (steps <= window // dil))[None] & (key_abs >= 0)[:, None, :]
    bias = (-slopes[:, None, None] * (steps * dil)[None]).astype(np.float32)
    s = jnp.where(mask[:, None], s + bias, -jnp.inf)
    m = jnp.max(s, axis=-1, keepdims=True)
    p = jnp.exp(s - m)
    l = jnp.sum(p, axis=-1, keepdims=True)
    o = jnp.einsum('brnhqk,brnkhe->brnqhe', p, vv) / l.transpose(0, 1, 2, 4, 3, 5)
    lse = (m + jnp.log(l))[..., 0].transpose(0, 1, 2, 4, 3)

    def from_blocks(t):
        t = t.reshape((B, dil, npad) + t.shape[4:])[:, :, :n]
        return jnp.moveaxis(t, 1, 2).reshape((B, S) + t.shape[3:])

    return from_blocks(o), from_blocks(lse)


def dilated_mixer(q, k, v):
    B, S, _ = q.shape
    n_pat = len(DIL_PATTERNS)
    shp = (B, S, n_pat, DIL_HEADS, DIL_DIM)
    qs, ks, vs = q.reshape(shp), k.reshape(shp), v.reshape(shp)
    slopes = alibi_slopes(n_pat * DIL_HEADS).reshape(n_pat, DIL_HEADS)
    outs, lses = [], []
    for p, (window, dil) in enumerate(DIL_PATTERNS):
        o, lse = dilated_group(qs[:, :, p], ks[:, :, p], vs[:, :, p], window, dil, slopes[p])
        outs.append(o)
        lses.append(lse)
    w = jax.nn.softmax(jnp.stack(lses), axis=0)[..., None]
    o = jnp.sum(w * jnp.stack(outs), axis=0)
    return o.reshape(B, S, DIL_OUT).astype(q.dtype)


def gmlp_mixer(u, v, ln_w, ln_b, ws, bs):
    B, S, _ = u.shape
    n = S // GM_CHUNK
    f32 = jnp.float32
    out_dtype = u.dtype
    u = jax.nn.gelu(u.astype(f32), approximate=False)
    v = jax.nn.gelu(v.astype(f32), approximate=False)
    mu = jnp.mean(v, axis=-1, keepdims=True)
    var = jnp.mean(jnp.square(v - mu), axis=-1, keepdims=True)
    v = (v - mu) * lax.rsqrt(var + EPS) * ln_w.astype(f32) + ln_b.astype(f32)
    v = v.reshape(B, n, GM_CHUNK, GM_GROUPS, GM_DIM)
    w = jnp.where(jnp.tril(jnp.ones((GM_CHUNK, GM_CHUNK), bool)), ws.astype(f32), 0.0)
    sv = jnp.einsum('gij,bnjgc->bnigc', w, v) + bs.astype(f32).T[:, :, None]
    return (u * sv.reshape(B, S, GM_W)).astype(out_dtype)


def ssd_mixer(z, xbc, dt, conv_w, conv_b, dt_bias, a_log, d_skip, norm_w):
    B, S, _ = z.shape
    H, P, G, N, L = SSD_HEADS, SSD_P, SSD_GROUPS, SSD_N, SSD_CHUNK
    R = H // G
    n = S // L
    f32 = jnp.float32
    xbc = lax.conv_general_dilated(xbc, conv_w[:, None, :], window_strides=(1,),
                                   padding=[(SSD_CONV - 1, 0)],
                                   dimension_numbers=('NWC', 'WIO', 'NWC'),
                                   feature_group_count=SSD_XBC) + conv_b
    xbc = jax.nn.silu(xbc.astype(f32))
    xs = xbc[..., :SSD_INNER].reshape(B, n, L, G, R, P)
    Bs = xbc[..., SSD_INNER:SSD_INNER + SSD_BC].reshape(B, n, L, G, N)
    Cs = xbc[..., SSD_INNER + SSD_BC:].reshape(B, n, L, G, N)
    dts = jax.nn.softplus(dt.astype(f32) + dt_bias.astype(f32)).reshape(B, n, L, G, R)
    A = -jnp.exp(a_log.astype(f32)).reshape(G, R)
    cs = jnp.cumsum(dts * A, axis=2)
    seg = cs[:, :, :, None] - cs[:, :, None, :]
    tri = jnp.tril(jnp.ones((L, L), bool))[:, :, None, None]
    Lmat = jnp.exp(jnp.where(tri, seg, -jnp.inf))
    cb = jnp.einsum('bclgn,bcsgn->bclsg', Cs, Bs)
    wgt = cb[..., None] * Lmat * dts[:, :, None]
    y = jnp.einsum('bclsgr,bcsgrp->bclgrp', wgt, xs)
    decay_states = jnp.exp(cs[:, :, -1:] - cs) * dts
    states = jnp.einsum('bclgn,bclgrp->bcgrpn', Bs, xs * decay_states[..., None])
    chunk_decay = jnp.exp(cs[:, :, -1])

    def step(h, inp):
        dcy, st = inp
        return dcy[..., None, None] * h + st, h

    _, prev = lax.scan(step, jnp.zeros((B, G, R, P, N), f32),
                       (jnp.moveaxis(chunk_decay, 1, 0), jnp.moveaxis(states, 1, 0)))
    prev = jnp.moveaxis(prev, 0, 1)
    y = y + jnp.einsum('bclgn,bcgrpn->bclgrp', Cs, prev) * jnp.exp(cs)[..., None]
    y = y + xs * d_skip.astype(f32).reshape(G, R)[:, :, None]
    y = y.reshape(B, S, SSD_INNER) * jax.nn.silu(z.astype(f32))
    y = y.reshape(B, S, G, SSD_INNER // G)
    y = y * lax.rsqrt(jnp.mean(y * y, axis=-1, keepdims=True) + EPS)
    return (y.reshape(B, S, SSD_INNER) * norm_w.astype(f32)).astype(z.dtype)


def setup_inputs(seed: int = 0) -> dict:
    key = jax.random.key(seed)
    ks = iter(jax.random.split(key, 32))
    f32 = jnp.float32

    def nrm(shape, scale):
        return scale * jax.random.normal(next(ks), shape, f32)

    x = nrm((BATCH, SEQ, D_MODEL), 1.0)
    c = nrm((BATCH, D_MODEL), 1.0)
    w_ada = nrm((DEPTH, D_MODEL, 6 * D_MODEL), 0.5 * D_MODEL ** -0.5)
    b_ada = nrm((DEPTH, 6 * D_MODEL), 0.01)
    norm1_w = 1.0 + nrm((DEPTH, D_MODEL), 0.02)
    norm2_w = 1.0 + nrm((DEPTH, D_MODEL), 0.02)
    w_in = nrm((DEPTH, D_MODEL, IN_WIDTH), D_MODEL ** -0.5)
    gla_w_a2 = nrm((DEPTH, GLA_RANK, GLA_QK), GLA_RANK ** -0.5)
    gla_b_a = nrm((DEPTH, GLA_QK), 0.01)
    gla_norm_w = 1.0 + nrm((DEPTH, GLA_DV), 0.02)
    gm_ln_w = 1.0 + nrm((DEPTH, GM_W), 0.02)
    gm_ln_b = nrm((DEPTH, GM_W), 0.01)
    gm_ws = nrm((DEPTH, GM_GROUPS, GM_CHUNK, GM_CHUNK), GM_CHUNK ** -0.5)
    gm_bs = 1.0 + nrm((DEPTH, GM_GROUPS, GM_CHUNK), 0.01)
    ssd_conv_w = nrm((DEPTH, SSD_CONV, SSD_XBC), 0.5)
    ssd_conv_b = nrm((DEPTH, SSD_XBC), 0.01)
    dt0 = jnp.exp(jax.random.uniform(next(ks), (DEPTH, SSD_HEADS), f32, math.log(1e-3), math.log(1e-1)))
    ssd_dt_bias = dt0 + jnp.log(-jnp.expm1(-dt0))
    ssd_a_log = jnp.log(jax.random.uniform(next(ks), (DEPTH, SSD_HEADS), f32, 1.0, 16.0))
    ssd_d = 1.0 + nrm((DEPTH, SSD_HEADS), 0.1)
    ssd_norm_w = 1.0 + nrm((DEPTH, SSD_INNER), 0.02)
    w_br_gla = nrm((DEPTH, GLA_V, D_MODEL), GLA_V ** -0.5)
    w_br_dil = nrm((DEPTH, DIL_OUT, D_MODEL), DIL_OUT ** -0.5)
    w_br_gm = nrm((DEPTH, GM_W, D_MODEL), GM_W ** -0.5)
    w_br_ssd = nrm((DEPTH, SSD_INNER, D_MODEL), SSD_INNER ** -0.5)
    w_out = nrm((DEPTH, D_MODEL, D_MODEL), D_MODEL ** -0.5)
    w_mlp1 = nrm((DEPTH, D_MODEL, MLP_HIDDEN), D_MODEL ** -0.5)
    w_mlp2 = nrm((DEPTH, MLP_HIDDEN, D_MODEL), MLP_HIDDEN ** -0.5)
    final_norm_w = 1.0 + nrm((D_MODEL,), 0.02)
    return {'x': x, 'c': c, 'w_ada': w_ada, 'b_ada': b_ada, 'norm1_w': norm1_w, 'norm2_w': norm2_w,
            'w_in': w_in, 'gla_w_a2': gla_w_a2, 'gla_b_a': gla_b_a, 'gla_norm_w': gla_norm_w,
            'gm_ln_w': gm_ln_w, 'gm_ln_b': gm_ln_b, 'gm_ws': gm_ws, 'gm_bs': gm_bs,
            'ssd_conv_w': ssd_conv_w, 'ssd_conv_b': ssd_conv_b, 'ssd_dt_bias': ssd_dt_bias,
            'ssd_a_log': ssd_a_log, 'ssd_d': ssd_d, 'ssd_norm_w': ssd_norm_w,
            'w_br_gla': w_br_gla, 'w_br_dil': w_br_dil, 'w_br_gm': w_br_gm, 'w_br_ssd': w_br_ssd,
            'w_out': w_out, 'w_mlp1': w_mlp1, 'w_mlp2': w_mlp2, 'final_norm_w': final_norm_w}


def reference(x, c, w_ada, b_ada, norm1_w, norm2_w, w_in, gla_w_a2, gla_b_a, gla_norm_w,
              gm_ln_w, gm_ln_b, gm_ws, gm_bs, ssd_conv_w, ssd_conv_b, ssd_dt_bias, ssd_a_log,
              ssd_d, ssd_norm_w, w_br_gla, w_br_dil, w_br_gm, w_br_ssd, w_out, w_mlp1, w_mlp2,
              final_norm_w):
    B, S, D = x.shape
    for l in range(DEPTH):
        mod = jax.nn.silu(c) @ w_ada[l] + b_ada[l]
        sh1, sc1, g1, sh2, sc2, g2 = jnp.split(mod, 6, axis=-1)
        h = rms_norm(x, norm1_w[l]) * (1.0 + sc1[:, None]) + sh1[:, None]
        proj = h @ w_in[l]
        (gq, gk, gv, gg, glr, dq, dk, dv, gu, gmv, sz, sxbc, sdt, gates) = jnp.split(proj, SPLIT_POINTS, axis=-1)
        o_gla = gla_mixer(gq, gk, gv, gg, glr, gla_w_a2[l], gla_b_a[l], gla_norm_w[l])
        o_dil = dilated_mixer(dq, dk, dv)
        o_gm = gmlp_mixer(gu, gmv, gm_ln_w[l], gm_ln_b[l], gm_ws[l], gm_bs[l])
        o_ssd = ssd_mixer(sz, sxbc, sdt, ssd_conv_w[l], ssd_conv_b[l], ssd_dt_bias[l],
                          ssd_a_log[l], ssd_d[l], ssd_norm_w[l])
        gates = jax.nn.sigmoid(gates).reshape(B, S, N_BRANCH, D)
        merged = (gates[:, :, 0] * (o_gla @ w_br_gla[l]) + gates[:, :, 1] * (o_dil @ w_br_dil[l])
                  + gates[:, :, 2] * (o_gm @ w_br_gm[l]) + gates[:, :, 3] * (o_ssd @ w_br_ssd[l]))
        x = x + g1[:, None] * (merged @ w_out[l])
        h = rms_norm(x, norm2_w[l]) * (1.0 + sc2[:, None]) + sh2[:, None]
        x = x + g2[:, None] * (jnp.square(jax.nn.relu(h @ w_mlp1[l])) @ w_mlp2[l])
    return rms_norm(x, final_norm_w)
```

```python
import functools

import numpy as np
import jax
import jax.numpy as jnp
from jax import lax
from jax.experimental import pallas as pl
from jax.experimental.pallas import tpu as pltpu

F32 = jnp.float32
BF16 = jnp.bfloat16

D_MODEL = 1024
DEPTH = 4
EPS = 1e-6

GLA_HEADS, GLA_DK, GLA_DV, GLA_RANK, GLA_CHUNK = 4, 32, 64, 16, 64
GLA_GATE_NORM = 16.0
DIL_PATTERNS = ((128, 1), (512, 4), (2048, 16))
DIL_HEADS, DIL_DIM, DIL_STEPS = 4, 64, 128
GM_GROUPS, GM_DIM, GM_CHUNK = 4, 64, 128
SSD_HEADS, SSD_P, SSD_GROUPS, SSD_N, SSD_CONV, SSD_CHUNK = 8, 64, 2, 128, 4, 128
N_BRANCH = 4
MLP_HIDDEN = 4 * D_MODEL

GLA_QK = GLA_HEADS * GLA_DK
GLA_V = GLA_HEADS * GLA_DV
DIL_W = len(DIL_PATTERNS) * DIL_HEADS * DIL_DIM
DIL_OUT = DIL_HEADS * DIL_DIM
GM_W = GM_GROUPS * GM_DIM
SSD_INNER = SSD_HEADS * SSD_P
SSD_BC = SSD_GROUPS * SSD_N
SSD_XBC = SSD_INNER + 2 * SSD_BC
IN_SPLITS = (GLA_QK, GLA_QK, GLA_V, GLA_V, GLA_RANK, DIL_W, DIL_W, DIL_W, GM_W, GM_W,
             SSD_INNER, SSD_XBC, SSD_HEADS, N_BRANCH * D_MODEL)
_SP = tuple(int(v) for v in np.cumsum((0,) + IN_SPLITS))

LANE = 128
NEG = -1e30

PROJ_OUTS = (("gla", 2 * GLA_QK + 2 * GLA_V, BF16), ("glr", LANE, F32), ("dil", 3 * DIL_W, BF16),
             ("gm", 2 * GM_W, BF16), ("ssd", SSD_INNER + SSD_XBC, BF16), ("dt", LANE, F32),
             ("gates", N_BRANCH * D_MODEL, BF16))
PROJ_WIDTH = sum(w for _, w, _ in PROJ_OUTS)

TM_PROJ = 256
TM_MLP = 512
T_GLA = 256
T_GM = 512
T_SSD = 256
COL_CHUNK = 512


def _cparams(sem, vmem_mib):
    return pltpu.CompilerParams(dimension_semantics=sem, vmem_limit_bytes=vmem_mib * 1024 * 1024)


def _const_spec(shape):
    nd = len(shape)
    return pl.BlockSpec(shape, lambda *_: (0,) * nd, pipeline_mode=pl.Buffered(1))


def _dot(a, b):
    return jnp.dot(a.astype(BF16), b.astype(BF16), preferred_element_type=F32)


def _dot_nt(a, b):
    return lax.dot_general(a.astype(BF16), b.astype(BF16), (((1,), (1,)), ((), ())),
                           preferred_element_type=F32)


def _split(a):
    hi = a.astype(BF16)
    lo = (a - hi.astype(F32)).astype(BF16)
    return hi, lo


def _dot_sel_r(a, sel):
    hi, lo = _split(a)
    return (jnp.dot(hi, sel, preferred_element_type=F32) + jnp.dot(lo, sel, preferred_element_type=F32))


def _dot_sel_l(sel, a):
    hi, lo = _split(a)
    return (jnp.dot(sel, hi, preferred_element_type=F32) + jnp.dot(sel, lo, preferred_element_type=F32))


def _sigmoid(x):
    return 1.0 / (1.0 + jnp.exp(-x))


def _silu(x):
    return x * _sigmoid(x)


def _gelu(x):
    return 0.5 * x * (1.0 + lax.erf(x * (2.0 ** -0.5)))


def _softplus(x):
    return jnp.maximum(x, 0.0) + jnp.log1p(jnp.exp(-jnp.abs(x)))


def _modulated_norm(x, nw, sc, sh):
    y = x * lax.rsqrt(jnp.mean(x * x, axis=-1, keepdims=True) + EPS)
    return (y * nw) * (1.0 + sc) + sh


def _ada_kernel(c_ref, w_ref, b_ref, o_ref):
    o_ref[...] = _dot(_silu(c_ref[...]), w_ref[...]) + b_ref[...]


def _ada_mod(c, w_ada, b_ada):
    B, D = c.shape
    rows = 16
    c_pad = jnp.pad(c, ((0, rows - B), (0, 0)))
    n_col = w_ada.shape[-1] // D
    out = pl.pallas_call(
        _ada_kernel,
        grid=(DEPTH, n_col),
        in_specs=[pl.BlockSpec((rows, D), lambda l, j: (0, 0)),
                  pl.BlockSpec((None, D, D), lambda l, j: (l, 0, j)),
                  pl.BlockSpec((None, 1, D), lambda l, j: (l, 0, j))],
        out_specs=pl.BlockSpec((None, rows, D), lambda l, j: (l, 0, j)),
        out_shape=jax.ShapeDtypeStruct((DEPTH, rows, n_col * D), F32),
        compiler_params=_cparams(("arbitrary", "arbitrary"), 32),
        name="ada_mod",
    )(c_pad, w_ada, b_ada.reshape(DEPTH, 1, -1))
    return out[:, :B].reshape(DEPTH, B, n_col, 1, D)


def _inproj_kernel(x_ref, nw_ref, sc_ref, sh_ref, w_ref, *o_refs):
    h = _modulated_norm(x_ref[...], nw_ref[...], sc_ref[...], sh_ref[...]).astype(BF16)
    col = 0
    for o_ref, (_, width, dtype) in zip(o_refs, PROJ_OUTS):
        for c0 in range(0, width, COL_CHUNK):
            c1 = min(c0 + COL_CHUNK, width)
            o_ref[:, c0:c1] = jnp.dot(h, w_ref[:, col + c0:col + c1],
                                      preferred_element_type=F32).astype(dtype)
        col += width


def _inproj(x, nw, sc, sh, w_cat):
    B, S, D = x.shape
    vec = pl.BlockSpec((None, 1, D), lambda b, i: (b, 0, 0))
    return pl.pallas_call(
        _inproj_kernel,
        grid=(B, S // TM_PROJ),
        in_specs=[pl.BlockSpec((None, TM_PROJ, D), lambda b, i: (b, i, 0)),
                  _const_spec((1, D)), vec, vec, _const_spec((D, PROJ_WIDTH))],
        out_specs=[pl.BlockSpec((None, TM_PROJ, w), lambda b, i: (b, i, 0)) for _, w, _ in PROJ_OUTS],
        out_shape=[jax.ShapeDtypeStruct((B, S, w), dt) for _, w, dt in PROJ_OUTS],
        compiler_params=_cparams(("parallel", "parallel"), 56),
        name="inproj",
    )(x, nw, sc, sh, w_cat)


def _gla_kernel(gla_ref, glr_ref, wa_ref, ba_ref, nw_ref, tri_ref, ones_ref, o_ref, st_ref):
    C, H, K, V = GLA_CHUNK, GLA_HEADS, GLA_DK, GLA_DV
    T = gla_ref.shape[0]

    @pl.when(pl.program_id(1) == 0)
    def _():
        st_ref[...] = jnp.zeros_like(st_ref)

    blk = gla_ref[...]
    q = blk[:, 0:GLA_QK].astype(F32) * (K ** -0.5)
    k = blk[:, GLA_QK:2 * GLA_QK].astype(F32)
    v = blk[:, 2 * GLA_QK:2 * GLA_QK + GLA_V]
    g = blk[:, 2 * GLA_QK + GLA_V:].astype(F32)
    pre = _dot(glr_ref[...], wa_ref[...]) + ba_ref[...]
    log_a = -_softplus(-pre) * (1.0 / GLA_GATE_NORM)
    b_all = _dot_sel_l(tri_ref[...], log_a)

    k_head = lax.broadcasted_iota(jnp.int32, (C, GLA_QK), 1) // K
    v_head = lax.broadcasted_iota(jnp.int32, (C, GLA_V), 1) // V
    row = lax.broadcasted_iota(jnp.int32, (H * C, C), 0)
    causal = (row % C) >= lax.broadcasted_iota(jnp.int32, (H * C, C), 1)
    st_mask = (lax.broadcasted_iota(jnp.int32, (GLA_V, GLA_QK), 0) // V
               == lax.broadcasted_iota(jnp.int32, (GLA_V, GLA_QK), 1) // K)

    st = st_ref[...]
    outs = []
    for c in range(T // C):
        rs = slice(c * C, (c + 1) * C)
        b = b_all[rs]
        b_last = b[C - 1:C]
        q_t = q[rs] * jnp.exp(b)
        k_t = k[rs] * jnp.exp(-b)
        k_dec = k[rs] * jnp.exp(b_last - b)
        v_c = v[rs]
        q_stack = jnp.concatenate([jnp.where(k_head == h, q_t, 0.0) for h in range(H)], axis=0)
        att = jnp.where(causal, _dot_nt(q_stack, k_t), 0.0)
        o_all = _dot(att, v_c)
        o = _dot_nt(q_t, st)
        for h in range(H):
            o = o + jnp.where(v_head == h, o_all[h * C:(h + 1) * C], 0.0)
        kv_t = _dot(v_c.astype(F32).T, k_dec)
        st = st * jnp.exp(b_last) + jnp.where(st_mask, kv_t, 0.0)
        outs.append(o)
    st_ref[...] = st

    o = jnp.concatenate(outs, axis=0)
    ms = _dot_sel_r(o * o, ones_ref[...]) * (1.0 / V)
    o = o * lax.rsqrt(ms + EPS) * nw_ref[...]
    o_ref[...] = (o * _silu(g)).astype(o_ref.dtype)


def _gla(gla, glr, wa, ba, nw):
    B, S, W = gla.shape
    T, C = T_GLA, GLA_CHUNK
    idx = np.arange(T)
    tri = ((idx[:, None] // C == idx[None, :] // C) & (idx[:, None] >= idx[None, :]))
    hv = np.arange(GLA_V) // GLA_DV
    ones = hv[:, None] == hv[None, :]
    return pl.pallas_call(
        _gla_kernel,
        grid=(B, S // T),
        in_specs=[pl.BlockSpec((None, T, W), lambda b, i: (b, i, 0)),
                  pl.BlockSpec((None, T, LANE), lambda b, i: (b, i, 0)),
                  _const_spec((LANE, GLA_QK)), _const_spec((1, GLA_QK)), _const_spec((1, GLA_V)),
                  _const_spec((T, T)), _const_spec((GLA_V, GLA_V))],
        out_specs=pl.BlockSpec((None, T, GLA_V), lambda b, i: (b, i, 0)),
        out_shape=jax.ShapeDtypeStruct((B, S, GLA_V), BF16),
        scratch_shapes=[pltpu.VMEM((GLA_V, GLA_QK), F32)],
        compiler_params=_cparams(("parallel", "arbitrary"), 32),
        name="gla_mixer",
    )(gla, glr, wa, ba, nw, jnp.asarray(tri, BF16), jnp.asarray(ones, BF16))


def _dil_kernel(q_ref, kp_ref, k_ref, vp_ref, v_ref, bias_ref, o_ref, lse_ref):
    H, E, BLK = DIL_HEADS, DIL_DIM, DIL_STEPS
    q = q_ref[...] * jnp.asarray(E ** -0.5, BF16)
    kc = jnp.concatenate([kp_ref[...], k_ref[...]], axis=0)
    vc = jnp.concatenate([vp_ref[...], v_ref[...]], axis=0)
    head = lax.broadcasted_iota(jnp.int32, (BLK, H * E), 1) // E
    q_stack = jnp.concatenate([jnp.where(head == h, q, jnp.zeros_like(q)) for h in range(H)], axis=0)
    s = _dot_nt(q_stack, kc) + bias_ref[...]
    m = jnp.max(s, axis=-1, keepdims=True)
    p = jnp.exp(s - m)
    l = jnp.sum(p, axis=-1, keepdims=True)
    o_all = _dot(p, vc) / l
    lse = m + jnp.log(l)
    lane = lax.broadcasted_iota(jnp.int32, (BLK, LANE), 1)
    o = jnp.zeros((BLK, H * E), F32)
    lse_out = jnp.zeros((BLK, LANE), F32)
    for h in range(H):
        rs = slice(h * BLK, (h + 1) * BLK)
        o = o + jnp.where(head == h, o_all[rs], 0.0)
        lse_out = lse_out + jnp.where(lane == h, lse[rs], 0.0)
    o_ref[...] = o.astype(o_ref.dtype)
    lse_ref[...] = lse_out


def _dil_bias(window, dil, slopes):
    BLK = DIL_STEPS
    steps = np.arange(BLK)[:, None] + BLK - np.arange(2 * BLK)[None, :]
    ok = (steps >= 0) & (steps <= window // dil)
    bias = -slopes[:, None, None] * (steps * dil)[None].astype(np.float32)
    full = np.where(ok[None], bias, NEG).astype(np.float32)
    first = np.where((np.arange(2 * BLK) >= BLK)[None, None, :], full, NEG)
    return np.stack([first.reshape(-1, 2 * BLK), full.reshape(-1, 2 * BLK)]).astype(np.float32)


def _dil_pattern(dil_proj, p, window, dil, slopes):
    B, S, W3 = dil_proj.shape
    HE, BLK = DIL_OUT, DIL_STEPS
    n = S // dil
    nb = n // BLK
    per_tok = W3 // HE
    npat = len(DIL_PATTERNS)
    view = dil_proj.reshape(B, n, dil * W3)

    def col(t):
        return lambda b, r, j: (b, j, r * per_tok + t * npat + p)

    def col_prev(t):
        return lambda b, r, j: (b, jnp.maximum(j - 1, 0), r * per_tok + t * npat + p)

    blk = lambda im: pl.BlockSpec((None, BLK, HE), im)
    o, lse = pl.pallas_call(
        _dil_kernel,
        grid=(B, dil, nb),
        in_specs=[blk(col(0)), blk(col_prev(1)), blk(col(1)), blk(col_prev(2)), blk(col(2)),
                  pl.BlockSpec((None, DIL_HEADS * BLK, 2 * BLK), lambda b, r, j: (jnp.minimum(j, 1), 0, 0))],
        out_specs=[pl.BlockSpec((None, BLK, HE), lambda b, r, j: (b, j, r)),
                   pl.BlockSpec((None, BLK, LANE), lambda b, r, j: (b, j, r))],
        out_shape=[jax.ShapeDtypeStruct((B, n, dil * HE), BF16),
                   jax.ShapeDtypeStruct((B, n, dil * LANE), F32)],
        compiler_params=_cparams(("parallel", "parallel", "arbitrary"), 32),
        name=f"dil_attn_d{dil}",
    )(view, view, view, view, view, jnp.asarray(_dil_bias(window, dil, slopes)))
    return o.reshape(B, S, HE), lse.reshape(B, S, LANE)


def _dilated(dil_proj):
    assert dil_proj.shape[1] % (DIL_STEPS * max(d for _, d in DIL_PATTERNS)) == 0
    n_pat = len(DIL_PATTERNS)
    n_h = n_pat * DIL_HEADS
    slopes = (2.0 ** (-8.0 * np.arange(1, n_h + 1) / n_h)).astype(np.float32).reshape(n_pat, DIL_HEADS)
    outs = [_dil_pattern(dil_proj, p, w, d, slopes[p]) for p, (w, d) in enumerate(DIL_PATTERNS)]
    return [o for o, _ in outs], [l for _, l in outs]


def _gm_kernel(gm_ref, lnw_ref, lnb_ref, w_ref, bs_ref, o_ref):
    G, E, C = GM_GROUPS, GM_DIM, GM_CHUNK
    T = gm_ref.shape[0]
    blk = gm_ref[...].astype(F32)
    u = _gelu(blk[:, :GM_W])
    v = _gelu(blk[:, GM_W:])
    mu = jnp.mean(v, axis=-1, keepdims=True)
    var = jnp.mean(jnp.square(v - mu), axis=-1, keepdims=True)
    v = (v - mu) * lax.rsqrt(var + EPS) * lnw_ref[...] + lnb_ref[...]
    group = lax.broadcasted_iota(jnp.int32, (C, GM_W), 1) // E
    w = w_ref[...]
    bs = bs_ref[...]
    for c in range(T // C):
        rs = slice(c * C, (c + 1) * C)
        r = _dot(w, v[rs])
        sv = bs
        for gi in range(G):
            sv = sv + jnp.where(group == gi, r[gi * C:(gi + 1) * C], 0.0)
        o_ref[rs, :] = (u[rs] * sv).astype(o_ref.dtype)


def _gmlp(gm, lnw, lnb, ws, bs):
    B, S, W = gm.shape
    C = GM_CHUNK
    tril = jnp.tril(jnp.ones((C, C), bool))
    w_stack = jnp.where(tril, ws, 0.0).reshape(GM_GROUPS * C, C).astype(BF16)
    bs_exp = jnp.repeat(bs.T, GM_DIM, axis=1)
    return pl.pallas_call(
        _gm_kernel,
        grid=(B, S // T_GM),
        in_specs=[pl.BlockSpec((None, T_GM, W), lambda b, i: (b, i, 0)),
                  _const_spec((1, GM_W)), _const_spec((1, GM_W)),
                  _const_spec((GM_GROUPS * C, C)), _const_spec((C, GM_W))],
        out_specs=pl.BlockSpec((None, T_GM, GM_W), lambda b, i: (b, i, 0)),
        out_shape=jax.ShapeDtypeStruct((B, S, GM_W), BF16),
        compiler_params=_cparams(("parallel", "parallel"), 32),
        name="gmlp_mixer",
    )(gm, lnw, lnb, w_stack, bs_exp)


def _ssd_kernel(ssd_ref, dt_ref, cw_ref, cb_ref, dtb_ref, alog_ref, dskip_ref, nw_ref, tri_ref, exp_ref,
                o_ref, st_ref, tail_ref):
    H, P, G, N, L = SSD_HEADS, SSD_P, SSD_GROUPS, SSD_N, SSD_CHUNK
    R = H // G
    GW = R * P
    T = ssd_ref.shape[0]
    TAIL = tail_ref.shape[0]

    @pl.when(pl.program_id(1) == 0)
    def _():
        st_ref[...] = jnp.zeros_like(st_ref)
        tail_ref[...] = jnp.zeros_like(tail_ref)

    blk = ssd_ref[...]
    z = blk[:, :SSD_INNER].astype(F32)
    x_raw = blk[:, SSD_INNER:].astype(F32)
    x_pad = jnp.concatenate([tail_ref[...], x_raw], axis=0)
    tail_ref[...] = x_raw[T - TAIL:]
    conv = cb_ref[...]
    for j in range(SSD_CONV):
        off = TAIL - (SSD_CONV - 1) + j
        conv = conv + cw_ref[j:j + 1, :] * x_pad[off:off + T]
    xbc = _silu(conv)
    xs_all = xbc[:, :SSD_INNER]
    b_all = xbc[:, SSD_INNER:SSD_INNER + SSD_BC]
    c_all = xbc[:, SSD_INNER + SSD_BC:]
    dt_all = _softplus(dt_ref[...] + dtb_ref[...])
    da_all = dt_all * (-jnp.exp(alog_ref[...]))

    tri = tri_ref[...]
    expand = exp_ref[...]
    lower = lax.broadcasted_iota(jnp.int32, (L, L), 0) >= lax.broadcasted_iota(jnp.int32, (L, L), 1)
    head_in_group = lax.broadcasted_iota(jnp.int32, (L, GW), 1) // P

    for c in range(T // L):
        rs = slice(c * L, (c + 1) * L)
        xs, z_c, dt_c = xs_all[rs], z[rs], dt_all[rs]
        cs = _dot_sel_l(tri, da_all[rs])
        cs_t = cs.T
        cs_last = cs[L - 1:L]
        dt_e = _dot_sel_r(dt_c, expand)
        dec_e = _dot_sel_r(jnp.exp(cs_last - cs) * dt_c, expand)
        ecs_e = _dot_sel_r(jnp.exp(cs), expand)
        x_dt = (xs * dt_e).astype(BF16)
        x_dec = (xs * dec_e).astype(BF16)
        ys = []
        for gi in range(G):
            gs = slice(gi * GW, (gi + 1) * GW)
            b_g = b_all[rs, gi * N:(gi + 1) * N]
            c_g = c_all[rs, gi * N:(gi + 1) * N]
            cb = _dot_nt(c_g, b_g)
            y_g = _dot(c_g, st_ref[:, gs]) * ecs_e[:, gs]
            for r in range(R):
                h = gi * R + r
                seg = cs[:, h:h + 1] - cs_t[h:h + 1, :]
                wgt = cb * jnp.exp(jnp.where(lower, seg, NEG))
                y_h = _dot(wgt, x_dt[:, gs])
                y_g = y_g + jnp.where(head_in_group == r, y_h, 0.0)
            st_ref[:, gs] = st_ref[:, gs] * ecs_e[L - 1:L, gs] + _dot(b_g.T, x_dec[:, gs])
            ys.append(y_g)
        y = jnp.concatenate(ys, axis=1) + xs * dskip_ref[...]
        y = y * _silu(z_c)
        normed = []
        for gi in range(G):
            y_g = y[:, gi * GW:(gi + 1) * GW]
            normed.append(y_g * lax.rsqrt(jnp.mean(y_g * y_g, axis=-1, keepdims=True) + EPS))
        o_ref[rs, :] = (jnp.concatenate(normed, axis=1) * nw_ref[...]).astype(o_ref.dtype)


def _ssd(ssd, dt, conv_w, conv_b, dt_bias, a_log, d_skip, norm_w):
    B, S, W = ssd.shape
    L = SSD_CHUNK
    pad = lambda t: jnp.pad(t, (0, LANE - t.shape[0])).reshape(1, LANE)
    tri = np.tril(np.ones((L, L), np.float32))
    expand = np.zeros((LANE, SSD_INNER), np.float32)
    for h in range(SSD_HEADS):
        expand[h, h * SSD_P:(h + 1) * SSD_P] = 1.0
    return pl.pallas_call(
        _ssd_kernel,
        grid=(B, S // T_SSD),
        in_specs=[pl.BlockSpec((None, T_SSD, W), lambda b, i: (b, i, 0)),
                  pl.BlockSpec((None, T_SSD, LANE), lambda b, i: (b, i, 0)),
                  _const_spec((SSD_CONV, SSD_XBC)), _const_spec((1, SSD_XBC)),
                  _const_spec((1, LANE)), _const_spec((1, LANE)),
                  _const_spec((1, SSD_INNER)), _const_spec((1, SSD_INNER)),
                  _const_spec((L, L)), _const_spec((LANE, SSD_INNER))],
        out_specs=pl.BlockSpec((None, T_SSD, SSD_INNER), lambda b, i: (b, i, 0)),
        out_shape=jax.ShapeDtypeStruct((B, S, SSD_INNER), BF16),
        scratch_shapes=[pltpu.VMEM((SSD_N, SSD_INNER), F32), pltpu.VMEM((8, SSD_XBC), F32)],
        compiler_params=_cparams(("parallel", "arbitrary"), 48),
        name="ssd_mixer",
    )(ssd, dt, conv_w, conv_b.reshape(1, -1), pad(dt_bias), pad(a_log),
      jnp.repeat(d_skip, SSD_P).reshape(1, -1), norm_w.reshape(1, -1),
      jnp.asarray(tri, BF16), jnp.asarray(expand, BF16))


def _merge_kernel(x_ref, g1_ref, gla_ref, d0_ref, d1_ref, d2_ref, l0_ref, l1_ref, l2_ref, gm_ref, ssd_ref,
                  gates_ref, wg_ref, wd_ref, wm_ref, ws_ref, wo_ref, exp_ref, o_ref):
    D = D_MODEL
    l0, l1, l2 = l0_ref[...], l1_ref[...], l2_ref[...]
    m = jnp.maximum(jnp.maximum(l0, l1), l2)
    e0, e1, e2 = jnp.exp(l0 - m), jnp.exp(l1 - m), jnp.exp(l2 - m)
    den = e0 + e1 + e2
    expand = exp_ref[...]
    o_dil = (_dot_sel_r(e0 / den, expand) * d0_ref[...].astype(F32)
             + _dot_sel_r(e1 / den, expand) * d1_ref[...].astype(F32)
             + _dot_sel_r(e2 / den, expand) * d2_ref[...].astype(F32))
    branches = ((gla_ref[...], wg_ref), (o_dil, wd_ref), (gm_ref[...], wm_ref), (ssd_ref[...], ws_ref))
    merged = None
    for i, (o_b, w_ref) in enumerate(branches):
        gate = _sigmoid(gates_ref[:, i * D:(i + 1) * D].astype(F32))
        term = gate * _dot(o_b, w_ref[...])
        merged = term if merged is None else merged + term
    o_ref[...] = x_ref[...] + g1_ref[...] * _dot(merged, wo_ref[...])


def _merge(x, g1, o_gla, o_dil, lse_dil, o_gm, o_ssd, gates, w_gla, w_dil, w_gm, w_ssd, w_out):
    B, S, D = x.shape
    TM = TM_MLP
    tok = lambda w: pl.BlockSpec((None, TM, w), lambda b, i: (b, i, 0))
    expand = np.zeros((LANE, DIL_OUT), np.float32)
    for h in range(DIL_HEADS):
        expand[h, h * DIL_DIM:(h + 1) * DIL_DIM] = 1.0
    return pl.pallas_call(
        _merge_kernel,
        grid=(B, S // TM),
        in_specs=[tok(D), pl.BlockSpec((None, 1, D), lambda b, i: (b, 0, 0)),
                  tok(GLA_V), tok(DIL_OUT), tok(DIL_OUT), tok(DIL_OUT), tok(LANE), tok(LANE), tok(LANE),
                  tok(GM_W), tok(SSD_INNER), tok(N_BRANCH * D),
                  _const_spec((GLA_V, D)), _const_spec((DIL_OUT, D)), _const_spec((GM_W, D)),
                  _const_spec((SSD_INNER, D)), _const_spec((D, D)), _const_spec((LANE, DIL_OUT))],
        out_specs=tok(D),
        out_shape=jax.ShapeDtypeStruct((B, S, D), F32),
        compiler_params=_cparams(("parallel", "parallel"), 48),
        name="merge_out",
    )(x, g1, o_gla, *o_dil, *lse_dil, o_gm, o_ssd, gates, w_gla, w_dil, w_gm, w_ssd, w_out,
      jnp.asarray(expand, BF16))


def _mlp_kernel(x_ref, nw_ref, sc_ref, sh_ref, g2_ref, w1_ref, w2_ref, fw_ref, o_ref, *, final_norm):
    x = x_ref[...]
    h = _modulated_norm(x, nw_ref[...], sc_ref[...], sh_ref[...]).astype(BF16)
    acc = jnp.zeros(x.shape, F32)
    for c0 in range(0, MLP_HIDDEN, D_MODEL):
        a = jnp.maximum(jnp.dot(h, w1_ref[:, c0:c0 + D_MODEL], preferred_element_type=F32), 0.0)
        acc = acc + _dot(a * a, w2_ref[c0:c0 + D_MODEL, :])
    y = x + g2_ref[...] * acc
    if final_norm:
        y = y * lax.rsqrt(jnp.mean(y * y, axis=-1, keepdims=True) + EPS) * fw_ref[...]
    o_ref[...] = y


def _mlp(x, nw, sc, sh, g2, w1, w2, fw, final_norm):
    B, S, D = x.shape
    TM = TM_MLP
    tok = pl.BlockSpec((None, TM, D), lambda b, i: (b, i, 0))
    vec = pl.BlockSpec((None, 1, D), lambda b, i: (b, 0, 0))
    return pl.pallas_call(
        functools.partial(_mlp_kernel, final_norm=final_norm),
        grid=(B, S // TM),
        in_specs=[tok, _const_spec((1, D)), vec, vec, vec,
                  _const_spec((D, MLP_HIDDEN)), _const_spec((MLP_HIDDEN, D)), _const_spec((1, D))],
        out_specs=tok,
        out_shape=jax.ShapeDtypeStruct((B, S, D), F32),
        compiler_params=_cparams(("parallel", "parallel"), 56),
        name="mlp_final" if final_norm else "mlp",
    )(x, nw, sc, sh, g2, w1, w2, fw)


def _pack_w_in(w_in):
    def cols(a, b):
        return w_in[:, :, _SP[a]:_SP[b]]

    def pad(t):
        return jnp.pad(t, ((0, 0), (0, 0), (0, LANE - t.shape[-1])))

    parts = [cols(0, 4), pad(cols(4, 5)), cols(5, 8), cols(8, 10), cols(10, 12), pad(cols(12, 13)),
             cols(13, 14)]
    return jnp.concatenate(parts, axis=-1).astype(BF16)


def kernel(x, c, w_ada, b_ada, norm1_w, norm2_w, w_in, gla_w_a2, gla_b_a, gla_norm_w, gm_ln_w, gm_ln_b, gm_ws, gm_bs, ssd_conv_w, ssd_conv_b, ssd_dt_bias, ssd_a_log, ssd_d, ssd_norm_w, w_br_gla, w_br_dil, w_br_gm, w_br_ssd, w_out, w_mlp1, w_mlp2, final_norm_w):
    mod = _ada_mod(c, w_ada, b_ada)
    w_cat = _pack_w_in(w_in)
    wa = jnp.pad(gla_w_a2, ((0, 0), (0, LANE - GLA_RANK), (0, 0))).astype(BF16)
    w_gla, w_dil, w_gm, w_ssd, w_o, w1, w2 = (t.astype(BF16) for t in (
        w_br_gla, w_br_dil, w_br_gm, w_br_ssd, w_out, w_mlp1, w_mlp2))
    row = lambda t: t.reshape(1, -1)
    fw = row(final_norm_w)
    for l in range(DEPTH):
        sh1, sc1, g1, sh2, sc2, g2 = (mod[l, :, i] for i in range(6))
        gla, glr, dil, gm, ssd, dt, gates = _inproj(x, row(norm1_w[l]), sc1, sh1, w_cat[l])
        o_gla = _gla(gla, glr, wa[l], row(gla_b_a[l]), row(jnp.tile(gla_norm_w[l], GLA_HEADS)))
        o_dil, lse_dil = _dilated(dil)
        o_gm = _gmlp(gm, row(gm_ln_w[l]), row(gm_ln_b[l]), gm_ws[l], gm_bs[l])
        o_ssd = _ssd(ssd, dt, ssd_conv_w[l], ssd_conv_b[l], ssd_dt_bias[l], ssd_a_log[l], ssd_d[l],
                     ssd_norm_w[l])
        x = _merge(x, g1, o_gla, o_dil, lse_dil, o_gm, o_ssd, gates,
                   w_gla[l], w_dil[l], w_gm[l], w_ssd[l], w_o[l])
        x = _mlp(x, row(norm2_w[l]), sc2, sh2, g2, w1[l], w2[l], fw, final_norm=(l == DEPTH - 1))
    return x
```

```python
import functools

import numpy as np
import jax
import jax.numpy as jnp
from jax import lax
from jax.experimental import pallas as pl
from jax.experimental.pallas import tpu as pltpu

F32 = jnp.float32
BF16 = jnp.bfloat16

D_MODEL = 1024
DEPTH = 4
EPS = 1e-6

GLA_HEADS, GLA_DK, GLA_DV, GLA_RANK, GLA_CHUNK = 4, 32, 64, 16, 64
GLA_GATE_NORM = 16.0
DIL_PATTERNS = ((128, 1), (512, 4), (2048, 16))
DIL_HEADS, DIL_DIM, DIL_STEPS = 4, 64, 128
GM_GROUPS, GM_DIM, GM_CHUNK = 4, 64, 128
SSD_HEADS, SSD_P, SSD_GROUPS, SSD_N, SSD_CONV, SSD_CHUNK = 8, 64, 2, 128, 4, 128
N_BRANCH = 4
MLP_HIDDEN = 4 * D_MODEL

GLA_QK = GLA_HEADS * GLA_DK
GLA_V = GLA_HEADS * GLA_DV
DIL_W = len(DIL_PATTERNS) * DIL_HEADS * DIL_DIM
DIL_OUT = DIL_HEADS * DIL_DIM
GM_W = GM_GROUPS * GM_DIM
SSD_INNER = SSD_HEADS * SSD_P
SSD_BC = SSD_GROUPS * SSD_N
SSD_XBC = SSD_INNER + 2 * SSD_BC
IN_SPLITS = (GLA_QK, GLA_QK, GLA_V, GLA_V, GLA_RANK, DIL_W, DIL_W, DIL_W, GM_W, GM_W,
             SSD_INNER, SSD_XBC, SSD_HEADS, N_BRANCH * D_MODEL)
_SP = tuple(int(v) for v in np.cumsum((0,) + IN_SPLITS))

LANE = 128
NEG = -1e30

PROJ_OUTS = (("gla", 2 * GLA_QK + 2 * GLA_V, BF16, 1), ("glr", LANE, F32, 1),
             ("dil0", 3 * DIL_OUT, BF16, DIL_PATTERNS[0][1]), ("dil1", 3 * DIL_OUT, BF16, DIL_PATTERNS[1][1]),
             ("dil2", 3 * DIL_OUT, BF16, DIL_PATTERNS[2][1]),
             ("gm", 2 * GM_W, BF16, 1), ("ssd", SSD_INNER + SSD_XBC, BF16, 1), ("dt", LANE, F32, 1),
             ("gates", N_BRANCH * D_MODEL, BF16, 1))
PROJ_WIDTH = sum(o[1] for o in PROJ_OUTS)

TM_PROJ = 256
TM_MLP = 512
T_GLA = 256
T_GM = 512
T_SSD = 256
COL_CHUNK = 512
DIL_QBLOCKS = 4


def _cparams(sem, vmem_mib):
    return pltpu.CompilerParams(dimension_semantics=sem, vmem_limit_bytes=vmem_mib * 1024 * 1024)


def _const_spec(shape):
    nd = len(shape)
    return pl.BlockSpec(shape, lambda *_: (0,) * nd, pipeline_mode=pl.Buffered(1))


def _layer_spec(shape, layer):
    nd = len(shape)
    return pl.BlockSpec((None,) + tuple(shape), lambda *_: (layer,) + (0,) * nd,
                        pipeline_mode=pl.Buffered(1))


def _dot(a, b):
    return jnp.dot(a.astype(BF16), b.astype(BF16), preferred_element_type=F32)


def _dot_nt(a, b):
    return lax.dot_general(a.astype(BF16), b.astype(BF16), (((1,), (1,)), ((), ())),
                           preferred_element_type=F32)


def _split(a):
    hi = a.astype(BF16)
    lo = (a - hi.astype(F32)).astype(BF16)
    return hi, lo


def _dot_sel_r(a, sel):
    hi, lo = _split(a)
    return (jnp.dot(hi, sel, preferred_element_type=F32) + jnp.dot(lo, sel, preferred_element_type=F32))


def _dot_sel_l(sel, a):
    hi, lo = _split(a)
    return (jnp.dot(sel, hi, preferred_element_type=F32) + jnp.dot(sel, lo, preferred_element_type=F32))


def _sigmoid(x):
    return 1.0 / (1.0 + jnp.exp(-x))


def _silu(x):
    return x * _sigmoid(x)


def _gelu(x):
    return 0.5 * x * (1.0 + lax.erf(x * (2.0 ** -0.5)))


def _softplus(x):
    return jnp.maximum(x, 0.0) + jnp.log1p(jnp.exp(-jnp.abs(x)))


def _modulated_norm(x, nw, sc, sh):
    y = x * lax.rsqrt(jnp.mean(x * x, axis=-1, keepdims=True) + EPS)
    return (y * nw) * (1.0 + sc) + sh


def _ada_kernel(c_ref, w_ref, b_ref, o_ref):
    o_ref[...] = _dot(_silu(c_ref[...]), w_ref[...]) + b_ref[...]


def _ada_mod(c, w_ada, b_ada):
    B, D = c.shape
    rows = 16
    c_pad = jnp.pad(c, ((0, rows - B), (0, 0)))
    n_col = w_ada.shape[-1] // D
    out = pl.pallas_call(
        _ada_kernel,
        grid=(DEPTH, n_col),
        in_specs=[pl.BlockSpec((rows, D), lambda l, j: (0, 0)),
                  pl.BlockSpec((None, D, D), lambda l, j: (l, 0, j)),
                  pl.BlockSpec((None, 1, D), lambda l, j: (l, 0, j))],
        out_specs=pl.BlockSpec((None, rows, D), lambda l, j: (l, 0, j)),
        out_shape=jax.ShapeDtypeStruct((DEPTH, rows, n_col * D), F32),
        compiler_params=_cparams(("arbitrary", "arbitrary"), 32),
        name="ada_mod",
    )(c_pad, w_ada, b_ada.reshape(DEPTH, 1, -1))
    return out[:, :B].reshape(DEPTH, B, n_col, 1, D)


def _inproj_kernel(x_ref, nw_ref, sc_ref, sh_ref, w_ref, *refs):
    o_refs, stage_ref = refs[:-1], refs[-1]
    TM = x_ref.shape[0]
    h = _modulated_norm(x_ref[...], nw_ref[...], sc_ref[...], sh_ref[...]).astype(BF16)
    col = 0
    for o_ref, (_, width, dtype, dil) in zip(o_refs, PROJ_OUTS):
        for c0 in range(0, width, COL_CHUNK):
            c1 = min(c0 + COL_CHUNK, width)
            res = jnp.dot(h, w_ref[:, col + c0:col + c1], preferred_element_type=F32)
            if dil == 1:
                o_ref[:, c0:c1] = res.astype(dtype)
            else:
                for t in range((c1 - c0) // LANE):
                    stage_ref[c0 // LANE + t] = res[:, t * LANE:(t + 1) * LANE]
        if dil > 1:
            for r in range(dil):
                for t in range(width // LANE):
                    o_ref[r, :, t * LANE:(t + 1) * LANE] = stage_ref[
                        t, pl.ds(r, TM // dil, stride=dil), :].astype(dtype)
        col += width


def _inproj(x, nw, mod, w_cat, layer):
    B, S, D = x.shape
    TM = TM_PROJ
    out_specs, out_shape = [], []
    for _, w, dt, dil in PROJ_OUTS:
        if dil == 1:
            out_specs.append(pl.BlockSpec((None, TM, w), lambda b, i: (b, i, 0)))
            out_shape.append(jax.ShapeDtypeStruct((B, S, w), dt))
        else:
            out_specs.append(pl.BlockSpec((None, dil, TM // dil, w), lambda b, i: (b, 0, i, 0)))
            out_shape.append(jax.ShapeDtypeStruct((B, dil, S // dil, w), dt))
    return pl.pallas_call(
        _inproj_kernel,
        grid=(B, S // TM),
        in_specs=[pl.BlockSpec((None, TM, D), lambda b, i: (b, i, 0)),
                  _layer_spec((1, D), layer), _mod_spec(layer, 1), _mod_spec(layer, 0),
                  _layer_spec((D, PROJ_WIDTH), layer)],
        out_specs=out_specs,
        out_shape=out_shape,
        scratch_shapes=[pltpu.VMEM((3 * DIL_OUT // LANE, TM, LANE), F32)],
        compiler_params=_cparams(("parallel", "parallel"), 56),
        name="inproj",
    )(x, nw, mod, mod, w_cat)


def _gla_kernel(gla_ref, glr_ref, wa_ref, ba_ref, nw_ref, tri_ref, ones_ref, o_ref, st_ref):
    C, H, K, V = GLA_CHUNK, GLA_HEADS, GLA_DK, GLA_DV
    T = gla_ref.shape[0]

    @pl.when(pl.program_id(1) == 0)
    def _():
        st_ref[...] = jnp.zeros_like(st_ref)

    blk = gla_ref[...]
    q = blk[:, 0:GLA_QK].astype(F32) * (K ** -0.5)
    k = blk[:, GLA_QK:2 * GLA_QK].astype(F32)
    v = blk[:, 2 * GLA_QK:2 * GLA_QK + GLA_V]
    g = blk[:, 2 * GLA_QK + GLA_V:].astype(F32)
    pre = _dot(glr_ref[...], wa_ref[...]) + ba_ref[...]
    log_a = -_softplus(-pre) * (1.0 / GLA_GATE_NORM)
    b_all = _dot_sel_l(tri_ref[...], log_a)

    k_head = lax.broadcasted_iota(jnp.int32, (C, GLA_QK), 1) // K
    v_head = lax.broadcasted_iota(jnp.int32, (C, GLA_V), 1) // V
    row = lax.broadcasted_iota(jnp.int32, (H * C, C), 0)
    causal = (row % C) >= lax.broadcasted_iota(jnp.int32, (H * C, C), 1)
    st_mask = (lax.broadcasted_iota(jnp.int32, (GLA_V, GLA_QK), 0) // V
               == lax.broadcasted_iota(jnp.int32, (GLA_V, GLA_QK), 1) // K)

    st = st_ref[...]
    outs = []
    for c in range(T // C):
        rs = slice(c * C, (c + 1) * C)
        b = b_all[rs]
        b_last = b[C - 1:C]
        q_t = q[rs] * jnp.exp(b)
        k_t = k[rs] * jnp.exp(-b)
        k_dec = k[rs] * jnp.exp(b_last - b)
        v_c = v[rs]
        q_stack = jnp.concatenate([jnp.where(k_head == h, q_t, 0.0) for h in range(H)], axis=0)
        att = jnp.where(causal, _dot_nt(q_stack, k_t), 0.0)
        o_all = _dot(att, v_c)
        o = _dot_nt(q_t, st)
        for h in range(H):
            o = o + jnp.where(v_head == h, o_all[h * C:(h + 1) * C], 0.0)
        kv_t = _dot(v_c.astype(F32).T, k_dec)
        st = st * jnp.exp(b_last) + jnp.where(st_mask, kv_t, 0.0)
        outs.append(o)
    st_ref[...] = st

    o = jnp.concatenate(outs, axis=0)
    ms = _dot_sel_r(o * o, ones_ref[...]) * (1.0 / V)
    o = o * lax.rsqrt(ms + EPS) * nw_ref[...]
    o_ref[...] = (o * _silu(g)).astype(o_ref.dtype)


def _gla(gla, glr, wa, ba, nw, layer):
    B, S, W = gla.shape
    T, C = T_GLA, GLA_CHUNK
    idx = np.arange(T)
    tri = ((idx[:, None] // C == idx[None, :] // C) & (idx[:, None] >= idx[None, :]))
    hv = np.arange(GLA_V) // GLA_DV
    ones = hv[:, None] == hv[None, :]
    return pl.pallas_call(
        _gla_kernel,
        grid=(B, S // T),
        in_specs=[pl.BlockSpec((None, T, W), lambda b, i: (b, i, 0)),
                  pl.BlockSpec((None, T, LANE), lambda b, i: (b, i, 0)),
                  _layer_spec((LANE, GLA_QK), layer), _layer_spec((1, GLA_QK), layer),
                  _layer_spec((1, GLA_V), layer), _const_spec((T, T)), _const_spec((GLA_V, GLA_V))],
        out_specs=pl.BlockSpec((None, T, GLA_V), lambda b, i: (b, i, 0)),
        out_shape=jax.ShapeDtypeStruct((B, S, GLA_V), BF16),
        scratch_shapes=[pltpu.VMEM((GLA_V, GLA_QK), F32)],
        compiler_params=_cparams(("parallel", "arbitrary"), 32),
        name="gla_mixer",
    )(gla, glr, wa, ba, nw, jnp.asarray(tri, BF16), jnp.asarray(ones, BF16))


def _dil_kernel(q_ref, kp_ref, k_ref, vp_ref, v_ref, bias_ref, o_ref, lse_ref):
    H, E, BLK = DIL_HEADS, DIL_DIM, DIL_STEPS
    n_blk = q_ref.shape[0] // BLK
    first = jnp.minimum(pl.program_id(2), 1)
    kcat = jnp.concatenate([kp_ref[...], k_ref[...]], axis=0)
    vcat = jnp.concatenate([vp_ref[...], v_ref[...]], axis=0)
    head = lax.broadcasted_iota(jnp.int32, (BLK, H * E), 1) // E
    lane = lax.broadcasted_iota(jnp.int32, (BLK, LANE), 1)
    for i in range(n_blk):
        q = q_ref[i * BLK:(i + 1) * BLK, :] * jnp.asarray(E ** -0.5, BF16)
        kc = kcat[i * BLK:(i + 2) * BLK]
        vc = vcat[i * BLK:(i + 2) * BLK]
        bias = bias_ref[first] if i == 0 else bias_ref[1]
        q_stack = jnp.concatenate([jnp.where(head == h, q, jnp.zeros_like(q)) for h in range(H)], axis=0)
        s = _dot_nt(q_stack, kc) + bias
        m = jnp.max(s, axis=-1, keepdims=True)
        p = jnp.exp(s - m)
        l = jnp.sum(p, axis=-1, keepdims=True)
        o_all = _dot(p, vc) / l
        lse = m + jnp.log(l)
        o = jnp.zeros((BLK, H * E), F32)
        lse_out = jnp.zeros((BLK, LANE), F32)
        for h in range(H):
            rs = slice(h * BLK, (h + 1) * BLK)
            o = o + jnp.where(head == h, o_all[rs], 0.0)
            lse_out = lse_out + jnp.where(lane == h, lse[rs], 0.0)
        o_ref[i * BLK:(i + 1) * BLK, :] = o.astype(o_ref.dtype)
        lse_ref[i * BLK:(i + 1) * BLK, :] = lse_out


def _dil_bias(window, dil, slopes):
    BLK = DIL_STEPS
    steps = np.arange(BLK)[:, None] + BLK - np.arange(2 * BLK)[None, :]
    ok = (steps >= 0) & (steps <= window // dil)
    bias = -slopes[:, None, None] * (steps * dil)[None].astype(np.float32)
    full = np.where(ok[None], bias, NEG).astype(np.float32)
    first = np.where((np.arange(2 * BLK) >= BLK)[None, None, :], full, NEG)
    return np.stack([first.reshape(-1, 2 * BLK), full.reshape(-1, 2 * BLK)]).astype(np.float32)


def _dil_pattern(qkv, window, dil, slopes):
    B, _, n, _ = qkv.shape
    HE, BLK = DIL_OUT, DIL_STEPS
    n_blk = min(DIL_QBLOCKS, n // BLK)
    rows = n_blk * BLK
    cur = lambda t: pl.BlockSpec((None, None, rows, HE), lambda b, r, j: (b, r, j, t))
    prev = lambda t: pl.BlockSpec((None, None, BLK, HE),
                                  lambda b, r, j: (b, r, jnp.maximum(j * n_blk - 1, 0), t))
    return pl.pallas_call(
        _dil_kernel,
        grid=(B, dil, n // rows),
        in_specs=[cur(0), prev(1), cur(1), prev(2), cur(2),
                  _const_spec((2, DIL_HEADS * BLK, 2 * BLK))],
        out_specs=[pl.BlockSpec((None, None, rows, HE), lambda b, r, j: (b, r, j, 0)),
                   pl.BlockSpec((None, None, rows, LANE), lambda b, r, j: (b, r, j, 0))],
        out_shape=[jax.ShapeDtypeStruct((B, dil, n, HE), BF16),
                   jax.ShapeDtypeStruct((B, dil, n, LANE), F32)],
        compiler_params=_cparams(("parallel", "parallel", "arbitrary"), 32),
        name=f"dil_attn_d{dil}",
    )(qkv, qkv, qkv, qkv, qkv, jnp.asarray(_dil_bias(window, dil, slopes)))


def _dilated(qkvs):
    n_pat = len(DIL_PATTERNS)
    n_h = n_pat * DIL_HEADS
    slopes = (2.0 ** (-8.0 * np.arange(1, n_h + 1) / n_h)).astype(np.float32).reshape(n_pat, DIL_HEADS)
    outs = []
    for p, (w, d) in enumerate(DIL_PATTERNS):
        qkv = qkvs[p] if qkvs[p].ndim == 4 else qkvs[p][:, None]
        assert qkv.shape[1] == d and qkv.shape[2] % (DIL_STEPS * min(DIL_QBLOCKS, qkv.shape[2] // DIL_STEPS)) == 0
        outs.append(_dil_pattern(qkv, w, d, slopes[p]))
    return [o for o, _ in outs], [l for _, l in outs]


def _gm_kernel(gm_ref, lnw_ref, lnb_ref, w_ref, bs_ref, o_ref):
    G, E, C = GM_GROUPS, GM_DIM, GM_CHUNK
    T = gm_ref.shape[0]
    blk = gm_ref[...].astype(F32)
    u = _gelu(blk[:, :GM_W])
    v = _gelu(blk[:, GM_W:])
    mu = jnp.mean(v, axis=-1, keepdims=True)
    var = jnp.mean(jnp.square(v - mu), axis=-1, keepdims=True)
    v = (v - mu) * lax.rsqrt(var + EPS) * lnw_ref[...] + lnb_ref[...]
    group = lax.broadcasted_iota(jnp.int32, (C, GM_W), 1) // E
    w = w_ref[...]
    bs = bs_ref[...]
    for c in range(T // C):
        rs = slice(c * C, (c + 1) * C)
        r = _dot(w, v[rs])
        sv = bs
        for gi in range(G):
            sv = sv + jnp.where(group == gi, r[gi * C:(gi + 1) * C], 0.0)
        o_ref[rs, :] = (u[rs] * sv).astype(o_ref.dtype)


def _gmlp_params(ws, bs):
    C = GM_CHUNK
    tril = jnp.tril(jnp.ones((C, C), bool))
    w_stack = jnp.where(tril, ws, 0.0).reshape(-1, GM_GROUPS * C, C).astype(BF16)
    bs_exp = jnp.repeat(jnp.swapaxes(bs, 1, 2), GM_DIM, axis=2)
    return w_stack, bs_exp


def _gmlp(gm, lnw, lnb, w_stack, bs_exp, layer):
    B, S, W = gm.shape
    C = GM_CHUNK
    return pl.pallas_call(
        _gm_kernel,
        grid=(B, S // T_GM),
        in_specs=[pl.BlockSpec((None, T_GM, W), lambda b, i: (b, i, 0)),
                  _layer_spec((1, GM_W), layer), _layer_spec((1, GM_W), layer),
                  _layer_spec((GM_GROUPS * C, C), layer), _layer_spec((C, GM_W), layer)],
        out_specs=pl.BlockSpec((None, T_GM, GM_W), lambda b, i: (b, i, 0)),
        out_shape=jax.ShapeDtypeStruct((B, S, GM_W), BF16),
        compiler_params=_cparams(("parallel", "parallel"), 32),
        name="gmlp_mixer",
    )(gm, lnw, lnb, w_stack, bs_exp)


def _ssd_kernel(ssd_ref, dt_ref, cw_ref, cb_ref, dtb_ref, alog_ref, dskip_ref, nw_ref, tri_ref, exp_ref,
                o_ref, st_ref, tail_ref):
    H, P, G, N, L = SSD_HEADS, SSD_P, SSD_GROUPS, SSD_N, SSD_CHUNK
    R = H // G
    GW = R * P
    T = ssd_ref.shape[0]
    TAIL = tail_ref.shape[0]

    @pl.when(pl.program_id(1) == 0)
    def _():
        st_ref[...] = jnp.zeros_like(st_ref)
        tail_ref[...] = jnp.zeros_like(tail_ref)

    blk = ssd_ref[...]
    z = blk[:, :SSD_INNER].astype(F32)
    x_raw = blk[:, SSD_INNER:].astype(F32)
    x_pad = jnp.concatenate([tail_ref[...], x_raw], axis=0)
    tail_ref[...] = x_raw[T - TAIL:]
    conv = cb_ref[...]
    for j in range(SSD_CONV):
        off = TAIL - (SSD_CONV - 1) + j
        conv = conv + cw_ref[j:j + 1, :] * x_pad[off:off + T]
    xbc = _silu(conv)
    xs_all = xbc[:, :SSD_INNER]
    b_all = xbc[:, SSD_INNER:SSD_INNER + SSD_BC]
    c_all = xbc[:, SSD_INNER + SSD_BC:]
    dt_all = _softplus(dt_ref[...] + dtb_ref[...])
    da_all = dt_all * (-jnp.exp(alog_ref[...]))

    tri = tri_ref[...]
    expand = exp_ref[...]
    lower = lax.broadcasted_iota(jnp.int32, (L, L), 0) >= lax.broadcasted_iota(jnp.int32, (L, L), 1)
    head_in_group = lax.broadcasted_iota(jnp.int32, (L, GW), 1) // P

    for c in range(T // L):
        rs = slice(c * L, (c + 1) * L)
        xs, z_c, dt_c = xs_all[rs], z[rs], dt_all[rs]
        cs = _dot_sel_l(tri, da_all[rs])
        cs_t = cs.T
        cs_last = cs[L - 1:L]
        dt_e = _dot_sel_r(dt_c, expand)
        dec_e = _dot_sel_r(jnp.exp(cs_last - cs) * dt_c, expand)
        ecs_e = _dot_sel_r(jnp.exp(cs), expand)
        x_dt = (xs * dt_e).astype(BF16)
        x_dec = (xs * dec_e).astype(BF16)
        ys = []
        for gi in range(G):
            gs = slice(gi * GW, (gi + 1) * GW)
            b_g = b_all[rs, gi * N:(gi + 1) * N]
            c_g = c_all[rs, gi * N:(gi + 1) * N]
            cb = _dot_nt(c_g, b_g)
            y_g = _dot(c_g, st_ref[:, gs]) * ecs_e[:, gs]
            for r in range(R):
                h = gi * R + r
                seg = cs[:, h:h + 1] - cs_t[h:h + 1, :]
                wgt = cb * jnp.exp(jnp.where(lower, seg, NEG))
                y_h = _dot(wgt, x_dt[:, gs])
                y_g = y_g + jnp.where(head_in_group == r, y_h, 0.0)
            st_ref[:, gs] = st_ref[:, gs] * ecs_e[L - 1:L, gs] + _dot(b_g.T, x_dec[:, gs])
            ys.append(y_g)
        y = jnp.concatenate(ys, axis=1) + xs * dskip_ref[...]
        y = y * _silu(z_c)
        normed = []
        for gi in range(G):
            y_g = y[:, gi * GW:(gi + 1) * GW]
            normed.append(y_g * lax.rsqrt(jnp.mean(y_g * y_g, axis=-1, keepdims=True) + EPS))
        o_ref[rs, :] = (jnp.concatenate(normed, axis=1) * nw_ref[...]).astype(o_ref.dtype)


def _ssd_params(conv_b, dt_bias, a_log, d_skip, norm_w):
    pad = lambda t: jnp.pad(t, ((0, 0), (0, LANE - t.shape[1])))[:, None, :]
    return (conv_b[:, None, :], pad(dt_bias), pad(a_log), jnp.repeat(d_skip, SSD_P, axis=1)[:, None, :],
            norm_w[:, None, :])


def _ssd(ssd, dt, conv_w, params, layer):
    B, S, W = ssd.shape
    L = SSD_CHUNK
    tri = np.tril(np.ones((L, L), np.float32))
    expand = np.zeros((LANE, SSD_INNER), np.float32)
    for h in range(SSD_HEADS):
        expand[h, h * SSD_P:(h + 1) * SSD_P] = 1.0
    return pl.pallas_call(
        _ssd_kernel,
        grid=(B, S // T_SSD),
        in_specs=[pl.BlockSpec((None, T_SSD, W), lambda b, i: (b, i, 0)),
                  pl.BlockSpec((None, T_SSD, LANE), lambda b, i: (b, i, 0)),
                  _layer_spec((SSD_CONV, SSD_XBC), layer), _layer_spec((1, SSD_XBC), layer),
                  _layer_spec((1, LANE), layer), _layer_spec((1, LANE), layer),
                  _layer_spec((1, SSD_INNER), layer), _layer_spec((1, SSD_INNER), layer),
                  _const_spec((L, L)), _const_spec((LANE, SSD_INNER))],
        out_specs=pl.BlockSpec((None, T_SSD, SSD_INNER), lambda b, i: (b, i, 0)),
        out_shape=jax.ShapeDtypeStruct((B, S, SSD_INNER), BF16),
        scratch_shapes=[pltpu.VMEM((SSD_N, SSD_INNER), F32), pltpu.VMEM((8, SSD_XBC), F32)],
        compiler_params=_cparams(("parallel", "arbitrary"), 48),
        name="ssd_mixer",
    )(ssd, dt, conv_w, *params, jnp.asarray(tri, BF16), jnp.asarray(expand, BF16))


def _token_order(src_ref, stage_ref):
    dil, per, w = src_ref.shape
    for r in range(dil):
        plane = src_ref[r].astype(F32)
        for t in range(w // LANE):
            stage_ref[t, pl.ds(r, per, stride=dil), :] = plane[:, t * LANE:(t + 1) * LANE]
    return jnp.concatenate([stage_ref[t] for t in range(w // LANE)], axis=1)


def _merge_kernel(x_ref, g1_ref, gla_ref, d0_ref, d1_ref, d2_ref, l0_ref, l1_ref, l2_ref, gm_ref, ssd_ref,
                  gates_ref, wg_ref, wd_ref, wm_ref, ws_ref, wo_ref, exp_ref, o_ref,
                  od1_ref, od2_ref, ol1_ref, ol2_ref):
    D = D_MODEL
    d0, l0 = d0_ref[...].astype(F32), l0_ref[...]
    d1, l1 = _token_order(d1_ref, od1_ref), _token_order(l1_ref, ol1_ref)
    d2, l2 = _token_order(d2_ref, od2_ref), _token_order(l2_ref, ol2_ref)
    m = jnp.maximum(jnp.maximum(l0, l1), l2)
    e0, e1, e2 = jnp.exp(l0 - m), jnp.exp(l1 - m), jnp.exp(l2 - m)
    den = e0 + e1 + e2
    expand = exp_ref[...]
    o_dil = (_dot_sel_r(e0 / den, expand) * d0 + _dot_sel_r(e1 / den, expand) * d1
             + _dot_sel_r(e2 / den, expand) * d2)
    branches = ((gla_ref[...], wg_ref), (o_dil, wd_ref), (gm_ref[...], wm_ref), (ssd_ref[...], ws_ref))
    merged = None
    for i, (o_b, w_ref) in enumerate(branches):
        gate = _sigmoid(gates_ref[:, i * D:(i + 1) * D].astype(F32))
        term = gate * _dot(o_b, w_ref[...])
        merged = term if merged is None else merged + term
    o_ref[...] = x_ref[...] + g1_ref[...] * _dot(merged, wo_ref[...])


def _mod_spec(layer, which):
    return pl.BlockSpec((None, None, None, 1, D_MODEL), lambda b, i: (layer, b, which, 0, 0))


def _merge(x, mod, o_gla, o_dil, lse_dil, o_gm, o_ssd, gates, w_gla, w_dil, w_gm, w_ssd, w_out, layer):
    B, S, D = x.shape
    TM = TM_MLP
    tok = lambda w: pl.BlockSpec((None, TM, w), lambda b, i: (b, i, 0))

    def plane(w, dil):
        if dil == 1:
            return pl.BlockSpec((None, None, TM, w), lambda b, i: (b, 0, i, 0))
        return pl.BlockSpec((None, dil, TM // dil, w), lambda b, i: (b, 0, i, 0))

    dils = [d for _, d in DIL_PATTERNS]
    assert dils[0] == 1 and len(dils) == 3
    expand = np.zeros((LANE, DIL_OUT), np.float32)
    for h in range(DIL_HEADS):
        expand[h, h * DIL_DIM:(h + 1) * DIL_DIM] = 1.0
    return pl.pallas_call(
        _merge_kernel,
        grid=(B, S // TM),
        in_specs=[tok(D), _mod_spec(layer, 2), tok(GLA_V)]
                 + [plane(DIL_OUT, d) for d in dils] + [plane(LANE, d) for d in dils]
                 + [tok(GM_W), tok(SSD_INNER), tok(N_BRANCH * D),
                    _layer_spec((GLA_V, D), layer), _layer_spec((DIL_OUT, D), layer),
                    _layer_spec((GM_W, D), layer), _layer_spec((SSD_INNER, D), layer),
                    _layer_spec((D, D), layer), _const_spec((LANE, DIL_OUT))],
        out_specs=tok(D),
        out_shape=jax.ShapeDtypeStruct((B, S, D), F32),
        scratch_shapes=[pltpu.VMEM((DIL_OUT // LANE, TM, LANE), F32), pltpu.VMEM((DIL_OUT // LANE, TM, LANE), F32),
                        pltpu.VMEM((1, TM, LANE), F32), pltpu.VMEM((1, TM, LANE), F32)],
        compiler_params=_cparams(("parallel", "parallel"), 48),
        name="merge_out",
    )(x, mod, o_gla, *o_dil, *lse_dil, o_gm, o_ssd, gates, w_gla, w_dil, w_gm, w_ssd, w_out,
      jnp.asarray(expand, BF16))


def _mlp_kernel(x_ref, nw_ref, sc_ref, sh_ref, g2_ref, w1_ref, w2_ref, fw_ref, o_ref, *, final_norm):
    x = x_ref[...]
    h = _modulated_norm(x, nw_ref[...], sc_ref[...], sh_ref[...]).astype(BF16)
    acc = jnp.zeros(x.shape, F32)
    for c0 in range(0, MLP_HIDDEN, D_MODEL):
        a = jnp.maximum(jnp.dot(h, w1_ref[:, c0:c0 + D_MODEL], preferred_element_type=F32), 0.0)
        acc = acc + _dot(a * a, w2_ref[c0:c0 + D_MODEL, :])
    y = x + g2_ref[...] * acc
    if final_norm:
        y = y * lax.rsqrt(jnp.mean(y * y, axis=-1, keepdims=True) + EPS) * fw_ref[...]
    o_ref[...] = y


def _mlp(x, nw, mod, w1, w2, fw, layer, final_norm):
    B, S, D = x.shape
    TM = TM_MLP
    tok = pl.BlockSpec((None, TM, D), lambda b, i: (b, i, 0))
    return pl.pallas_call(
        functools.partial(_mlp_kernel, final_norm=final_norm),
        grid=(B, S // TM),
        in_specs=[tok, _layer_spec((1, D), layer), _mod_spec(layer, 4), _mod_spec(layer, 3),
                  _mod_spec(layer, 5), _layer_spec((D, MLP_HIDDEN), layer),
                  _layer_spec((MLP_HIDDEN, D), layer), _const_spec((1, D))],
        out_specs=tok,
        out_shape=jax.ShapeDtypeStruct((B, S, D), F32),
        compiler_params=_cparams(("parallel", "parallel"), 56),
        name="mlp_final" if final_norm else "mlp",
    )(x, nw, mod, mod, mod, w1, w2, fw)


def _pack_w_in(w_in):
    def cols(a, b):
        return w_in[:, :, _SP[a]:_SP[b]]

    def pad(t):
        return jnp.pad(t, ((0, 0), (0, 0), (0, LANE - t.shape[-1])))

    dil = [cols(t, t + 1)[:, :, p * DIL_OUT:(p + 1) * DIL_OUT]
           for p in range(len(DIL_PATTERNS)) for t in (5, 6, 7)]
    parts = [cols(0, 4), pad(cols(4, 5)), *dil, cols(8, 10), cols(10, 12), pad(cols(12, 13)),
             cols(13, 14)]
    return jnp.concatenate(parts, axis=-1).astype(BF16)


def kernel(x, c, w_ada, b_ada, norm1_w, norm2_w, w_in, gla_w_a2, gla_b_a, gla_norm_w, gm_ln_w, gm_ln_b, gm_ws, gm_bs, ssd_conv_w, ssd_conv_b, ssd_dt_bias, ssd_a_log, ssd_d, ssd_norm_w, w_br_gla, w_br_dil, w_br_gm, w_br_ssd, w_out, w_mlp1, w_mlp2, final_norm_w):
    mod = _ada_mod(c, w_ada, b_ada)
    w_cat = _pack_w_in(w_in)
    wa = jnp.pad(gla_w_a2, ((0, 0), (0, LANE - GLA_RANK), (0, 0))).astype(BF16)
    w_gla, w_dil, w_gm, w_ssd, w_o, w1, w2 = (t.astype(BF16) for t in (
        w_br_gla, w_br_dil, w_br_gm, w_br_ssd, w_out, w_mlp1, w_mlp2))
    rows = lambda t: t[:, None, :]
    n1, n2 = rows(norm1_w), rows(norm2_w)
    gla_ba, gla_nw = rows(gla_b_a), rows(jnp.tile(gla_norm_w, (1, GLA_HEADS)))
    gm_lnw, gm_lnb = rows(gm_ln_w), rows(gm_ln_b)
    gm_w, gm_b = _gmlp_params(gm_ws, gm_bs)
    ssd_params = _ssd_params(ssd_conv_b, ssd_dt_bias, ssd_a_log, ssd_d, ssd_norm_w)
    fw = final_norm_w.reshape(1, -1)
    for l in range(DEPTH):
        gla, glr, dil0, dil1, dil2, gm, ssd, dt, gates = _inproj(x, n1, mod, w_cat, l)
        o_gla = _gla(gla, glr, wa, gla_ba, gla_nw, l)
        o_dil, lse_dil = _dilated([dil0, dil1, dil2])
        o_gm = _gmlp(gm, gm_lnw, gm_lnb, gm_w, gm_b, l)
        o_ssd = _ssd(ssd, dt, ssd_conv_w, ssd_params, l)
        x = _merge(x, mod, o_gla, o_dil, lse_dil, o_gm, o_ssd, gates, w_gla, w_dil, w_gm, w_ssd, w_o, l)
        x = _mlp(x, n2, mod, w1, w2, fw, l, final_norm=(l == DEPTH - 1))
    return x
```

```python
import functools

import numpy as np
import jax
import jax.numpy as jnp
from jax import lax
from jax.experimental import pallas as pl
from jax.experimental.pallas import tpu as pltpu

F32 = jnp.float32
BF16 = jnp.bfloat16

D_MODEL = 1024
DEPTH = 4
EPS = 1e-6

GLA_HEADS, GLA_DK, GLA_DV, GLA_RANK, GLA_CHUNK = 4, 32, 64, 16, 64
GLA_GATE_NORM = 16.0
DIL_PATTERNS = ((128, 1), (512, 4), (2048, 16))
DIL_HEADS, DIL_DIM, DIL_STEPS = 4, 64, 128
GM_GROUPS, GM_DIM, GM_CHUNK = 4, 64, 128
SSD_HEADS, SSD_P, SSD_GROUPS, SSD_N, SSD_CONV, SSD_CHUNK = 8, 64, 2, 128, 4, 128
N_BRANCH = 4
MLP_HIDDEN = 4 * D_MODEL

GLA_QK = GLA_HEADS * GLA_DK
GLA_V = GLA_HEADS * GLA_DV
DIL_W = len(DIL_PATTERNS) * DIL_HEADS * DIL_DIM
DIL_OUT = DIL_HEADS * DIL_DIM
GM_W = GM_GROUPS * GM_DIM
SSD_INNER = SSD_HEADS * SSD_P
SSD_BC = SSD_GROUPS * SSD_N
SSD_XBC = SSD_INNER + 2 * SSD_BC
IN_SPLITS = (GLA_QK, GLA_QK, GLA_V, GLA_V, GLA_RANK, DIL_W, DIL_W, DIL_W, GM_W, GM_W,
             SSD_INNER, SSD_XBC, SSD_HEADS, N_BRANCH * D_MODEL)
_SP = tuple(int(v) for v in np.cumsum((0,) + IN_SPLITS))

LANE = 128
NEG = -1e30

PROJ_OUTS = (("gla", 2 * GLA_QK + 2 * GLA_V, BF16, 1), ("glr", LANE, F32, 1),
             ("dil0", 3 * DIL_OUT, BF16, DIL_PATTERNS[0][1]), ("dil1", 3 * DIL_OUT, BF16, DIL_PATTERNS[1][1]),
             ("dil2", 3 * DIL_OUT, BF16, DIL_PATTERNS[2][1]),
             ("gm", 2 * GM_W, BF16, 1), ("ssd", SSD_INNER + SSD_XBC, BF16, 1), ("dt", LANE, F32, 1),
             ("gates", N_BRANCH * D_MODEL, BF16, 1))
PROJ_WIDTH = sum(o[1] for o in PROJ_OUTS)

TM_PROJ = 512
TM_MLP = 512
T_GLA = 256
T_GM = 512
T_SSD = 256
COL_CHUNK = 512
DIL_QBLOCKS = 8


def _cparams(sem, vmem_mib):
    return pltpu.CompilerParams(dimension_semantics=sem, vmem_limit_bytes=vmem_mib * 1024 * 1024)


def _const_spec(shape):
    nd = len(shape)
    return pl.BlockSpec(shape, lambda *_: (0,) * nd, pipeline_mode=pl.Buffered(1))


def _layer_spec(shape, layer):
    nd = len(shape)
    return pl.BlockSpec((None,) + tuple(shape), lambda *_: (layer,) + (0,) * nd,
                        pipeline_mode=pl.Buffered(1))


def _dot(a, b):
    return jnp.dot(a.astype(BF16), b.astype(BF16), preferred_element_type=F32)


def _dot_nt(a, b):
    return lax.dot_general(a.astype(BF16), b.astype(BF16), (((1,), (1,)), ((), ())),
                           preferred_element_type=F32)


def _split(a):
    hi = a.astype(BF16)
    lo = (a - hi.astype(F32)).astype(BF16)
    return hi, lo


def _dot_sel_r(a, sel):
    hi, lo = _split(a)
    return (jnp.dot(hi, sel, preferred_element_type=F32) + jnp.dot(lo, sel, preferred_element_type=F32))


def _dot_sel_l(sel, a):
    hi, lo = _split(a)
    return (jnp.dot(sel, hi, preferred_element_type=F32) + jnp.dot(sel, lo, preferred_element_type=F32))


def _sigmoid(x):
    return 1.0 / (1.0 + jnp.exp(-x))


def _silu(x):
    return x * _sigmoid(x)


def _gelu(x):
    return 0.5 * x * (1.0 + lax.erf(x * (2.0 ** -0.5)))


def _softplus(x):
    return jnp.maximum(x, 0.0) + jnp.log1p(jnp.exp(-jnp.abs(x)))


def _modulated_norm(x, nw, sc, sh):
    y = x * lax.rsqrt(jnp.mean(x * x, axis=-1, keepdims=True) + EPS)
    return (y * nw) * (1.0 + sc) + sh


def _ada_kernel(c_ref, w_ref, b_ref, o_ref):
    o_ref[...] = _dot(_silu(c_ref[...]), w_ref[...]) + b_ref[...]


def _ada_mod(c, w_ada, b_ada):
    B, D = c.shape
    rows = 16
    c_pad = jnp.pad(c, ((0, rows - B), (0, 0)))
    n_col = w_ada.shape[-1] // D
    out = pl.pallas_call(
        _ada_kernel,
        grid=(DEPTH, n_col),
        in_specs=[pl.BlockSpec((rows, D), lambda l, j: (0, 0)),
                  pl.BlockSpec((None, D, D), lambda l, j: (l, 0, j)),
                  pl.BlockSpec((None, 1, D), lambda l, j: (l, 0, j))],
        out_specs=pl.BlockSpec((None, rows, D), lambda l, j: (l, 0, j)),
        out_shape=jax.ShapeDtypeStruct((DEPTH, rows, n_col * D), F32),
        compiler_params=_cparams(("arbitrary", "arbitrary"), 32),
        name="ada_mod",
    )(c_pad, w_ada, b_ada.reshape(DEPTH, 1, -1))
    return out[:, :B].reshape(DEPTH, B, n_col, 1, D)


def _inproj_kernel(x_ref, nw_ref, sc_ref, sh_ref, *refs):
    n_out = len(PROJ_OUTS)
    w_refs, o_refs, stage_ref = refs[:n_out], refs[n_out:2 * n_out], refs[-1]
    TM = x_ref.shape[0]
    h = _modulated_norm(x_ref[...], nw_ref[...], sc_ref[...], sh_ref[...]).astype(BF16)
    for w_ref, o_ref, (_, width, dtype, dil) in zip(w_refs, o_refs, PROJ_OUTS):
        for c0 in range(0, width, COL_CHUNK):
            c1 = min(c0 + COL_CHUNK, width)
            res = jnp.dot(h, w_ref[:, c0:c1], preferred_element_type=F32)
            if dil == 1:
                o_ref[:, c0:c1] = res.astype(dtype)
            else:
                for t in range((c1 - c0) // LANE):
                    stage_ref[c0 // LANE + t] = res[:, t * LANE:(t + 1) * LANE]
        if dil > 1:
            for r in range(dil):
                for t in range(width // LANE):
                    o_ref[r, :, t * LANE:(t + 1) * LANE] = stage_ref[
                        t, pl.ds(r, TM // dil, stride=dil), :].astype(dtype)


def _inproj(x, nw, mod, w_parts, layer):
    B, S, D = x.shape
    TM = TM_PROJ
    out_specs, out_shape = [], []
    for _, w, dt, dil in PROJ_OUTS:
        if dil == 1:
            out_specs.append(pl.BlockSpec((None, TM, w), lambda b, i: (b, i, 0)))
            out_shape.append(jax.ShapeDtypeStruct((B, S, w), dt))
        else:
            out_specs.append(pl.BlockSpec((None, dil, TM // dil, w), lambda b, i: (b, 0, i, 0)))
            out_shape.append(jax.ShapeDtypeStruct((B, dil, S // dil, w), dt))
    return pl.pallas_call(
        _inproj_kernel,
        grid=(B, S // TM),
        in_specs=[pl.BlockSpec((None, TM, D), lambda b, i: (b, i, 0)),
                  _layer_spec((1, D), layer), _mod_spec(layer, 1), _mod_spec(layer, 0)]
                 + [_layer_spec((D, o[1]), layer) for o in PROJ_OUTS],
        out_specs=out_specs,
        out_shape=out_shape,
        scratch_shapes=[pltpu.VMEM((3 * DIL_OUT // LANE, TM, LANE), F32)],
        compiler_params=_cparams(("parallel", "parallel"), 58),
        name="inproj",
    )(x, nw, mod, mod, *w_parts)


def _gla_kernel(gla_ref, glr_ref, wa_ref, ba_ref, nw_ref, tri_ref, ones_ref, o_ref, st_ref):
    C, H, K, V = GLA_CHUNK, GLA_HEADS, GLA_DK, GLA_DV
    T = gla_ref.shape[0]

    @pl.when(pl.program_id(1) == 0)
    def _():
        st_ref[...] = jnp.zeros_like(st_ref)

    blk = gla_ref[...]
    q = blk[:, 0:GLA_QK].astype(F32) * (K ** -0.5)
    k = blk[:, GLA_QK:2 * GLA_QK].astype(F32)
    v = blk[:, 2 * GLA_QK:2 * GLA_QK + GLA_V]
    g = blk[:, 2 * GLA_QK + GLA_V:].astype(F32)
    pre = _dot(glr_ref[...], wa_ref[...]) + ba_ref[...]
    log_a = -_softplus(-pre) * (1.0 / GLA_GATE_NORM)
    b_all = _dot_sel_l(tri_ref[...], log_a)

    k_head = lax.broadcasted_iota(jnp.int32, (C, GLA_QK), 1) // K
    v_head = lax.broadcasted_iota(jnp.int32, (C, GLA_V), 1) // V
    row = lax.broadcasted_iota(jnp.int32, (H * C, C), 0)
    causal = (row % C) >= lax.broadcasted_iota(jnp.int32, (H * C, C), 1)
    st_mask = (lax.broadcasted_iota(jnp.int32, (GLA_V, GLA_QK), 0) // V
               == lax.broadcasted_iota(jnp.int32, (GLA_V, GLA_QK), 1) // K)

    st = st_ref[...]
    outs = []
    for c in range(T // C):
        rs = slice(c * C, (c + 1) * C)
        b = b_all[rs]
        b_last = b[C - 1:C]
        q_t = q[rs] * jnp.exp(b)
        k_t = k[rs] * jnp.exp(-b)
        k_dec = k[rs] * jnp.exp(b_last - b)
        v_c = v[rs]
        q_stack = jnp.concatenate([jnp.where(k_head == h, q_t, 0.0) for h in range(H)], axis=0)
        att = jnp.where(causal, _dot_nt(q_stack, k_t), 0.0)
        o_all = _dot(att, v_c)
        o = _dot_nt(q_t, st)
        for h in range(H):
            o = o + jnp.where(v_head == h, o_all[h * C:(h + 1) * C], 0.0)
        kv_t = _dot(v_c.astype(F32).T, k_dec)
        st = st * jnp.exp(b_last) + jnp.where(st_mask, kv_t, 0.0)
        outs.append(o)
    st_ref[...] = st

    o = jnp.concatenate(outs, axis=0)
    ms = _dot_sel_r(o * o, ones_ref[...]) * (1.0 / V)
    o = o * lax.rsqrt(ms + EPS) * nw_ref[...]
    o_ref[...] = (o * _silu(g)).astype(o_ref.dtype)


def _gla(gla, glr, wa, ba, nw, layer):
    B, S, W = gla.shape
    T, C = T_GLA, GLA_CHUNK
    idx = np.arange(T)
    tri = ((idx[:, None] // C == idx[None, :] // C) & (idx[:, None] >= idx[None, :]))
    hv = np.arange(GLA_V) // GLA_DV
    ones = hv[:, None] == hv[None, :]
    return pl.pallas_call(
        _gla_kernel,
        grid=(B, S // T),
        in_specs=[pl.BlockSpec((None, T, W), lambda b, i: (b, i, 0)),
                  pl.BlockSpec((None, T, LANE), lambda b, i: (b, i, 0)),
                  _layer_spec((LANE, GLA_QK), layer), _layer_spec((1, GLA_QK), layer),
                  _layer_spec((1, GLA_V), layer), _const_spec((T, T)), _const_spec((GLA_V, GLA_V))],
        out_specs=pl.BlockSpec((None, T, GLA_V), lambda b, i: (b, i, 0)),
        out_shape=jax.ShapeDtypeStruct((B, S, GLA_V), BF16),
        scratch_shapes=[pltpu.VMEM((GLA_V, GLA_QK), F32)],
        compiler_params=_cparams(("parallel", "arbitrary"), 32),
        name="gla_mixer",
    )(gla, glr, wa, ba, nw, jnp.asarray(tri, BF16), jnp.asarray(ones, BF16))


def _dil_kernel(q_ref, kp_ref, k_ref, vp_ref, v_ref, bias_ref, o_ref, lse_ref):
    H, E, BLK = DIL_HEADS, DIL_DIM, DIL_STEPS
    n_res, n_blk = q_ref.shape[0], q_ref.shape[1] // BLK
    first = jnp.minimum(pl.program_id(2), 1)
    head = lax.broadcasted_iota(jnp.int32, (BLK, H * E), 1) // E
    lane = lax.broadcasted_iota(jnp.int32, (BLK, LANE), 1)
    for r in range(n_res):
        kcat = jnp.concatenate([kp_ref[r], k_ref[r]], axis=0)
        vcat = jnp.concatenate([vp_ref[r], v_ref[r]], axis=0)
        for i in range(n_blk):
            rows = slice(i * BLK, (i + 1) * BLK)
            q = q_ref[r, rows, :] * jnp.asarray(E ** -0.5, BF16)
            kc = kcat[i * BLK:(i + 2) * BLK]
            vc = vcat[i * BLK:(i + 2) * BLK]
            bias = bias_ref[first] if i == 0 else bias_ref[1]
            q_stack = jnp.concatenate([jnp.where(head == h, q, jnp.zeros_like(q)) for h in range(H)],
                                      axis=0)
            s = _dot_nt(q_stack, kc) + bias
            m = jnp.max(s, axis=-1, keepdims=True)
            p = jnp.exp(s - m)
            l = jnp.sum(p, axis=-1, keepdims=True)
            o_all = _dot(p, vc) / l
            lse = m + jnp.log(l)
            o = o_all[0:BLK]
            lse_out = jnp.zeros((BLK, LANE), F32)
            for h in range(H):
                hs = slice(h * BLK, (h + 1) * BLK)
                if h > 0:
                    o = jnp.where(head == h, o_all[hs], o)
                lse_out = jnp.where(lane == h, lse[hs], lse_out)
            o_ref[r, rows, :] = o.astype(o_ref.dtype)
            lse_ref[r, rows, :] = lse_out


def _dil_bias(window, dil, slopes):
    BLK = DIL_STEPS
    steps = np.arange(BLK)[:, None] + BLK - np.arange(2 * BLK)[None, :]
    ok = (steps >= 0) & (steps <= window // dil)
    bias = -slopes[:, None, None] * (steps * dil)[None].astype(np.float32)
    full = np.where(ok[None], bias, NEG).astype(np.float32)
    first = np.where((np.arange(2 * BLK) >= BLK)[None, None, :], full, NEG)
    return np.stack([first.reshape(-1, 2 * BLK), full.reshape(-1, 2 * BLK)]).astype(np.float32)


def _dil_pattern(qkv, window, dil, slopes):
    B, _, n, _ = qkv.shape
    HE, BLK = DIL_OUT, DIL_STEPS
    n_blk = min(DIL_QBLOCKS, n // BLK)
    n_res = min(dil, DIL_QBLOCKS // n_blk)
    rows = n_blk * BLK
    assert n % rows == 0 and dil % n_res == 0
    cur = lambda t: pl.BlockSpec((None, n_res, rows, HE), lambda b, r, j: (b, r, j, t))
    prev = lambda t: pl.BlockSpec((None, n_res, BLK, HE),
                                  lambda b, r, j: (b, r, jnp.maximum(j * n_blk - 1, 0), t))
    return pl.pallas_call(
        _dil_kernel,
        grid=(B, dil // n_res, n // rows),
        in_specs=[cur(0), prev(1), cur(1), prev(2), cur(2),
                  _const_spec((2, DIL_HEADS * BLK, 2 * BLK))],
        out_specs=[pl.BlockSpec((None, n_res, rows, HE), lambda b, r, j: (b, r, j, 0)),
                   pl.BlockSpec((None, n_res, rows, LANE), lambda b, r, j: (b, r, j, 0))],
        out_shape=[jax.ShapeDtypeStruct((B, dil, n, HE), BF16),
                   jax.ShapeDtypeStruct((B, dil, n, LANE), F32)],
        compiler_params=_cparams(("parallel", "parallel", "arbitrary"), 32),
        name=f"dil_attn_d{dil}",
    )(qkv, qkv, qkv, qkv, qkv, jnp.asarray(_dil_bias(window, dil, slopes)))


def _dilated(qkvs):
    n_pat = len(DIL_PATTERNS)
    n_h = n_pat * DIL_HEADS
    slopes = (2.0 ** (-8.0 * np.arange(1, n_h + 1) / n_h)).astype(np.float32).reshape(n_pat, DIL_HEADS)
    outs = []
    for p, (w, d) in enumerate(DIL_PATTERNS):
        qkv = qkvs[p] if qkvs[p].ndim == 4 else qkvs[p][:, None]
        assert qkv.shape[1] == d
        outs.append(_dil_pattern(qkv, w, d, slopes[p]))
    return [o for o, _ in outs], [l for _, l in outs]


def _gm_kernel(gm_ref, lnw_ref, lnb_ref, w_ref, bs_ref, o_ref):
    G, E, C = GM_GROUPS, GM_DIM, GM_CHUNK
    T = gm_ref.shape[0]
    blk = gm_ref[...].astype(F32)
    u = _gelu(blk[:, :GM_W])
    v = _gelu(blk[:, GM_W:])
    mu = jnp.mean(v, axis=-1, keepdims=True)
    var = jnp.mean(jnp.square(v - mu), axis=-1, keepdims=True)
    v = (v - mu) * lax.rsqrt(var + EPS) * lnw_ref[...] + lnb_ref[...]
    group = lax.broadcasted_iota(jnp.int32, (C, GM_W), 1) // E
    w = w_ref[...]
    bs = bs_ref[...]
    for c in range(T // C):
        rs = slice(c * C, (c + 1) * C)
        r = _dot(w, v[rs])
        sv = bs
        for gi in range(G):
            sv = sv + jnp.where(group == gi, r[gi * C:(gi + 1) * C], 0.0)
        o_ref[rs, :] = (u[rs] * sv).astype(o_ref.dtype)


def _gmlp_params(ws, bs):
    C = GM_CHUNK
    tril = jnp.tril(jnp.ones((C, C), bool))
    w_stack = jnp.where(tril, ws, 0.0).reshape(-1, GM_GROUPS * C, C).astype(BF16)
    bs_exp = jnp.repeat(jnp.swapaxes(bs, 1, 2), GM_DIM, axis=2)
    return w_stack, bs_exp


def _gmlp(gm, lnw, lnb, w_stack, bs_exp, layer):
    B, S, W = gm.shape
    C = GM_CHUNK
    return pl.pallas_call(
        _gm_kernel,
        grid=(B, S // T_GM),
        in_specs=[pl.BlockSpec((None, T_GM, W), lambda b, i: (b, i, 0)),
                  _layer_spec((1, GM_W), layer), _layer_spec((1, GM_W), layer),
                  _layer_spec((GM_GROUPS * C, C), layer), _layer_spec((C, GM_W), layer)],
        out_specs=pl.BlockSpec((None, T_GM, GM_W), lambda b, i: (b, i, 0)),
        out_shape=jax.ShapeDtypeStruct((B, S, GM_W), BF16),
        compiler_params=_cparams(("parallel", "parallel"), 32),
        name="gmlp_mixer",
    )(gm, lnw, lnb, w_stack, bs_exp)


def _ssd_kernel(ssd_ref, dt_ref, cw_ref, cb_ref, dtb_ref, alog_ref, dskip_ref, nw_ref, tri_ref, exp_ref,
                o_ref, st_ref, tail_ref):
    H, P, G, N, L = SSD_HEADS, SSD_P, SSD_GROUPS, SSD_N, SSD_CHUNK
    R = H // G
    GW = R * P
    T = ssd_ref.shape[0]
    TAIL = tail_ref.shape[0]

    @pl.when(pl.program_id(1) == 0)
    def _():
        st_ref[...] = jnp.zeros_like(st_ref)
        tail_ref[...] = jnp.zeros_like(tail_ref)

    blk = ssd_ref[...]
    z = blk[:, :SSD_INNER].astype(F32)
    x_raw = blk[:, SSD_INNER:].astype(F32)
    x_pad = jnp.concatenate([tail_ref[...], x_raw], axis=0)
    tail_ref[...] = x_raw[T - TAIL:]
    conv = cb_ref[...]
    for j in range(SSD_CONV):
        off = TAIL - (SSD_CONV - 1) + j
        conv = conv + cw_ref[j:j + 1, :] * x_pad[off:off + T]
    xbc = _silu(conv)
    xs_all = xbc[:, :SSD_INNER]
    b_all = xbc[:, SSD_INNER:SSD_INNER + SSD_BC]
    c_all = xbc[:, SSD_INNER + SSD_BC:]
    dt_all = _softplus(dt_ref[...] + dtb_ref[...])
    da_all = dt_all * (-jnp.exp(alog_ref[...]))

    tri = tri_ref[...]
    expand = exp_ref[...]
    lower = lax.broadcasted_iota(jnp.int32, (L, L), 0) >= lax.broadcasted_iota(jnp.int32, (L, L), 1)
    head_in_group = lax.broadcasted_iota(jnp.int32, (L, GW), 1) // P

    for c in range(T // L):
        rs = slice(c * L, (c + 1) * L)
        xs, z_c, dt_c = xs_all[rs], z[rs], dt_all[rs]
        cs = _dot_sel_l(tri, da_all[rs])
        cs_t = cs.T
        cs_last = cs[L - 1:L]
        dt_e = _dot_sel_r(dt_c, expand)
        dec_e = _dot_sel_r(jnp.exp(cs_last - cs) * dt_c, expand)
        ecs_e = _dot_sel_r(jnp.exp(cs), expand)
        x_dt = (xs * dt_e).astype(BF16)
        x_dec = (xs * dec_e).astype(BF16)
        ys = []
        for gi in range(G):
            gs = slice(gi * GW, (gi + 1) * GW)
            b_g = b_all[rs, gi * N:(gi + 1) * N]
            c_g = c_all[rs, gi * N:(gi + 1) * N]
            cb = _dot_nt(c_g, b_g)
            y_g = _dot(c_g, st_ref[:, gs]) * ecs_e[:, gs]
            for r in range(R):
                h = gi * R + r
                seg = cs[:, h:h + 1] - cs_t[h:h + 1, :]
                wgt = cb * jnp.exp(jnp.where(lower, seg, NEG))
                y_h = _dot(wgt, x_dt[:, gs])
                y_g = y_g + jnp.where(head_in_group == r, y_h, 0.0)
            st_ref[:, gs] = st_ref[:, gs] * ecs_e[L - 1:L, gs] + _dot(b_g.T, x_dec[:, gs])
            ys.append(y_g)
        y = jnp.concatenate(ys, axis=1) + xs * dskip_ref[...]
        y = y * _silu(z_c)
        normed = []
        for gi in range(G):
            y_g = y[:, gi * GW:(gi + 1) * GW]
            normed.append(y_g * lax.rsqrt(jnp.mean(y_g * y_g, axis=-1, keepdims=True) + EPS))
        o_ref[rs, :] = (jnp.concatenate(normed, axis=1) * nw_ref[...]).astype(o_ref.dtype)


def _ssd_params(conv_b, dt_bias, a_log, d_skip, norm_w):
    pad = lambda t: jnp.pad(t, ((0, 0), (0, LANE - t.shape[1])))[:, None, :]
    return (conv_b[:, None, :], pad(dt_bias), pad(a_log), jnp.repeat(d_skip, SSD_P, axis=1)[:, None, :],
            norm_w[:, None, :])


def _ssd(ssd, dt, conv_w, params, layer):
    B, S, W = ssd.shape
    L = SSD_CHUNK
    tri = np.tril(np.ones((L, L), np.float32))
    expand = np.zeros((LANE, SSD_INNER), np.float32)
    for h in range(SSD_HEADS):
        expand[h, h * SSD_P:(h + 1) * SSD_P] = 1.0
    return pl.pallas_call(
        _ssd_kernel,
        grid=(B, S // T_SSD),
        in_specs=[pl.BlockSpec((None, T_SSD, W), lambda b, i: (b, i, 0)),
                  pl.BlockSpec((None, T_SSD, LANE), lambda b, i: (b, i, 0)),
                  _layer_spec((SSD_CONV, SSD_XBC), layer), _layer_spec((1, SSD_XBC), layer),
                  _layer_spec((1, LANE), layer), _layer_spec((1, LANE), layer),
                  _layer_spec((1, SSD_INNER), layer), _layer_spec((1, SSD_INNER), layer),
                  _const_spec((L, L)), _const_spec((LANE, SSD_INNER))],
        out_specs=pl.BlockSpec((None, T_SSD, SSD_INNER), lambda b, i: (b, i, 0)),
        out_shape=jax.ShapeDtypeStruct((B, S, SSD_INNER), BF16),
        scratch_shapes=[pltpu.VMEM((SSD_N, SSD_INNER), F32), pltpu.VMEM((8, SSD_XBC), F32)],
        compiler_params=_cparams(("parallel", "arbitrary"), 48),
        name="ssd_mixer",
    )(ssd, dt, conv_w, *params, jnp.asarray(tri, BF16), jnp.asarray(expand, BF16))


def _token_order(src_ref, stage_ref):
    dil, per, w = src_ref.shape
    for r in range(dil):
        plane = src_ref[r].astype(F32)
        for t in range(w // LANE):
            stage_ref[t, pl.ds(r, per, stride=dil), :] = plane[:, t * LANE:(t + 1) * LANE]
    return jnp.concatenate([stage_ref[t] for t in range(w // LANE)], axis=1)


def _tail_kernel(x_ref, g1_ref, gla_ref, d0_ref, d1_ref, d2_ref, l0_ref, l1_ref, l2_ref, gm_ref, ssd_ref,
                 gates_ref, wg_ref, wd_ref, wm_ref, ws_ref, wo_ref, exp_ref,
                 nw_ref, sc_ref, sh_ref, g2_ref, w1_ref, w2_ref, fw_ref, o_ref,
                 od1_ref, od2_ref, ol1_ref, ol2_ref, *, final_norm):
    D = D_MODEL
    d0, l0 = d0_ref[...].astype(F32), l0_ref[...]
    d1, l1 = _token_order(d1_ref, od1_ref), _token_order(l1_ref, ol1_ref)
    d2, l2 = _token_order(d2_ref, od2_ref), _token_order(l2_ref, ol2_ref)
    m = jnp.maximum(jnp.maximum(l0, l1), l2)
    e0, e1, e2 = jnp.exp(l0 - m), jnp.exp(l1 - m), jnp.exp(l2 - m)
    den = e0 + e1 + e2
    expand = exp_ref[...]
    o_dil = (_dot_sel_r(e0 / den, expand) * d0 + _dot_sel_r(e1 / den, expand) * d1
             + _dot_sel_r(e2 / den, expand) * d2)
    branches = ((gla_ref[...], wg_ref), (o_dil, wd_ref), (gm_ref[...], wm_ref), (ssd_ref[...], ws_ref))
    merged = None
    for i, (o_b, w_ref) in enumerate(branches):
        g = gates_ref[:, i * D:(i + 1) * D]
        gate = (jnp.tanh(g * jnp.asarray(0.5, g.dtype)) * jnp.asarray(0.5, g.dtype)
                + jnp.asarray(0.5, g.dtype)).astype(F32)
        term = gate * _dot(o_b, w_ref[...])
        merged = term if merged is None else merged + term
    x = x_ref[...] + g1_ref[...] * _dot(merged, wo_ref[...])

    h = _modulated_norm(x, nw_ref[...], sc_ref[...], sh_ref[...]).astype(BF16)
    acc = jnp.zeros(x.shape, F32)
    for c0 in range(0, MLP_HIDDEN, D_MODEL):
        a = jnp.maximum(jnp.dot(h, w1_ref[:, c0:c0 + D_MODEL], preferred_element_type=F32), 0.0)
        acc = acc + _dot(a * a, w2_ref[c0:c0 + D_MODEL, :])
    y = x + g2_ref[...] * acc
    if final_norm:
        y = y * lax.rsqrt(jnp.mean(y * y, axis=-1, keepdims=True) + EPS) * fw_ref[...]
    o_ref[...] = y


def _mod_spec(layer, which):
    return pl.BlockSpec((None, None, None, 1, D_MODEL), lambda b, i: (layer, b, which, 0, 0))


def _tail(x, mod, o_gla, o_dil, lse_dil, o_gm, o_ssd, gates, w_gla, w_dil, w_gm, w_ssd, w_out,
          nw2, w1, w2, fw, layer, final_norm):
    B, S, D = x.shape
    TM = TM_MLP
    tok = lambda w: pl.BlockSpec((None, TM, w), lambda b, i: (b, i, 0))

    def plane(w, dil):
        if dil == 1:
            return pl.BlockSpec((None, None, TM, w), lambda b, i: (b, 0, i, 0))
        return pl.BlockSpec((None, dil, TM // dil, w), lambda b, i: (b, 0, i, 0))

    dils = [d for _, d in DIL_PATTERNS]
    assert dils[0] == 1 and len(dils) == 3
    expand = np.zeros((LANE, DIL_OUT), np.float32)
    for h in range(DIL_HEADS):
        expand[h, h * DIL_DIM:(h + 1) * DIL_DIM] = 1.0
    return pl.pallas_call(
        functools.partial(_tail_kernel, final_norm=final_norm),
        grid=(B, S // TM),
        in_specs=[tok(D), _mod_spec(layer, 2), tok(GLA_V)]
                 + [plane(DIL_OUT, d) for d in dils] + [plane(LANE, d) for d in dils]
                 + [tok(GM_W), tok(SSD_INNER), tok(N_BRANCH * D),
                    _layer_spec((GLA_V, D), layer), _layer_spec((DIL_OUT, D), layer),
                    _layer_spec((GM_W, D), layer), _layer_spec((SSD_INNER, D), layer),
                    _layer_spec((D, D), layer), _const_spec((LANE, DIL_OUT)),
                    _layer_spec((1, D), layer), _mod_spec(layer, 4), _mod_spec(layer, 3),
                    _mod_spec(layer, 5), _layer_spec((D, MLP_HIDDEN), layer),
                    _layer_spec((MLP_HIDDEN, D), layer), _const_spec((1, D))],
        out_specs=tok(D),
        out_shape=jax.ShapeDtypeStruct((B, S, D), F32),
        scratch_shapes=[pltpu.VMEM((DIL_OUT // LANE, TM, LANE), F32), pltpu.VMEM((DIL_OUT // LANE, TM, LANE), F32),
                        pltpu.VMEM((1, TM, LANE), F32), pltpu.VMEM((1, TM, LANE), F32)],
        compiler_params=_cparams(("parallel", "parallel"), 58),
        name="tail_final" if final_norm else "tail",
    )(x, mod, o_gla, *o_dil, *lse_dil, o_gm, o_ssd, gates, w_gla, w_dil, w_gm, w_ssd, w_out,
      jnp.asarray(expand, BF16), nw2, mod, mod, mod, w1, w2, fw)


def _pack_w_in(w_in):
    def cols(a, b):
        return w_in[:, :, _SP[a]:_SP[b]]

    def pad(t):
        return jnp.pad(t, ((0, 0), (0, 0), (0, LANE - t.shape[-1])))

    dil = [cols(t, t + 1)[:, :, p * DIL_OUT:(p + 1) * DIL_OUT]
           for p in range(len(DIL_PATTERNS)) for t in (5, 6, 7)]
    dil = [jnp.concatenate(dil[3 * p:3 * p + 3], axis=-1) for p in range(len(DIL_PATTERNS))]
    parts = [cols(0, 4), pad(cols(4, 5)), *dil, cols(8, 10), cols(10, 12), pad(cols(12, 13)),
             cols(13, 14)]
    assert [t.shape[-1] for t in parts] == [o[1] for o in PROJ_OUTS]
    return [t.astype(BF16) for t in parts]


def kernel(x, c, w_ada, b_ada, norm1_w, norm2_w, w_in, gla_w_a2, gla_b_a, gla_norm_w, gm_ln_w, gm_ln_b, gm_ws, gm_bs, ssd_conv_w, ssd_conv_b, ssd_dt_bias, ssd_a_log, ssd_d, ssd_norm_w, w_br_gla, w_br_dil, w_br_gm, w_br_ssd, w_out, w_mlp1, w_mlp2, final_norm_w):
    mod = _ada_mod(c, w_ada, b_ada)
    w_cat = _pack_w_in(w_in)
    wa = jnp.pad(gla_w_a2, ((0, 0), (0, LANE - GLA_RANK), (0, 0))).astype(BF16)
    w_gla, w_dil, w_gm, w_ssd, w_o, w1, w2 = (t.astype(BF16) for t in (
        w_br_gla, w_br_dil, w_br_gm, w_br_ssd, w_out, w_mlp1, w_mlp2))
    rows = lambda t: t[:, None, :]
    n1, n2 = rows(norm1_w), rows(norm2_w)
    gla_ba, gla_nw = rows(gla_b_a), rows(jnp.tile(gla_norm_w, (1, GLA_HEADS)))
    gm_lnw, gm_lnb = rows(gm_ln_w), rows(gm_ln_b)
    gm_w, gm_b = _gmlp_params(gm_ws, gm_bs)
    ssd_params = _ssd_params(ssd_conv_b, ssd_dt_bias, ssd_a_log, ssd_d, ssd_norm_w)
    fw = final_norm_w.reshape(1, -1)
    for l in range(DEPTH):
        gla, glr, dil0, dil1, dil2, gm, ssd, dt, gates = _inproj(x, n1, mod, w_cat, l)
        o_gla = _gla(gla, glr, wa, gla_ba, gla_nw, l)
        o_dil, lse_dil = _dilated([dil0, dil1, dil2])
        o_gm = _gmlp(gm, gm_lnw, gm_lnb, gm_w, gm_b, l)
        o_ssd = _ssd(ssd, dt, ssd_conv_w, ssd_params, l)
        x = _tail(x, mod, o_gla, o_dil, lse_dil, o_gm, o_ssd, gates, w_gla, w_dil, w_gm, w_ssd, w_o,
                  n2, w1, w2, fw, l, final_norm=(l == DEPTH - 1))
    return x
```

```python
import functools

import numpy as np
import jax
import jax.numpy as jnp
from jax import lax
from jax.experimental import pallas as pl
from jax.experimental.pallas import tpu as pltpu

F32 = jnp.float32
BF16 = jnp.bfloat16

D_MODEL = 1024
DEPTH = 4
EPS = 1e-6

GLA_HEADS, GLA_DK, GLA_DV, GLA_RANK, GLA_CHUNK = 4, 32, 64, 16, 64
GLA_GATE_NORM = 16.0
DIL_PATTERNS = ((128, 1), (512, 4), (2048, 16))
DIL_HEADS, DIL_DIM, DIL_STEPS = 4, 64, 128
GM_GROUPS, GM_DIM, GM_CHUNK = 4, 64, 128
SSD_HEADS, SSD_P, SSD_GROUPS, SSD_N, SSD_CONV, SSD_CHUNK = 8, 64, 2, 128, 4, 128
N_BRANCH = 4
MLP_HIDDEN = 4 * D_MODEL

GLA_QK = GLA_HEADS * GLA_DK
GLA_V = GLA_HEADS * GLA_DV
DIL_W = len(DIL_PATTERNS) * DIL_HEADS * DIL_DIM
DIL_OUT = DIL_HEADS * DIL_DIM
GM_W = GM_GROUPS * GM_DIM
SSD_INNER = SSD_HEADS * SSD_P
SSD_BC = SSD_GROUPS * SSD_N
SSD_XBC = SSD_INNER + 2 * SSD_BC
IN_SPLITS = (GLA_QK, GLA_QK, GLA_V, GLA_V, GLA_RANK, DIL_W, DIL_W, DIL_W, GM_W, GM_W,
             SSD_INNER, SSD_XBC, SSD_HEADS, N_BRANCH * D_MODEL)
_SP = tuple(int(v) for v in np.cumsum((0,) + IN_SPLITS))

LANE = 128
NEG = -1e30

PROJ_OUTS = (("gla", 2 * GLA_QK + 2 * GLA_V, BF16, 1), ("glr", LANE, F32, 1),
             ("dil0", 3 * DIL_OUT, BF16, DIL_PATTERNS[0][1]), ("dil1", 3 * DIL_OUT, BF16, DIL_PATTERNS[1][1]),
             ("dil2", 3 * DIL_OUT, BF16, DIL_PATTERNS[2][1]),
             ("gm", 2 * GM_W, BF16, 1), ("ssd_z", SSD_INNER, BF16, 1), ("ssd_xbc", SSD_XBC, BF16, 1),
             ("dt", LANE, F32, 1),
             ("gates", N_BRANCH * D_MODEL, BF16, 1))
PROJ_WIDTH = sum(o[1] for o in PROJ_OUTS)

TM_PROJ = 512
TM_MLP = 512
T_GLA = 512
GLA_CUMSUM_ROWS = 256
T_GM = 1024
T_SSD = 512
COL_CHUNK = 512
DIL_QBLOCKS = 8


def _cparams(sem, vmem_mib):
    return pltpu.CompilerParams(dimension_semantics=sem, vmem_limit_bytes=vmem_mib * 1024 * 1024)


def _const_spec(shape):
    nd = len(shape)
    return pl.BlockSpec(shape, lambda *_: (0,) * nd, pipeline_mode=pl.Buffered(1))


def _layer_spec(shape, layer):
    nd = len(shape)
    return pl.BlockSpec((None,) + tuple(shape), lambda *_: (layer,) + (0,) * nd,
                        pipeline_mode=pl.Buffered(1))


def _dot(a, b):
    return jnp.dot(a.astype(BF16), b.astype(BF16), preferred_element_type=F32)


def _dot_nt(a, b):
    return lax.dot_general(a.astype(BF16), b.astype(BF16), (((1,), (1,)), ((), ())),
                           preferred_element_type=F32)


def _split(a):
    hi = a.astype(BF16)
    lo = (a - hi.astype(F32)).astype(BF16)
    return hi, lo


def _dot_sel_r(a, sel):
    hi, lo = _split(a)
    return (jnp.dot(hi, sel, preferred_element_type=F32) + jnp.dot(lo, sel, preferred_element_type=F32))


def _dot_sel_l(sel, a):
    hi, lo = _split(a)
    return (jnp.dot(sel, hi, preferred_element_type=F32) + jnp.dot(sel, lo, preferred_element_type=F32))


def _sigmoid(x):
    return 1.0 / (1.0 + jnp.exp(-x))


def _silu(x):
    return x * _sigmoid(x)


def _gelu(x):
    return 0.5 * x * (1.0 + lax.erf(x * (2.0 ** -0.5)))


def _softplus(x):
    return jnp.maximum(x, 0.0) + jnp.log1p(jnp.exp(-jnp.abs(x)))


def _modulated_norm(x, nw, sc, sh):
    y = x * lax.rsqrt(jnp.mean(x * x, axis=-1, keepdims=True) + EPS)
    return (y * nw) * (1.0 + sc) + sh


def _ada_kernel(c_ref, w_ref, b_ref, o_ref):
    o_ref[...] = _dot(_silu(c_ref[...]), w_ref[...]) + b_ref[...]


def _ada_mod(c, w_ada, b_ada):
    B, D = c.shape
    rows = 16
    c_pad = jnp.pad(c, ((0, rows - B), (0, 0)))
    n_col = w_ada.shape[-1] // D
    out = pl.pallas_call(
        _ada_kernel,
        grid=(DEPTH, n_col),
        in_specs=[pl.BlockSpec((rows, D), lambda l, j: (0, 0)),
                  pl.BlockSpec((None, D, D), lambda l, j: (l, 0, j)),
                  pl.BlockSpec((None, 1, D), lambda l, j: (l, 0, j))],
        out_specs=pl.BlockSpec((None, rows, D), lambda l, j: (l, 0, j)),
        out_shape=jax.ShapeDtypeStruct((DEPTH, rows, n_col * D), F32),
        compiler_params=_cparams(("arbitrary", "arbitrary"), 32),
        name="ada_mod",
    )(c_pad, w_ada, b_ada.reshape(DEPTH, 1, -1))
    return out[:, :B].reshape(DEPTH, B, n_col, 1, D)


def _inproj_kernel(x_ref, nw_ref, sc_ref, sh_ref, cw_ref, cb_ref, *refs):
    n_out = len(PROJ_OUTS)
    w_refs, o_refs, (stage_ref, carry_ref) = refs[:n_out], refs[n_out:2 * n_out], refs[2 * n_out:]
    TM = x_ref.shape[0]
    KEEP = carry_ref.shape[0]

    @pl.when(pl.program_id(1) == 0)
    def _():
        carry_ref[...] = jnp.zeros_like(carry_ref)

    h = _modulated_norm(x_ref[...], nw_ref[...], sc_ref[...], sh_ref[...]).astype(BF16)
    for w_ref, o_ref, (name, width, dtype, dil) in zip(w_refs, o_refs, PROJ_OUTS):
        for c0 in range(0, width, COL_CHUNK):
            c1 = min(c0 + COL_CHUNK, width)
            res = jnp.dot(h, w_ref[:, c0:c1], preferred_element_type=F32)
            if name == "ssd_xbc":
                padded = jnp.concatenate([carry_ref[:, c0:c1], res], axis=0)
                carry_ref[:, c0:c1] = res[TM - KEEP:]
                conv = cb_ref[:, c0:c1]
                for j in range(SSD_CONV):
                    off = KEEP - (SSD_CONV - 1) + j
                    conv = conv + cw_ref[j:j + 1, c0:c1] * padded[off:off + TM]
                o_ref[:, c0:c1] = _silu(conv).astype(dtype)
            elif dil == 1:
                o_ref[:, c0:c1] = res.astype(dtype)
            else:
                for t in range((c1 - c0) // LANE):
                    stage_ref[c0 // LANE + t] = res[:, t * LANE:(t + 1) * LANE]
        if dil > 1:
            for r in range(dil):
                for t in range(width // LANE):
                    o_ref[r, :, t * LANE:(t + 1) * LANE] = stage_ref[
                        t, pl.ds(r, TM // dil, stride=dil), :].astype(dtype)


def _inproj(x, nw, mod, conv_w, conv_b, w_parts, layer):
    B, S, D = x.shape
    TM = TM_PROJ
    out_specs, out_shape = [], []
    for _, w, dt, dil in PROJ_OUTS:
        if dil == 1:
            out_specs.append(pl.BlockSpec((None, TM, w), lambda b, i: (b, i, 0)))
            out_shape.append(jax.ShapeDtypeStruct((B, S, w), dt))
        else:
            out_specs.append(pl.BlockSpec((None, dil, TM // dil, w), lambda b, i: (b, 0, i, 0)))
            out_shape.append(jax.ShapeDtypeStruct((B, dil, S // dil, w), dt))
    return pl.pallas_call(
        _inproj_kernel,
        grid=(B, S // TM),
        in_specs=[pl.BlockSpec((None, TM, D), lambda b, i: (b, i, 0)),
                  _layer_spec((1, D), layer), _mod_spec(layer, 1), _mod_spec(layer, 0),
                  _layer_spec((SSD_CONV, SSD_XBC), layer), _layer_spec((1, SSD_XBC), layer)]
                 + [_layer_spec((D, o[1]), layer) for o in PROJ_OUTS],
        out_specs=out_specs,
        out_shape=out_shape,
        scratch_shapes=[pltpu.VMEM((3 * DIL_OUT // LANE, TM, LANE), F32),
                        pltpu.VMEM((8, SSD_XBC), F32)],
        compiler_params=_cparams(("parallel", "arbitrary"), 58),
        name="inproj",
    )(x, nw, mod, mod, conv_w, conv_b, *w_parts)


def _gla_kernel(gla_ref, glr_ref, wa_ref, ba_ref, nw_ref, tri_ref, ones_ref, o_ref, st_ref):
    C, H, K, V = GLA_CHUNK, GLA_HEADS, GLA_DK, GLA_DV
    T = gla_ref.shape[0]

    @pl.when(pl.program_id(1) == 0)
    def _():
        st_ref[...] = jnp.zeros_like(st_ref)

    blk = gla_ref[...]
    q = blk[:, 0:GLA_QK].astype(F32) * (K ** -0.5)
    k = blk[:, GLA_QK:2 * GLA_QK].astype(F32)
    v = blk[:, 2 * GLA_QK:2 * GLA_QK + GLA_V]
    g = blk[:, 2 * GLA_QK + GLA_V:].astype(F32)
    pre = _dot(glr_ref[...], wa_ref[...]) + ba_ref[...]
    log_a = -_softplus(-pre) * (1.0 / GLA_GATE_NORM)
    tri = tri_ref[...]
    TB = tri.shape[0]
    b_all = jnp.concatenate([_dot_sel_l(tri, log_a[i * TB:(i + 1) * TB]) for i in range(T // TB)],
                            axis=0)

    k_head = lax.broadcasted_iota(jnp.int32, (C, GLA_QK), 1) // K
    v_head = lax.broadcasted_iota(jnp.int32, (C, GLA_V), 1) // V
    row = lax.broadcasted_iota(jnp.int32, (H * C, C), 0)
    causal = (row % C) >= lax.broadcasted_iota(jnp.int32, (H * C, C), 1)
    st_mask = (lax.broadcasted_iota(jnp.int32, (GLA_V, GLA_QK), 0) // V
               == lax.broadcasted_iota(jnp.int32, (GLA_V, GLA_QK), 1) // K)

    st = st_ref[...]
    outs = []
    for c in range(T // C):
        rs = slice(c * C, (c + 1) * C)
        b = b_all[rs]
        b_last = b[C - 1:C]
        q_t = q[rs] * jnp.exp(b)
        k_t = k[rs] * jnp.exp(-b)
        k_dec = k[rs] * jnp.exp(b_last - b)
        v_c = v[rs]
        q_stack = jnp.concatenate([jnp.where(k_head == h, q_t, 0.0) for h in range(H)], axis=0)
        att = jnp.where(causal, _dot_nt(q_stack, k_t), 0.0)
        o_all = _dot(att, v_c)
        o = _dot_nt(q_t, st)
        for h in range(H):
            o = o + jnp.where(v_head == h, o_all[h * C:(h + 1) * C], 0.0)
        kv_t = _dot(v_c.astype(F32).T, k_dec)
        st = st * jnp.exp(b_last) + jnp.where(st_mask, kv_t, 0.0)
        outs.append(o)
    st_ref[...] = st

    o = jnp.concatenate(outs, axis=0)
    ms = _dot_sel_r(o * o, ones_ref[...]) * (1.0 / V)
    o = o * lax.rsqrt(ms + EPS) * nw_ref[...]
    o_ref[...] = (o * _silu(g)).astype(o_ref.dtype)


def _gla(gla, glr, wa, ba, nw, layer):
    B, S, W = gla.shape
    T, C, TB = T_GLA, GLA_CHUNK, GLA_CUMSUM_ROWS
    idx = np.arange(TB)
    tri = ((idx[:, None] // C == idx[None, :] // C) & (idx[:, None] >= idx[None, :]))
    hv = np.arange(GLA_V) // GLA_DV
    ones = hv[:, None] == hv[None, :]
    return pl.pallas_call(
        _gla_kernel,
        grid=(B, S // T),
        in_specs=[pl.BlockSpec((None, T, W), lambda b, i: (b, i, 0)),
                  pl.BlockSpec((None, T, LANE), lambda b, i: (b, i, 0)),
                  _layer_spec((LANE, GLA_QK), layer), _layer_spec((1, GLA_QK), layer),
                  _layer_spec((1, GLA_V), layer), _const_spec((TB, TB)), _const_spec((GLA_V, GLA_V))],
        out_specs=pl.BlockSpec((None, T, GLA_V), lambda b, i: (b, i, 0)),
        out_shape=jax.ShapeDtypeStruct((B, S, GLA_V), BF16),
        scratch_shapes=[pltpu.VMEM((GLA_V, GLA_QK), F32)],
        compiler_params=_cparams(("parallel", "arbitrary"), 32),
        name="gla_mixer",
    )(gla, glr, wa, ba, nw, jnp.asarray(tri, BF16), jnp.asarray(ones, BF16))


def _dil_kernel(q_ref, kp_ref, k_ref, vp_ref, v_ref, bias_ref, o_ref, lse_ref):
    H, E, BLK = DIL_HEADS, DIL_DIM, DIL_STEPS
    n_res, n_blk = q_ref.shape[0], q_ref.shape[1] // BLK
    first = jnp.minimum(pl.program_id(2), 1)
    head = lax.broadcasted_iota(jnp.int32, (BLK, H * E), 1) // E
    lane = lax.broadcasted_iota(jnp.int32, (BLK, LANE), 1)
    for r in range(n_res):
        kcat = jnp.concatenate([kp_ref[r], k_ref[r]], axis=0)
        vcat = jnp.concatenate([vp_ref[r], v_ref[r]], axis=0)
        for i in range(n_blk):
            rows = slice(i * BLK, (i + 1) * BLK)
            q = q_ref[r, rows, :] * jnp.asarray(E ** -0.5, BF16)
            kc = kcat[i * BLK:(i + 2) * BLK]
            vc = vcat[i * BLK:(i + 2) * BLK]
            bias = bias_ref[first] if i == 0 else bias_ref[1]
            q_stack = jnp.concatenate([jnp.where(head == h, q, jnp.zeros_like(q)) for h in range(H)],
                                      axis=0)
            s = _dot_nt(q_stack, kc) + bias
            m = jnp.max(s, axis=-1, keepdims=True)
            p = jnp.exp(s - m)
            l = jnp.sum(p, axis=-1, keepdims=True)
            o_all = _dot(p, vc) / l
            lse = m + jnp.log(l)
            o = o_all[0:BLK]
            lse_out = jnp.zeros((BLK, LANE), F32)
            for h in range(H):
                hs = slice(h * BLK, (h + 1) * BLK)
                if h > 0:
                    o = jnp.where(head == h, o_all[hs], o)
                lse_out = jnp.where(lane == h, lse[hs], lse_out)
            o_ref[r, rows, :] = o.astype(o_ref.dtype)
            lse_ref[r, rows, :] = lse_out


def _dil_bias(window, dil, slopes):
    BLK = DIL_STEPS
    steps = np.arange(BLK)[:, None] + BLK - np.arange(2 * BLK)[None, :]
    ok = (steps >= 0) & (steps <= window // dil)
    bias = -slopes[:, None, None] * (steps * dil)[None].astype(np.float32)
    full = np.where(ok[None], bias, NEG).astype(np.float32)
    first = np.where((np.arange(2 * BLK) >= BLK)[None, None, :], full, NEG)
    return np.stack([first.reshape(-1, 2 * BLK), full.reshape(-1, 2 * BLK)]).astype(np.float32)


def _dil_pattern(qkv, window, dil, slopes):
    B, _, n, _ = qkv.shape
    HE, BLK = DIL_OUT, DIL_STEPS
    n_blk = min(DIL_QBLOCKS, n // BLK)
    n_res = min(dil, DIL_QBLOCKS // n_blk)
    rows = n_blk * BLK
    assert n % rows == 0 and dil % n_res == 0
    cur = lambda t: pl.BlockSpec((None, n_res, rows, HE), lambda b, r, j: (b, r, j, t))
    prev = lambda t: pl.BlockSpec((None, n_res, BLK, HE),
                                  lambda b, r, j: (b, r, jnp.maximum(j * n_blk - 1, 0), t))
    return pl.pallas_call(
        _dil_kernel,
        grid=(B, dil // n_res, n // rows),
        in_specs=[cur(0), prev(1), cur(1), prev(2), cur(2),
                  _const_spec((2, DIL_HEADS * BLK, 2 * BLK))],
        out_specs=[pl.BlockSpec((None, n_res, rows, HE), lambda b, r, j: (b, r, j, 0)),
                   pl.BlockSpec((None, n_res, rows, LANE), lambda b, r, j: (b, r, j, 0))],
        out_shape=[jax.ShapeDtypeStruct((B, dil, n, HE), BF16),
                   jax.ShapeDtypeStruct((B, dil, n, LANE), F32)],
        compiler_params=_cparams(("parallel", "parallel", "arbitrary"), 32),
        name=f"dil_attn_d{dil}",
    )(qkv, qkv, qkv, qkv, qkv, jnp.asarray(_dil_bias(window, dil, slopes)))


def _dilated(qkvs):
    n_pat = len(DIL_PATTERNS)
    n_h = n_pat * DIL_HEADS
    slopes = (2.0 ** (-8.0 * np.arange(1, n_h + 1) / n_h)).astype(np.float32).reshape(n_pat, DIL_HEADS)
    outs = []
    for p, (w, d) in enumerate(DIL_PATTERNS):
        qkv = qkvs[p] if qkvs[p].ndim == 4 else qkvs[p][:, None]
        assert qkv.shape[1] == d
        outs.append(_dil_pattern(qkv, w, d, slopes[p]))
    return [o for o, _ in outs], [l for _, l in outs]


def _gm_kernel(gm_ref, lnw_ref, lnb_ref, w_ref, bs_ref, o_ref):
    G, E, C = GM_GROUPS, GM_DIM, GM_CHUNK
    T = gm_ref.shape[0]
    blk = gm_ref[...].astype(F32)
    u = _gelu(blk[:, :GM_W])
    v = _gelu(blk[:, GM_W:])
    mu = jnp.mean(v, axis=-1, keepdims=True)
    var = jnp.mean(jnp.square(v - mu), axis=-1, keepdims=True)
    v = (v - mu) * lax.rsqrt(var + EPS) * lnw_ref[...] + lnb_ref[...]
    group = lax.broadcasted_iota(jnp.int32, (C, GM_W), 1) // E
    w = w_ref[...]
    bs = bs_ref[...]
    for c in range(T // C):
        rs = slice(c * C, (c + 1) * C)
        r = _dot(w, v[rs])
        sv = bs
        for gi in range(G):
            sv = sv + jnp.where(group == gi, r[gi * C:(gi + 1) * C], 0.0)
        o_ref[rs, :] = (u[rs] * sv).astype(o_ref.dtype)


def _gmlp_params(ws, bs):
    C = GM_CHUNK
    tril = jnp.tril(jnp.ones((C, C), bool))
    w_stack = jnp.where(tril, ws, 0.0).reshape(-1, GM_GROUPS * C, C).astype(BF16)
    bs_exp = jnp.repeat(jnp.swapaxes(bs, 1, 2), GM_DIM, axis=2)
    return w_stack, bs_exp


def _gmlp(gm, lnw, lnb, w_stack, bs_exp, layer):
    B, S, W = gm.shape
    C = GM_CHUNK
    return pl.pallas_call(
        _gm_kernel,
        grid=(B, S // T_GM),
        in_specs=[pl.BlockSpec((None, T_GM, W), lambda b, i: (b, i, 0)),
                  _layer_spec((1, GM_W), layer), _layer_spec((1, GM_W), layer),
                  _layer_spec((GM_GROUPS * C, C), layer), _layer_spec((C, GM_W), layer)],
        out_specs=pl.BlockSpec((None, T_GM, GM_W), lambda b, i: (b, i, 0)),
        out_shape=jax.ShapeDtypeStruct((B, S, GM_W), BF16),
        compiler_params=_cparams(("parallel", "parallel"), 32),
        name="gmlp_mixer",
    )(gm, lnw, lnb, w_stack, bs_exp)


def _ssd_kernel(z_ref, xbc_ref, dt_ref, dtb_ref, alog_ref, dskip_ref, nw_ref, tri_ref, exp_ref,
                o_ref, st_ref):
    H, P, G, N, L = SSD_HEADS, SSD_P, SSD_GROUPS, SSD_N, SSD_CHUNK
    R = H // G
    GW = R * P
    T = z_ref.shape[0]

    @pl.when(pl.program_id(1) == 0)
    def _():
        st_ref[...] = jnp.zeros_like(st_ref)

    dt_all = _softplus(dt_ref[...] + dtb_ref[...])
    da_all = dt_all * (-jnp.exp(alog_ref[...]))
    tri = tri_ref[...]
    expand = exp_ref[...]
    lower = lax.broadcasted_iota(jnp.int32, (L, L), 0) >= lax.broadcasted_iota(jnp.int32, (L, L), 1)
    head_in_group = lax.broadcasted_iota(jnp.int32, (L, GW), 1) // P

    for c in range(T // L):
        rs = slice(c * L, (c + 1) * L)
        xs = xbc_ref[rs, :SSD_INNER].astype(F32)
        dt_c = dt_all[rs]
        cs = _dot_sel_l(tri, da_all[rs])
        cs_t = cs.T
        cs_last = cs[L - 1:L]
        dt_e = _dot(dt_c, expand)
        dec_e = _dot(jnp.exp(cs_last - cs) * dt_c, expand)
        ecs_e = _dot_sel_r(jnp.exp(cs), expand)
        x_dt = (xs * dt_e).astype(BF16)
        x_dec = (xs * dec_e).astype(BF16)
        ys = []
        for gi in range(G):
            gs = slice(gi * GW, (gi + 1) * GW)
            b_g = xbc_ref[rs, SSD_INNER + gi * N:SSD_INNER + (gi + 1) * N]
            c_g = xbc_ref[rs, SSD_INNER + SSD_BC + gi * N:SSD_INNER + SSD_BC + (gi + 1) * N]
            cb = _dot_nt(c_g, b_g)
            y_g = _dot(c_g, st_ref[:, gs]) * ecs_e[:, gs]
            y_intra = None
            for r in range(R):
                h = gi * R + r
                seg = cs[:, h:h + 1] - cs_t[h:h + 1, :]
                wgt = cb * jnp.exp(jnp.where(lower, seg, NEG))
                y_h = _dot(wgt, x_dt[:, gs])
                y_intra = y_h if r == 0 else jnp.where(head_in_group == r, y_h, y_intra)
            st_ref[:, gs] = st_ref[:, gs] * ecs_e[L - 1:L, gs] + _dot(b_g.astype(F32).T, x_dec[:, gs])
            ys.append(y_g + y_intra)
        y = jnp.concatenate(ys, axis=1) + xs * dskip_ref[...]
        y = y * _silu(z_ref[rs, :].astype(F32))
        normed = []
        for gi in range(G):
            y_g = y[:, gi * GW:(gi + 1) * GW]
            normed.append(y_g * lax.rsqrt(jnp.mean(y_g * y_g, axis=-1, keepdims=True) + EPS))
        o_ref[rs, :] = (jnp.concatenate(normed, axis=1) * nw_ref[...]).astype(o_ref.dtype)


def _ssd_params(dt_bias, a_log, d_skip, norm_w):
    pad = lambda t: jnp.pad(t, ((0, 0), (0, LANE - t.shape[1])))[:, None, :]
    return (pad(dt_bias), pad(a_log), jnp.repeat(d_skip, SSD_P, axis=1)[:, None, :], norm_w[:, None, :])


def _ssd(z, xbc, dt, params, layer):
    B, S, _ = z.shape
    L = SSD_CHUNK
    tri = np.tril(np.ones((L, L), np.float32))
    expand = np.zeros((LANE, SSD_INNER), np.float32)
    for h in range(SSD_HEADS):
        expand[h, h * SSD_P:(h + 1) * SSD_P] = 1.0
    return pl.pallas_call(
        _ssd_kernel,
        grid=(B, S // T_SSD),
        in_specs=[pl.BlockSpec((None, T_SSD, SSD_INNER), lambda b, i: (b, i, 0)),
                  pl.BlockSpec((None, T_SSD, SSD_XBC), lambda b, i: (b, i, 0)),
                  pl.BlockSpec((None, T_SSD, LANE), lambda b, i: (b, i, 0)),
                  _layer_spec((1, LANE), layer), _layer_spec((1, LANE), layer),
                  _layer_spec((1, SSD_INNER), layer), _layer_spec((1, SSD_INNER), layer),
                  _const_spec((L, L)), _const_spec((LANE, SSD_INNER))],
        out_specs=pl.BlockSpec((None, T_SSD, SSD_INNER), lambda b, i: (b, i, 0)),
        out_shape=jax.ShapeDtypeStruct((B, S, SSD_INNER), BF16),
        scratch_shapes=[pltpu.VMEM((SSD_N, SSD_INNER), F32)],
        compiler_params=_cparams(("parallel", "arbitrary"), 48),
        name="ssd_mixer",
    )(z, xbc, dt, *params, jnp.asarray(tri, BF16), jnp.asarray(expand, BF16))


def _token_order(src_ref, stage_ref):
    dil, per, w = src_ref.shape
    for r in range(dil):
        plane = src_ref[r].astype(F32)
        for t in range(w // LANE):
            stage_ref[t, pl.ds(r, per, stride=dil), :] = plane[:, t * LANE:(t + 1) * LANE]
    return jnp.concatenate([stage_ref[t] for t in range(w // LANE)], axis=1)


def _tail_kernel(x_ref, g1_ref, gla_ref, d0_ref, d1_ref, d2_ref, l0_ref, l1_ref, l2_ref, gm_ref, ssd_ref,
                 gates_ref, wg_ref, wd_ref, wm_ref, ws_ref, wo_ref, exp_ref,
                 nw_ref, sc_ref, sh_ref, g2_ref, w1_ref, w2_ref, fw_ref, o_ref,
                 od1_ref, od2_ref, ol1_ref, ol2_ref, *, final_norm):
    D = D_MODEL
    d0, l0 = d0_ref[...].astype(F32), l0_ref[...]
    d1, l1 = _token_order(d1_ref, od1_ref), _token_order(l1_ref, ol1_ref)
    d2, l2 = _token_order(d2_ref, od2_ref), _token_order(l2_ref, ol2_ref)
    m = jnp.maximum(jnp.maximum(l0, l1), l2)
    e0, e1, e2 = jnp.exp(l0 - m), jnp.exp(l1 - m), jnp.exp(l2 - m)
    den = e0 + e1 + e2
    expand = exp_ref[...]
    o_dil = (_dot_sel_r(e0 / den, expand) * d0 + _dot_sel_r(e1 / den, expand) * d1
             + _dot_sel_r(e2 / den, expand) * d2)
    branches = ((gla_ref[...], wg_ref), (o_dil, wd_ref), (gm_ref[...], wm_ref), (ssd_ref[...], ws_ref))
    merged = None
    for i, (o_b, w_ref) in enumerate(branches):
        g = gates_ref[:, i * D:(i + 1) * D]
        gate = (jnp.tanh(g * jnp.asarray(0.5, g.dtype)) * jnp.asarray(0.5, g.dtype)
                + jnp.asarray(0.5, g.dtype)).astype(F32)
        term = gate * _dot(o_b, w_ref[...])
        merged = term if merged is None else merged + term
    x = x_ref[...] + g1_ref[...] * _dot(merged, wo_ref[...])

    h = _modulated_norm(x, nw_ref[...], sc_ref[...], sh_ref[...]).astype(BF16)
    acc = jnp.zeros(x.shape, F32)
    for c0 in range(0, MLP_HIDDEN, D_MODEL):
        a = jnp.maximum(jnp.dot(h, w1_ref[:, c0:c0 + D_MODEL], preferred_element_type=F32), 0.0)
        acc = acc + _dot(a * a, w2_ref[c0:c0 + D_MODEL, :])
    y = x + g2_ref[...] * acc
    if final_norm:
        y = y * lax.rsqrt(jnp.mean(y * y, axis=-1, keepdims=True) + EPS) * fw_ref[...]
    o_ref[...] = y


def _mod_spec(layer, which):
    return pl.BlockSpec((None, None, None, 1, D_MODEL), lambda b, i: (layer, b, which, 0, 0))


def _tail(x, mod, o_gla, o_dil, lse_dil, o_gm, o_ssd, gates, w_gla, w_dil, w_gm, w_ssd, w_out,
          nw2, w1, w2, fw, layer, final_norm):
    B, S, D = x.shape
    TM = TM_MLP
    tok = lambda w: pl.BlockSpec((None, TM, w), lambda b, i: (b, i, 0))

    def plane(w, dil):
        if dil == 1:
            return pl.BlockSpec((None, None, TM, w), lambda b, i: (b, 0, i, 0))
        return pl.BlockSpec((None, dil, TM // dil, w), lambda b, i: (b, 0, i, 0))

    dils = [d for _, d in DIL_PATTERNS]
    assert dils[0] == 1 and len(dils) == 3
    expand = np.zeros((LANE, DIL_OUT), np.float32)
    for h in range(DIL_HEADS):
        expand[h, h * DIL_DIM:(h + 1) * DIL_DIM] = 1.0
    return pl.pallas_call(
        functools.partial(_tail_kernel, final_norm=final_norm),
        grid=(B, S // TM),
        in_specs=[tok(D), _mod_spec(layer, 2), tok(GLA_V)]
                 + [plane(DIL_OUT, d) for d in dils] + [plane(LANE, d) for d in dils]
                 + [tok(GM_W), tok(SSD_INNER), tok(N_BRANCH * D),
                    _layer_spec((GLA_V, D), layer), _layer_spec((DIL_OUT, D), layer),
                    _layer_spec((GM_W, D), layer), _layer_spec((SSD_INNER, D), layer),
                    _layer_spec((D, D), layer), _const_spec((LANE, DIL_OUT)),
                    _layer_spec((1, D), layer), _mod_spec(layer, 4), _mod_spec(layer, 3),
                    _mod_spec(layer, 5), _layer_spec((D, MLP_HIDDEN), layer),
                    _layer_spec((MLP_HIDDEN, D), layer), _const_spec((1, D))],
        out_specs=tok(D),
        out_shape=jax.ShapeDtypeStruct((B, S, D), F32),
        scratch_shapes=[pltpu.VMEM((DIL_OUT // LANE, TM, LANE), F32), pltpu.VMEM((DIL_OUT // LANE, TM, LANE), F32),
                        pltpu.VMEM((1, TM, LANE), F32), pltpu.VMEM((1, TM, LANE), F32)],
        compiler_params=_cparams(("parallel", "parallel"), 58),
        name="tail_final" if final_norm else "tail",
    )(x, mod, o_gla, *o_dil, *lse_dil, o_gm, o_ssd, gates, w_gla, w_dil, w_gm, w_ssd, w_out,
      jnp.asarray(expand, BF16), nw2, mod, mod, mod, w1, w2, fw)


def _pack_w_in(w_in):
    def cols(a, b):
        return w_in[:, :, _SP[a]:_SP[b]]

    def pad(t):
        return jnp.pad(t, ((0, 0), (0, 0), (0, LANE - t.shape[-1])))

    dil = [cols(t, t + 1)[:, :, p * DIL_OUT:(p + 1) * DIL_OUT]
           for p in range(len(DIL_PATTERNS)) for t in (5, 6, 7)]
    dil = [jnp.concatenate(dil[3 * p:3 * p + 3], axis=-1) for p in range(len(DIL_PATTERNS))]
    parts = [cols(0, 4), pad(cols(4, 5)), *dil, cols(8, 10), cols(10, 11), cols(11, 12), pad(cols(12, 13)),
             cols(13, 14)]
    assert [t.shape[-1] for t in parts] == [o[1] for o in PROJ_OUTS]
    return [t.astype(BF16) for t in parts]


def kernel(x, c, w_ada, b_ada, norm1_w, norm2_w, w_in, gla_w_a2, gla_b_a, gla_norm_w, gm_ln_w, gm_ln_b, gm_ws, gm_bs, ssd_conv_w, ssd_conv_b, ssd_dt_bias, ssd_a_log, ssd_d, ssd_norm_w, w_br_gla, w_br_dil, w_br_gm, w_br_ssd, w_out, w_mlp1, w_mlp2, final_norm_w):
    mod = _ada_mod(c, w_ada, b_ada)
    w_parts = _pack_w_in(w_in)
    wa = jnp.pad(gla_w_a2, ((0, 0), (0, LANE - GLA_RANK), (0, 0))).astype(BF16)
    w_gla, w_dil, w_gm, w_ssd, w_o, w1, w2 = (t.astype(BF16) for t in (
        w_br_gla, w_br_dil, w_br_gm, w_br_ssd, w_out, w_mlp1, w_mlp2))
    rows = lambda t: t[:, None, :]
    n1, n2 = rows(norm1_w), rows(norm2_w)
    gla_ba, gla_nw = rows(gla_b_a), rows(jnp.tile(gla_norm_w, (1, GLA_HEADS)))
    gm_lnw, gm_lnb = rows(gm_ln_w), rows(gm_ln_b)
    gm_w, gm_b = _gmlp_params(gm_ws, gm_bs)
    ssd_params = _ssd_params(ssd_dt_bias, ssd_a_log, ssd_d, ssd_norm_w)
    conv_b = rows(ssd_conv_b)
    fw = final_norm_w.reshape(1, -1)
    for l in range(DEPTH):
        gla, glr, dil0, dil1, dil2, gm, ssd_z, ssd_xbc, dt, gates = _inproj(
            x, n1, mod, ssd_conv_w, conv_b, w_parts, l)
        o_gla = _gla(gla, glr, wa, gla_ba, gla_nw, l)
        o_dil, lse_dil = _dilated([dil0, dil1, dil2])
        o_gm = _gmlp(gm, gm_lnw, gm_lnb, gm_w, gm_b, l)
        o_ssd = _ssd(ssd_z, ssd_xbc, dt, ssd_params, l)
        x = _tail(x, mod, o_gla, o_dil, lse_dil, o_gm, o_ssd, gates, w_gla, w_dil, w_gm, w_ssd, w_o,
                  n2, w1, w2, fw, l, final_norm=(l == DEPTH - 1))
    return x
```

```python
import functools

import numpy as np
import jax
import jax.numpy as jnp
from jax import lax
from jax.experimental import pallas as pl
from jax.experimental.pallas import tpu as pltpu

F32 = jnp.float32
BF16 = jnp.bfloat16

D_MODEL = 1024
DEPTH = 4
EPS = 1e-6

GLA_HEADS, GLA_DK, GLA_DV, GLA_RANK, GLA_CHUNK = 4, 32, 64, 16, 64
GLA_GATE_NORM = 16.0
DIL_PATTERNS = ((128, 1), (512, 4), (2048, 16))
DIL_HEADS, DIL_DIM, DIL_STEPS = 4, 64, 128
GM_GROUPS, GM_DIM, GM_CHUNK = 4, 64, 128
SSD_HEADS, SSD_P, SSD_GROUPS, SSD_N, SSD_CONV, SSD_CHUNK = 8, 64, 2, 128, 4, 128
N_BRANCH = 4
MLP_HIDDEN = 4 * D_MODEL

GLA_QK = GLA_HEADS * GLA_DK
GLA_V = GLA_HEADS * GLA_DV
DIL_W = len(DIL_PATTERNS) * DIL_HEADS * DIL_DIM
DIL_OUT = DIL_HEADS * DIL_DIM
GM_W = GM_GROUPS * GM_DIM
SSD_INNER = SSD_HEADS * SSD_P
SSD_BC = SSD_GROUPS * SSD_N
SSD_XBC = SSD_INNER + 2 * SSD_BC
IN_SPLITS = (GLA_QK, GLA_QK, GLA_V, GLA_V, GLA_RANK, DIL_W, DIL_W, DIL_W, GM_W, GM_W,
             SSD_INNER, SSD_XBC, SSD_HEADS, N_BRANCH * D_MODEL)
_SP = tuple(int(v) for v in np.cumsum((0,) + IN_SPLITS))

LANE = 128
NEG = -1e30

PROJ_OUTS = (("gla", 2 * GLA_QK + 2 * GLA_V, BF16, 1), ("glr", LANE, F32, 1),
             ("dil0", 3 * DIL_OUT, BF16, DIL_PATTERNS[0][1]), ("dil1", 3 * DIL_OUT, BF16, DIL_PATTERNS[1][1]),
             ("dil2", 3 * DIL_OUT, BF16, DIL_PATTERNS[2][1]),
             ("gm", 2 * GM_W, BF16, 1), ("ssd_z", SSD_INNER, BF16, 1), ("ssd_xbc", SSD_XBC, BF16, 1),
             ("dt", LANE, F32, 1), ("gates", N_BRANCH * D_MODEL, BF16, 1))
PROJ_NAMES = tuple(o[0] for o in PROJ_OUTS)

TM_PROJ = 512
TM_MLP = 512
T_GLA = 512
GLA_CUMSUM_ROWS = 256
T_GM = 1024
T_SSD = 512
COL_CHUNK = 512
DIL_QBLOCKS = 8


def _cparams(sem, vmem_mib):
    return pltpu.CompilerParams(dimension_semantics=sem, vmem_limit_bytes=vmem_mib * 1024 * 1024)


def _const_spec(shape):
    nd = len(shape)
    return pl.BlockSpec(shape, lambda *_: (0,) * nd, pipeline_mode=pl.Buffered(1))


def _layer_spec(shape, layer):
    nd = len(shape)
    return pl.BlockSpec((None,) + tuple(shape), lambda *_: (layer,) + (0,) * nd,
                        pipeline_mode=pl.Buffered(1))


def _dot(a, b):
    return jnp.dot(a.astype(BF16), b.astype(BF16), preferred_element_type=F32)


def _dot_nt(a, b):
    return lax.dot_general(a.astype(BF16), b.astype(BF16), (((1,), (1,)), ((), ())),
                           preferred_element_type=F32)


def _split(a):
    hi = a.astype(BF16)
    lo = (a - hi.astype(F32)).astype(BF16)
    return hi, lo


def _dot_sel_r(a, sel):
    hi, lo = _split(a)
    return (jnp.dot(hi, sel, preferred_element_type=F32) + jnp.dot(lo, sel, preferred_element_type=F32))


def _dot_sel_l(sel, a):
    hi, lo = _split(a)
    return (jnp.dot(sel, hi, preferred_element_type=F32) + jnp.dot(sel, lo, preferred_element_type=F32))


def _sigmoid(x):
    return 1.0 / (1.0 + jnp.exp(-x))


def _silu(x):
    return x * _sigmoid(x)


def _gelu(x):
    return 0.5 * x * (1.0 + lax.erf(x * (2.0 ** -0.5)))


def _softplus(x):
    return jnp.maximum(x, 0.0) + jnp.log1p(jnp.exp(-jnp.abs(x)))


def _modulated_norm(x, nw, sc, sh):
    y = x * lax.rsqrt(jnp.mean(x * x, axis=-1, keepdims=True) + EPS)
    return (y * nw) * (1.0 + sc) + sh


def _ada_kernel(c_ref, w_ref, b_ref, o_ref):
    o_ref[...] = _dot(_silu(c_ref[...]), w_ref[...]) + b_ref[...]


def _ada_mod(c, w_ada, b_ada):
    B, D = c.shape
    rows = 16
    c_pad = jnp.pad(c, ((0, rows - B), (0, 0)))
    n_col = w_ada.shape[-1] // D
    out = pl.pallas_call(
        _ada_kernel,
        grid=(DEPTH, n_col),
        in_specs=[pl.BlockSpec((rows, D), lambda l, j: (0, 0)),
                  pl.BlockSpec((None, D, D), lambda l, j: (l, 0, j)),
                  pl.BlockSpec((None, 1, D), lambda l, j: (l, 0, j))],
        out_specs=pl.BlockSpec((None, rows, D), lambda l, j: (l, 0, j)),
        out_shape=jax.ShapeDtypeStruct((DEPTH, rows, n_col * D), F32),
        compiler_params=_cparams(("arbitrary", "arbitrary"), 32),
        name="ada_mod",
    )(c_pad, w_ada, b_ada.reshape(DEPTH, 1, -1))
    return out[:, :B].reshape(DEPTH, B, n_col, 1, D)


def _inproj_kernel(x_ref, nw_ref, sc_ref, sh_ref, cw_ref, cb_ref, *refs):
    n_out = len(PROJ_OUTS)
    w_refs, o_refs, (stage_ref, carry_ref, win_ref) = refs[:n_out], refs[n_out:2 * n_out], refs[2 * n_out:]
    TM = x_ref.shape[0]
    KEEP = carry_ref.shape[0]

    @pl.when(pl.program_id(1) == 0)
    def _():
        carry_ref[...] = jnp.zeros_like(carry_ref)

    h = _modulated_norm(x_ref[...], nw_ref[...], sc_ref[...], sh_ref[...]).astype(BF16)
    for w_ref, o_ref, (name, width, dtype, dil) in zip(w_refs, o_refs, PROJ_OUTS):
        for c0 in range(0, width, COL_CHUNK):
            c1 = min(c0 + COL_CHUNK, width)
            res = jnp.dot(h, w_ref[:, c0:c1], preferred_element_type=F32)
            if name == "ssd_xbc":
                win_ref[0:KEEP, :] = carry_ref[:, c0:c1]
                win_ref[KEEP:, :] = res
                carry_ref[:, c0:c1] = res[TM - KEEP:]
                conv = cb_ref[:, c0:c1]
                for j in range(SSD_CONV):
                    off = KEEP - (SSD_CONV - 1) + j
                    conv = conv + cw_ref[j:j + 1, c0:c1] * win_ref[off:off + TM, :]
                o_ref[:, c0:c1] = _silu(conv).astype(dtype)
            elif dil == 1:
                o_ref[:, c0:c1] = res.astype(dtype)
            else:
                for t in range((c1 - c0) // LANE):
                    stage_ref[c0 // LANE + t] = res[:, t * LANE:(t + 1) * LANE]
        if dil > 1:
            for r in range(dil):
                for t in range(width // LANE):
                    o_ref[r, :, t * LANE:(t + 1) * LANE] = stage_ref[
                        t, pl.ds(r, TM // dil, stride=dil), :].astype(dtype)


def _inproj(x, nw, mod, conv_w, conv_b, w_parts, layer):
    B, S, D = x.shape
    TM = TM_PROJ
    out_specs, out_shape = [], []
    for _, w, dt, dil in PROJ_OUTS:
        if dil == 1:
            out_specs.append(pl.BlockSpec((None, TM, w), lambda b, i: (b, i, 0)))
            out_shape.append(jax.ShapeDtypeStruct((B, S, w), dt))
        else:
            out_specs.append(pl.BlockSpec((None, dil, TM // dil, w), lambda b, i: (b, 0, i, 0)))
            out_shape.append(jax.ShapeDtypeStruct((B, dil, S // dil, w), dt))
    return pl.pallas_call(
        _inproj_kernel,
        grid=(B, S // TM),
        in_specs=[pl.BlockSpec((None, TM, D), lambda b, i: (b, i, 0)),
                  _layer_spec((1, D), layer), _mod_spec(layer, 1), _mod_spec(layer, 0),
                  _layer_spec((SSD_CONV, SSD_XBC), layer), _layer_spec((1, SSD_XBC), layer)]
                 + [_layer_spec((D, o[1]), layer) for o in PROJ_OUTS],
        out_specs=out_specs,
        out_shape=out_shape,
        scratch_shapes=[pltpu.VMEM((3 * DIL_OUT // LANE, TM, LANE), F32),
                        pltpu.VMEM((8, SSD_XBC), F32),
                        pltpu.VMEM((TM + 8, COL_CHUNK), F32)],
        compiler_params=_cparams(("parallel", "arbitrary"), 58),
        name="inproj",
    )(x, nw, mod, mod, conv_w, conv_b, *w_parts)


def _gla_kernel(gla_ref, glr_ref, wa_ref, ba_ref, nw_ref, tri_ref, ones_ref, o_ref, st_ref):
    C, H, K, V = GLA_CHUNK, GLA_HEADS, GLA_DK, GLA_DV
    T = gla_ref.shape[0]

    @pl.when(pl.program_id(1) == 0)
    def _():
        st_ref[...] = jnp.zeros_like(st_ref)

    blk = gla_ref[...]
    q = blk[:, 0:GLA_QK].astype(F32) * (K ** -0.5)
    k = blk[:, GLA_QK:2 * GLA_QK].astype(F32)
    v = blk[:, 2 * GLA_QK:2 * GLA_QK + GLA_V]
    g = blk[:, 2 * GLA_QK + GLA_V:].astype(F32)
    pre = _dot(glr_ref[...], wa_ref[...]) + ba_ref[...]
    log_a = -_softplus(-pre) * (1.0 / GLA_GATE_NORM)
    tri = tri_ref[...]
    TB = tri.shape[0]
    b_all = jnp.concatenate([_dot_sel_l(tri, log_a[i * TB:(i + 1) * TB]) for i in range(T // TB)],
                            axis=0)

    k_head = lax.broadcasted_iota(jnp.int32, (C, GLA_QK), 1) // K
    v_head = lax.broadcasted_iota(jnp.int32, (C, GLA_V), 1) // V
    row = lax.broadcasted_iota(jnp.int32, (H * C, C), 0)
    causal = (row % C) >= lax.broadcasted_iota(jnp.int32, (H * C, C), 1)
    st_mask = (lax.broadcasted_iota(jnp.int32, (GLA_V, GLA_QK), 0) // V
               == lax.broadcasted_iota(jnp.int32, (GLA_V, GLA_QK), 1) // K)

    st = st_ref[...]
    outs = []
    for c in range(T // C):
        rs = slice(c * C, (c + 1) * C)
        b = b_all[rs]
        b_last = b[C - 1:C]
        q_t = q[rs] * jnp.exp(b)
        k_t = k[rs] * jnp.exp(-b)
        k_dec = k[rs] * jnp.exp(b_last - b)
        v_c = v[rs]
        q_stack = jnp.concatenate([jnp.where(k_head == h, q_t, 0.0) for h in range(H)], axis=0)
        att = jnp.where(causal, _dot_nt(q_stack, k_t), 0.0)
        o_all = _dot(att, v_c)
        o = _dot_nt(q_t, st)
        for h in range(H):
            o = o + jnp.where(v_head == h, o_all[h * C:(h + 1) * C], 0.0)
        kv_t = _dot(v_c.astype(F32).T, k_dec)
        st = st * jnp.exp(b_last) + jnp.where(st_mask, kv_t, 0.0)
        outs.append(o)
    st_ref[...] = st

    o = jnp.concatenate(outs, axis=0)
    ms = _dot_sel_r(o * o, ones_ref[...]) * (1.0 / V)
    o = o * lax.rsqrt(ms + EPS) * nw_ref[...]
    o_ref[...] = (o * _silu(g)).astype(o_ref.dtype)


def _gla(gla, glr, wa, ba, nw, layer):
    B, S, W = gla.shape
    T, C, TB = T_GLA, GLA_CHUNK, GLA_CUMSUM_ROWS
    idx = np.arange(TB)
    tri = ((idx[:, None] // C == idx[None, :] // C) & (idx[:, None] >= idx[None, :]))
    hv = np.arange(GLA_V) // GLA_DV
    ones = hv[:, None] == hv[None, :]
    return pl.pallas_call(
        _gla_kernel,
        grid=(B, S // T),
        in_specs=[pl.BlockSpec((None, T, W), lambda b, i: (b, i, 0)),
                  pl.BlockSpec((None, T, LANE), lambda b, i: (b, i, 0)),
                  _layer_spec((LANE, GLA_QK), layer), _layer_spec((1, GLA_QK), layer),
                  _layer_spec((1, GLA_V), layer), _const_spec((TB, TB)), _const_spec((GLA_V, GLA_V))],
        out_specs=pl.BlockSpec((None, T, GLA_V), lambda b, i: (b, i, 0)),
        out_shape=jax.ShapeDtypeStruct((B, S, GLA_V), BF16),
        scratch_shapes=[pltpu.VMEM((GLA_V, GLA_QK), F32)],
        compiler_params=_cparams(("parallel", "arbitrary"), 32),
        name="gla_mixer",
    )(gla, glr, wa, ba, nw, jnp.asarray(tri, BF16), jnp.asarray(ones, BF16))


def _dil_kernel(q_ref, kp_ref, k_ref, vp_ref, v_ref, bias_ref, o_ref, lse_ref):
    H, E, BLK = DIL_HEADS, DIL_DIM, DIL_STEPS
    n_res, n_blk = q_ref.shape[0], q_ref.shape[1] // BLK
    first = jnp.minimum(pl.program_id(2), 1)
    head = lax.broadcasted_iota(jnp.int32, (BLK, H * E), 1) // E
    lane = lax.broadcasted_iota(jnp.int32, (BLK, LANE), 1)
    for r in range(n_res):
        kcat = jnp.concatenate([kp_ref[r], k_ref[r]], axis=0)
        vcat = jnp.concatenate([vp_ref[r], v_ref[r]], axis=0)
        for i in range(n_blk):
            rows = slice(i * BLK, (i + 1) * BLK)
            q = q_ref[r, rows, :] * jnp.asarray(E ** -0.5, BF16)
            kc = kcat[i * BLK:(i + 2) * BLK]
            vc = vcat[i * BLK:(i + 2) * BLK]
            bias = bias_ref[first] if i == 0 else bias_ref[1]
            q_stack = jnp.concatenate([jnp.where(head == h, q, jnp.zeros_like(q)) for h in range(H)],
                                      axis=0)
            s = _dot_nt(q_stack, kc) + bias
            m = jnp.max(s, axis=-1, keepdims=True)
            p = jnp.exp(s - m)
            l = jnp.sum(p, axis=-1, keepdims=True)
            o_all = _dot(p, vc)
            lse = m + jnp.log(l)
            inv_l = 1.0 / l
            o = o_all[0:BLK] * inv_l[0:BLK]
            lse_out = jnp.zeros((BLK, LANE), F32)
            for h in range(H):
                hs = slice(h * BLK, (h + 1) * BLK)
                if h > 0:
                    o = jnp.where(head == h, o_all[hs] * inv_l[hs], o)
                lse_out = jnp.where(lane == h, lse[hs], lse_out)
            o_ref[r, rows, :] = o.astype(o_ref.dtype)
            lse_ref[r, rows, :] = lse_out


def _dil_bias(window, dil, slopes):
    BLK = DIL_STEPS
    steps = np.arange(BLK)[:, None] + BLK - np.arange(2 * BLK)[None, :]
    ok = (steps >= 0) & (steps <= window // dil)
    bias = -slopes[:, None, None] * (steps * dil)[None].astype(np.float32)
    full = np.where(ok[None], bias, NEG).astype(np.float32)
    first = np.where((np.arange(2 * BLK) >= BLK)[None, None, :], full, NEG)
    return np.stack([first.reshape(-1, 2 * BLK), full.reshape(-1, 2 * BLK)]).astype(np.float32)


def _dil_pattern(qkv, window, dil, slopes):
    B, _, n, _ = qkv.shape
    HE, BLK = DIL_OUT, DIL_STEPS
    n_blk = min(DIL_QBLOCKS, n // BLK)
    n_res = min(dil, DIL_QBLOCKS // n_blk)
    rows = n_blk * BLK
    assert n % rows == 0 and dil % n_res == 0
    cur = lambda t: pl.BlockSpec((None, n_res, rows, HE), lambda b, r, j: (b, r, j, t))
    prev = lambda t: pl.BlockSpec((None, n_res, BLK, HE),
                                  lambda b, r, j: (b, r, jnp.maximum(j * n_blk - 1, 0), t))
    return pl.pallas_call(
        _dil_kernel,
        grid=(B, dil // n_res, n // rows),
        in_specs=[cur(0), prev(1), cur(1), prev(2), cur(2),
                  _const_spec((2, DIL_HEADS * BLK, 2 * BLK))],
        out_specs=[pl.BlockSpec((None, n_res, rows, HE), lambda b, r, j: (b, r, j, 0)),
                   pl.BlockSpec((None, n_res, rows, LANE), lambda b, r, j: (b, r, j, 0))],
        out_shape=[jax.ShapeDtypeStruct((B, dil, n, HE), BF16),
                   jax.ShapeDtypeStruct((B, dil, n, LANE), F32)],
        compiler_params=_cparams(("parallel", "parallel", "arbitrary"), 32),
        name=f"dil_attn_d{dil}",
    )(qkv, qkv, qkv, qkv, qkv, jnp.asarray(_dil_bias(window, dil, slopes)))


def _dilated(qkvs):
    n_pat = len(DIL_PATTERNS)
    n_h = n_pat * DIL_HEADS
    slopes = (2.0 ** (-8.0 * np.arange(1, n_h + 1) / n_h)).astype(np.float32).reshape(n_pat, DIL_HEADS)
    outs = []
    for p, (w, d) in enumerate(DIL_PATTERNS):
        qkv = qkvs[p] if qkvs[p].ndim == 4 else qkvs[p][:, None]
        assert qkv.shape[1] == d
        outs.append(_dil_pattern(qkv, w, d, slopes[p]))
    return [o for o, _ in outs], [l for _, l in outs]


def _gm_kernel(gm_ref, lnw_ref, lnb_ref, w_ref, bs_ref, o_ref):
    G, E, C = GM_GROUPS, GM_DIM, GM_CHUNK
    T = gm_ref.shape[0]
    blk = gm_ref[...].astype(F32)
    u = _gelu(blk[:, :GM_W])
    v = _gelu(blk[:, GM_W:])
    mu = jnp.mean(v, axis=-1, keepdims=True)
    var = jnp.mean(jnp.square(v - mu), axis=-1, keepdims=True)
    v = (v - mu) * lax.rsqrt(var + EPS) * lnw_ref[...] + lnb_ref[...]
    group = lax.broadcasted_iota(jnp.int32, (C, GM_W), 1) // E
    w = w_ref[...]
    bs = bs_ref[...]
    for c in range(T // C):
        rs = slice(c * C, (c + 1) * C)
        r = _dot(w, v[rs])
        sv = bs
        for gi in range(G):
            sv = sv + jnp.where(group == gi, r[gi * C:(gi + 1) * C], 0.0)
        o_ref[rs, :] = (u[rs] * sv).astype(o_ref.dtype)


def _gmlp_params(ws, bs):
    C = GM_CHUNK
    tril = jnp.tril(jnp.ones((C, C), bool))
    w_stack = jnp.where(tril, ws, 0.0).reshape(-1, GM_GROUPS * C, C).astype(BF16)
    bs_exp = jnp.repeat(jnp.swapaxes(bs, 1, 2), GM_DIM, axis=2)
    return w_stack, bs_exp


def _gmlp(gm, lnw, lnb, w_stack, bs_exp, layer):
    B, S, W = gm.shape
    C = GM_CHUNK
    return pl.pallas_call(
        _gm_kernel,
        grid=(B, S // T_GM),
        in_specs=[pl.BlockSpec((None, T_GM, W), lambda b, i: (b, i, 0)),
                  _layer_spec((1, GM_W), layer), _layer_spec((1, GM_W), layer),
                  _layer_spec((GM_GROUPS * C, C), layer), _layer_spec((C, GM_W), layer)],
        out_specs=pl.BlockSpec((None, T_GM, GM_W), lambda b, i: (b, i, 0)),
        out_shape=jax.ShapeDtypeStruct((B, S, GM_W), BF16),
        compiler_params=_cparams(("parallel", "parallel"), 32),
        name="gmlp_mixer",
    )(gm, lnw, lnb, w_stack, bs_exp)


def _ssd_kernel(z_ref, xbc_ref, dt_ref, dtb_ref, alog_ref, dskip_ref, nw_ref, tri_ref, exp_ref,
                o_ref, st_ref):
    H, P, G, N, L = SSD_HEADS, SSD_P, SSD_GROUPS, SSD_N, SSD_CHUNK
    R = H // G
    GW = R * P
    T = z_ref.shape[0]

    @pl.when(pl.program_id(1) == 0)
    def _():
        st_ref[...] = jnp.zeros_like(st_ref)

    dt_all = _softplus(dt_ref[...] + dtb_ref[...])
    da_all = dt_all * (-jnp.exp(alog_ref[...]))
    tri = tri_ref[...]
    expand = exp_ref[...]
    lower = lax.broadcasted_iota(jnp.int32, (L, L), 0) >= lax.broadcasted_iota(jnp.int32, (L, L), 1)
    head_in_group = lax.broadcasted_iota(jnp.int32, (L, GW), 1) // P

    for c in range(T // L):
        rs = slice(c * L, (c + 1) * L)
        xs = xbc_ref[rs, :SSD_INNER].astype(F32)
        dt_c = dt_all[rs]
        cs = _dot_sel_l(tri, da_all[rs])
        cs_t = cs.T
        cs_last = cs[L - 1:L]
        dt_e = _dot(dt_c, expand)
        dec_e = _dot(jnp.exp(cs_last - cs) * dt_c, expand)
        ecs_e = _dot_sel_r(jnp.exp(cs), expand)
        x_dt = (xs * dt_e).astype(BF16)
        x_dec = (xs * dec_e).astype(BF16)
        ys = []
        for gi in range(G):
            gs = slice(gi * GW, (gi + 1) * GW)
            b_g = xbc_ref[rs, SSD_INNER + gi * N:SSD_INNER + (gi + 1) * N]
            c_g = xbc_ref[rs, SSD_INNER + SSD_BC + gi * N:SSD_INNER + SSD_BC + (gi + 1) * N]
            cb = _dot_nt(c_g, b_g)
            y_g = _dot(c_g, st_ref[:, gs]) * ecs_e[:, gs]
            y_intra = None
            for r in range(R):
                h = gi * R + r
                seg = cs[:, h:h + 1] - cs_t[h:h + 1, :]
                wgt = cb * jnp.exp(jnp.where(lower, seg, NEG))
                y_h = _dot(wgt, x_dt[:, gs])
                y_intra = y_h if r == 0 else jnp.where(head_in_group == r, y_h, y_intra)
            st_ref[:, gs] = st_ref[:, gs] * ecs_e[L - 1:L, gs] + _dot(b_g.astype(F32).T, x_dec[:, gs])
            ys.append(y_g + y_intra)
        y = jnp.concatenate(ys, axis=1) + xs * dskip_ref[...]
        y = y * _silu(z_ref[rs, :].astype(F32))
        normed = []
        for gi in range(G):
            y_g = y[:, gi * GW:(gi + 1) * GW]
            normed.append(y_g * lax.rsqrt(jnp.mean(y_g * y_g, axis=-1, keepdims=True) + EPS))
        o_ref[rs, :] = (jnp.concatenate(normed, axis=1) * nw_ref[...]).astype(o_ref.dtype)


def _ssd_params(dt_bias, a_log, d_skip, norm_w):
    pad = lambda t: jnp.pad(t, ((0, 0), (0, LANE - t.shape[1])))[:, None, :]
    return (pad(dt_bias), pad(a_log), jnp.repeat(d_skip, SSD_P, axis=1)[:, None, :], norm_w[:, None, :])


def _ssd(z, xbc, dt, params, layer):
    B, S, _ = z.shape
    L = SSD_CHUNK
    tri = np.tril(np.ones((L, L), np.float32))
    expand = np.zeros((LANE, SSD_INNER), np.float32)
    for h in range(SSD_HEADS):
        expand[h, h * SSD_P:(h + 1) * SSD_P] = 1.0
    return pl.pallas_call(
        _ssd_kernel,
        grid=(B, S // T_SSD),
        in_specs=[pl.BlockSpec((None, T_SSD, SSD_INNER), lambda b, i: (b, i, 0)),
                  pl.BlockSpec((None, T_SSD, SSD_XBC), lambda b, i: (b, i, 0)),
                  pl.BlockSpec((None, T_SSD, LANE), lambda b, i: (b, i, 0)),
                  _layer_spec((1, LANE), layer), _layer_spec((1, LANE), layer),
                  _layer_spec((1, SSD_INNER), layer), _layer_spec((1, SSD_INNER), layer),
                  _const_spec((L, L)), _const_spec((LANE, SSD_INNER))],
        out_specs=pl.BlockSpec((None, T_SSD, SSD_INNER), lambda b, i: (b, i, 0)),
        out_shape=jax.ShapeDtypeStruct((B, S, SSD_INNER), BF16),
        scratch_shapes=[pltpu.VMEM((SSD_N, SSD_INNER), F32)],
        compiler_params=_cparams(("parallel", "arbitrary"), 48),
        name="ssd_mixer",
    )(z, xbc, dt, *params, jnp.asarray(tri, BF16), jnp.asarray(expand, BF16))


def _token_order(src_ref, stage_ref):
    dil, per, w = src_ref.shape
    for r in range(dil):
        plane = src_ref[r].astype(F32)
        for t in range(w // LANE):
            stage_ref[t, pl.ds(r, per, stride=dil), :] = plane[:, t * LANE:(t + 1) * LANE]
    return jnp.concatenate([stage_ref[t] for t in range(w // LANE)], axis=1)


def _tail_kernel(x_ref, g1_ref, gla_ref, d0_ref, d1_ref, d2_ref, l0_ref, l1_ref, l2_ref, gm_ref, ssd_ref,
                 gates_ref, wg_ref, wd_ref, wm_ref, ws_ref, wo_ref, exp_ref,
                 nw_ref, sc_ref, sh_ref, g2_ref, w1_ref, w2_ref, fw_ref, o_ref,
                 od1_ref, od2_ref, ol1_ref, ol2_ref, *, final_norm):
    D = D_MODEL
    d0, l0 = d0_ref[...].astype(F32), l0_ref[...]
    d1, l1 = _token_order(d1_ref, od1_ref), _token_order(l1_ref, ol1_ref)
    d2, l2 = _token_order(d2_ref, od2_ref), _token_order(l2_ref, ol2_ref)
    m = jnp.maximum(jnp.maximum(l0, l1), l2)
    e0, e1, e2 = jnp.exp(l0 - m), jnp.exp(l1 - m), jnp.exp(l2 - m)
    den = e0 + e1 + e2
    expand = exp_ref[...]
    o_dil = (_dot_sel_r(e0 / den, expand) * d0 + _dot_sel_r(e1 / den, expand) * d1
             + _dot_sel_r(e2 / den, expand) * d2)
    branches = ((gla_ref[...], wg_ref), (o_dil, wd_ref), (gm_ref[...], wm_ref), (ssd_ref[...], ws_ref))
    merged = None
    for i, (o_b, w_ref) in enumerate(branches):
        g = gates_ref[:, i * D:(i + 1) * D]
        gate = (jnp.tanh(g * jnp.asarray(0.5, g.dtype)) * jnp.asarray(0.5, g.dtype)
                + jnp.asarray(0.5, g.dtype)).astype(F32)
        term = gate * _dot(o_b, w_ref[...])
        merged = term if merged is None else merged + term
    x = x_ref[...] + g1_ref[...] * _dot(merged, wo_ref[...])

    h = _modulated_norm(x, nw_ref[...], sc_ref[...], sh_ref[...]).astype(BF16)
    acc = jnp.zeros(x.shape, F32)
    for c0 in range(0, MLP_HIDDEN, D_MODEL):
        a = jnp.maximum(jnp.dot(h, w1_ref[:, c0:c0 + D_MODEL], preferred_element_type=F32), 0.0)
        acc = acc + _dot(a * a, w2_ref[c0:c0 + D_MODEL, :])
    y = x + g2_ref[...] * acc
    if final_norm:
        y = y * lax.rsqrt(jnp.mean(y * y, axis=-1, keepdims=True) + EPS) * fw_ref[...]
    o_ref[...] = y


def _mod_spec(layer, which):
    return pl.BlockSpec((None, None, None, 1, D_MODEL), lambda b, i: (layer, b, which, 0, 0))


def _tail(x, mod, o_gla, o_dil, lse_dil, o_gm, o_ssd, gates, w_gla, w_dil, w_gm, w_ssd, w_out,
          nw2, w1, w2, fw, layer, final_norm):
    B, S, D = x.shape
    TM = TM_MLP
    tok = lambda w: pl.BlockSpec((None, TM, w), lambda b, i: (b, i, 0))

    def plane(w, dil):
        if dil == 1:
            return pl.BlockSpec((None, None, TM, w), lambda b, i: (b, 0, i, 0))
        return pl.BlockSpec((None, dil, TM // dil, w), lambda b, i: (b, 0, i, 0))

    dils = [d for _, d in DIL_PATTERNS]
    assert dils[0] == 1 and len(dils) == 3
    expand = np.zeros((LANE, DIL_OUT), np.float32)
    for h in range(DIL_HEADS):
        expand[h, h * DIL_DIM:(h + 1) * DIL_DIM] = 1.0
    return pl.pallas_call(
        functools.partial(_tail_kernel, final_norm=final_norm),
        grid=(B, S // TM),
        in_specs=[tok(D), _mod_spec(layer, 2), tok(GLA_V)]
                 + [plane(DIL_OUT, d) for d in dils] + [plane(LANE, d) for d in dils]
                 + [tok(GM_W), tok(SSD_INNER), tok(N_BRANCH * D),
                    _layer_spec((GLA_V, D), layer), _layer_spec((DIL_OUT, D), layer),
                    _layer_spec((GM_W, D), layer), _layer_spec((SSD_INNER, D), layer),
                    _layer_spec((D, D), layer), _const_spec((LANE, DIL_OUT)),
                    _layer_spec((1, D), layer), _mod_spec(layer, 4), _mod_spec(layer, 3),
                    _mod_spec(layer, 5), _layer_spec((D, MLP_HIDDEN), layer),
                    _layer_spec((MLP_HIDDEN, D), layer), _const_spec((1, D))],
        out_specs=tok(D),
        out_shape=jax.ShapeDtypeStruct((B, S, D), F32),
        scratch_shapes=[pltpu.VMEM((DIL_OUT // LANE, TM, LANE), F32), pltpu.VMEM((DIL_OUT // LANE, TM, LANE), F32),
                        pltpu.VMEM((1, TM, LANE), F32), pltpu.VMEM((1, TM, LANE), F32)],
        compiler_params=_cparams(("parallel", "parallel"), 58),
        name="tail_final" if final_norm else "tail",
    )(x, mod, o_gla, *o_dil, *lse_dil, o_gm, o_ssd, gates, w_gla, w_dil, w_gm, w_ssd, w_out,
      jnp.asarray(expand, BF16), nw2, mod, mod, mod, w1, w2, fw)


def _repack_plan():
    whole = lambda a, b: [(_SP[a], _SP[b] - _SP[a], _SP[b] - _SP[a])]
    plan = {"gla": whole(0, 4), "glr": [(_SP[4], LANE, GLA_RANK)], "gm": whole(8, 10),
            "ssd_z": whole(10, 11), "ssd_xbc": whole(11, 12), "dt": [(_SP[12], LANE, SSD_HEADS)],
            "gates": whole(13, 14)}
    for p in range(len(DIL_PATTERNS)):
        plan[f"dil{p}"] = [(_SP[t] + p * DIL_OUT, DIL_OUT, DIL_OUT) for t in (5, 6, 7)]
    plan = [plan[name] for name in PROJ_NAMES]
    assert [sum(w for _, w, _ in pieces) for pieces in plan] == [o[1] for o in PROJ_OUTS]
    return plan


def _repack_kernel(w_ref, *o_refs):
    rows = w_ref.shape[0]
    lane = lax.broadcasted_iota(jnp.int32, (rows, LANE), 1)
    for o_ref, pieces in zip(o_refs, _repack_plan()):
        col = 0
        for start, width, valid in pieces:
            for c0 in range(0, width, COL_CHUNK):
                n = min(COL_CHUNK, width - c0)
                v = w_ref[:, start + c0:start + c0 + n]
                if valid < width:
                    v = jnp.where(lane < valid, v, 0.0)
                o_ref[:, col + c0:col + c0 + n] = v.astype(o_ref.dtype)
            col += width


def _pack_w_in(w_in):
    L, D, W = w_in.shape
    RB = 256
    return pl.pallas_call(
        _repack_kernel,
        grid=(L, D // RB),
        in_specs=[pl.BlockSpec((None, RB, W), lambda l, r: (l, r, 0))],
        out_specs=[pl.BlockSpec((None, RB, o[1]), lambda l, r: (l, r, 0)) for o in PROJ_OUTS],
        out_shape=[jax.ShapeDtypeStruct((L, D, o[1]), BF16) for o in PROJ_OUTS],
        compiler_params=_cparams(("parallel", "parallel"), 48),
        name="repack_w_in",
    )(w_in)


def kernel(x, c, w_ada, b_ada, norm1_w, norm2_w, w_in, gla_w_a2, gla_b_a, gla_norm_w, gm_ln_w, gm_ln_b, gm_ws, gm_bs, ssd_conv_w, ssd_conv_b, ssd_dt_bias, ssd_a_log, ssd_d, ssd_norm_w, w_br_gla, w_br_dil, w_br_gm, w_br_ssd, w_out, w_mlp1, w_mlp2, final_norm_w):
    mod = _ada_mod(c, w_ada, b_ada)
    w_parts = _pack_w_in(w_in)
    wa = jnp.pad(gla_w_a2, ((0, 0), (0, LANE - GLA_RANK), (0, 0))).astype(BF16)
    w_gla, w_dil, w_gm, w_ssd, w_o, w1, w2 = (t.astype(BF16) for t in (
        w_br_gla, w_br_dil, w_br_gm, w_br_ssd, w_out, w_mlp1, w_mlp2))
    rows = lambda t: t[:, None, :]
    n1, n2 = rows(norm1_w), rows(norm2_w)
    gla_ba, gla_nw = rows(gla_b_a), rows(jnp.tile(gla_norm_w, (1, GLA_HEADS)))
    gm_lnw, gm_lnb = rows(gm_ln_w), rows(gm_ln_b)
    gm_w, gm_b = _gmlp_params(gm_ws, gm_bs)
    ssd_params = _ssd_params(ssd_dt_bias, ssd_a_log, ssd_d, ssd_norm_w)
    conv_b = rows(ssd_conv_b)
    fw = final_norm_w.reshape(1, -1)
    for l in range(DEPTH):
        p = dict(zip(PROJ_NAMES, _inproj(x, n1, mod, ssd_conv_w, conv_b, w_parts, l)))
        o_gla = _gla(p["gla"], p["glr"], wa, gla_ba, gla_nw, l)
        o_dil, lse_dil = _dilated([p["dil0"], p["dil1"], p["dil2"]])
        o_gm = _gmlp(p["gm"], gm_lnw, gm_lnb, gm_w, gm_b, l)
        o_ssd = _ssd(p["ssd_z"], p["ssd_xbc"], p["dt"], ssd_params, l)
        x = _tail(x, mod, o_gla, o_dil, lse_dil, o_gm, o_ssd, p["gates"], w_gla, w_dil, w_gm, w_ssd, w_o,
                  n2, w1, w2, fw, l, final_norm=(l == DEPTH - 1))
    return x
```

```python
import functools

import numpy as np
import jax
import jax.numpy as jnp
from jax import lax
from jax.experimental import pallas as pl
from jax.experimental.pallas import tpu as pltpu

F32 = jnp.float32
BF16 = jnp.bfloat16

D_MODEL = 1024
DEPTH = 4
EPS = 1e-6

GLA_HEADS, GLA_DK, GLA_DV, GLA_RANK, GLA_CHUNK = 4, 32, 64, 16, 64
GLA_GATE_NORM = 16.0
DIL_PATTERNS = ((128, 1), (512, 4), (2048, 16))
DIL_HEADS, DIL_DIM, DIL_STEPS = 4, 64, 128
GM_GROUPS, GM_DIM, GM_CHUNK = 4, 64, 128
SSD_HEADS, SSD_P, SSD_GROUPS, SSD_N, SSD_CONV, SSD_CHUNK = 8, 64, 2, 128, 4, 128
N_BRANCH = 4
MLP_HIDDEN = 4 * D_MODEL

GLA_QK = GLA_HEADS * GLA_DK
GLA_V = GLA_HEADS * GLA_DV
DIL_W = len(DIL_PATTERNS) * DIL_HEADS * DIL_DIM
DIL_OUT = DIL_HEADS * DIL_DIM
GM_W = GM_GROUPS * GM_DIM
SSD_INNER = SSD_HEADS * SSD_P
SSD_BC = SSD_GROUPS * SSD_N
SSD_XBC = SSD_INNER + 2 * SSD_BC
IN_SPLITS = (GLA_QK, GLA_QK, GLA_V, GLA_V, GLA_RANK, DIL_W, DIL_W, DIL_W, GM_W, GM_W,
             SSD_INNER, SSD_XBC, SSD_HEADS, N_BRANCH * D_MODEL)
_SP = tuple(int(v) for v in np.cumsum((0,) + IN_SPLITS))

LANE = 128
NEG = -1e30

PROJ_OUTS = (("gla", 2 * GLA_QK + 2 * GLA_V, BF16, 1), ("glr", LANE, F32, 1),
             ("dil0", 3 * DIL_OUT, BF16, DIL_PATTERNS[0][1]), ("dil1", 3 * DIL_OUT, BF16, DIL_PATTERNS[1][1]),
             ("dil2", 3 * DIL_OUT, BF16, DIL_PATTERNS[2][1]),
             ("gm", 2 * GM_W, BF16, 1), ("ssd_z", SSD_INNER, BF16, 1), ("ssd_xbc", SSD_XBC, BF16, 1),
             ("dt", LANE, F32, 1), ("gates", N_BRANCH * D_MODEL, BF16, 1))
PROJ_NAMES = tuple(o[0] for o in PROJ_OUTS)

TM_PROJ = 512
TM_MLP = 512
T_GLA = 512
GLA_CUMSUM_ROWS = 256
T_GM = 1024
T_SSD = 512
COL_CHUNK = 512
DIL_QBLOCKS = 8
DIL_LOOKAHEAD = 2


def _cparams(sem, vmem_mib):
    return pltpu.CompilerParams(dimension_semantics=sem, vmem_limit_bytes=vmem_mib * 1024 * 1024)


def _const_spec(shape):
    nd = len(shape)
    return pl.BlockSpec(shape, lambda *_: (0,) * nd, pipeline_mode=pl.Buffered(1))


def _layer_spec(shape, layer):
    nd = len(shape)
    return pl.BlockSpec((None,) + tuple(shape), lambda *_: (layer,) + (0,) * nd,
                        pipeline_mode=pl.Buffered(1))


def _dot(a, b):
    return jnp.dot(a.astype(BF16), b.astype(BF16), preferred_element_type=F32)


def _dot_nt(a, b):
    return lax.dot_general(a.astype(BF16), b.astype(BF16), (((1,), (1,)), ((), ())),
                           preferred_element_type=F32)


def _split(a):
    hi = a.astype(BF16)
    lo = (a - hi.astype(F32)).astype(BF16)
    return hi, lo


def _dot_sel_r(a, sel):
    hi, lo = _split(a)
    return (jnp.dot(hi, sel, preferred_element_type=F32) + jnp.dot(lo, sel, preferred_element_type=F32))


def _dot_sel_l(sel, a):
    hi, lo = _split(a)
    return (jnp.dot(sel, hi, preferred_element_type=F32) + jnp.dot(sel, lo, preferred_element_type=F32))


def _sigmoid(x):
    return 1.0 / (1.0 + jnp.exp(-x))


def _silu(x):
    return x * _sigmoid(x)


def _gelu(x):
    return 0.5 * x * (1.0 + lax.erf(x * (2.0 ** -0.5)))


def _softplus(x):
    return jnp.maximum(x, 0.0) + jnp.log1p(jnp.exp(-jnp.abs(x)))


def _modulated_norm(x, nw, sc, sh):
    y = x * lax.rsqrt(jnp.mean(x * x, axis=-1, keepdims=True) + EPS)
    return (y * nw) * (1.0 + sc) + sh


def _ada_kernel(c_ref, w_ref, b_ref, o_ref):
    o_ref[...] = _dot(_silu(c_ref[...]), w_ref[...]) + b_ref[...]


def _ada_mod(c, w_ada, b_ada):
    B, D = c.shape
    rows = 16
    c_pad = jnp.pad(c, ((0, rows - B), (0, 0)))
    n_col = w_ada.shape[-1] // D
    out = pl.pallas_call(
        _ada_kernel,
        grid=(DEPTH, n_col),
        in_specs=[pl.BlockSpec((rows, D), lambda l, j: (0, 0)),
                  pl.BlockSpec((None, D, D), lambda l, j: (l, 0, j)),
                  pl.BlockSpec((None, 1, D), lambda l, j: (l, 0, j))],
        out_specs=pl.BlockSpec((None, rows, D), lambda l, j: (l, 0, j)),
        out_shape=jax.ShapeDtypeStruct((DEPTH, rows, n_col * D), F32),
        compiler_params=_cparams(("arbitrary", "arbitrary"), 32),
        name="ada_mod",
    )(c_pad, w_ada, b_ada.reshape(DEPTH, 1, -1))
    return out[:, :B].reshape(DEPTH, B, n_col, 1, D)


def _inproj_kernel(x_ref, nw_ref, sc_ref, sh_ref, cw_ref, cb_ref, *refs):
    n_out = len(PROJ_OUTS)
    w_refs, o_refs, (stage_ref, carry_ref, win_ref) = refs[:n_out], refs[n_out:2 * n_out], refs[2 * n_out:]
    TM = x_ref.shape[0]
    KEEP = carry_ref.shape[0]

    @pl.when(pl.program_id(1) == 0)
    def _():
        carry_ref[...] = jnp.zeros_like(carry_ref)

    h = _modulated_norm(x_ref[...], nw_ref[...], sc_ref[...], sh_ref[...]).astype(BF16)
    for w_ref, o_ref, (name, width, dtype, dil) in zip(w_refs, o_refs, PROJ_OUTS):
        for c0 in range(0, width, COL_CHUNK):
            c1 = min(c0 + COL_CHUNK, width)
            res = jnp.dot(h, w_ref[:, c0:c1], preferred_element_type=F32)
            if name == "ssd_xbc":
                win_ref[0:KEEP, :] = carry_ref[:, c0:c1]
                win_ref[KEEP:, :] = res
                carry_ref[:, c0:c1] = res[TM - KEEP:]
                conv = cb_ref[:, c0:c1]
                for j in range(SSD_CONV):
                    off = KEEP - (SSD_CONV - 1) + j
                    conv = conv + cw_ref[j:j + 1, c0:c1] * win_ref[off:off + TM, :]
                o_ref[:, c0:c1] = _silu(conv).astype(dtype)
            elif dil == 1:
                o_ref[:, c0:c1] = res.astype(dtype)
            else:
                for t in range((c1 - c0) // LANE):
                    stage_ref[c0 // LANE + t] = res[:, t * LANE:(t + 1) * LANE]
        if dil > 1:
            for r in range(dil):
                for t in range(width // LANE):
                    o_ref[r, :, t * LANE:(t + 1) * LANE] = stage_ref[
                        t, pl.ds(r, TM // dil, stride=dil), :].astype(dtype)


def _inproj(x, nw, mod, conv_w, conv_b, w_parts, layer):
    B, S, D = x.shape
    TM = TM_PROJ
    out_specs, out_shape = [], []
    for _, w, dt, dil in PROJ_OUTS:
        if dil == 1:
            out_specs.append(pl.BlockSpec((None, TM, w), lambda b, i: (b, i, 0)))
            out_shape.append(jax.ShapeDtypeStruct((B, S, w), dt))
        else:
            out_specs.append(pl.BlockSpec((None, dil, TM // dil, w), lambda b, i: (b, 0, i, 0)))
            out_shape.append(jax.ShapeDtypeStruct((B, dil, S // dil, w), dt))
    return pl.pallas_call(
        _inproj_kernel,
        grid=(B, S // TM),
        in_specs=[pl.BlockSpec((None, TM, D), lambda b, i: (b, i, 0)),
                  _layer_spec((1, D), layer), _mod_spec(layer, 1), _mod_spec(layer, 0),
                  _layer_spec((SSD_CONV, SSD_XBC), layer), _layer_spec((1, SSD_XBC), layer)]
                 + [_layer_spec((D, o[1]), layer) for o in PROJ_OUTS],
        out_specs=out_specs,
        out_shape=out_shape,
        scratch_shapes=[pltpu.VMEM((3 * DIL_OUT // LANE, TM, LANE), F32),
                        pltpu.VMEM((8, SSD_XBC), F32),
                        pltpu.VMEM((TM + 8, COL_CHUNK), F32)],
        compiler_params=_cparams(("parallel", "arbitrary"), 58),
        name="inproj",
    )(x, nw, mod, mod, conv_w, conv_b, *w_parts)


def _gla_kernel(gla_ref, glr_ref, wa_ref, ba_ref, nw_ref, tri_ref, ones_ref, o_ref, st_ref):
    C, H, K, V = GLA_CHUNK, GLA_HEADS, GLA_DK, GLA_DV
    T = gla_ref.shape[0]

    @pl.when(pl.program_id(1) == 0)
    def _():
        st_ref[...] = jnp.zeros_like(st_ref)

    blk = gla_ref[...]
    q = blk[:, 0:GLA_QK].astype(F32) * (K ** -0.5)
    k = blk[:, GLA_QK:2 * GLA_QK].astype(F32)
    v = blk[:, 2 * GLA_QK:2 * GLA_QK + GLA_V]
    g = blk[:, 2 * GLA_QK + GLA_V:].astype(F32)
    pre = _dot(glr_ref[...], wa_ref[...]) + ba_ref[...]
    log_a = -_softplus(-pre) * (1.0 / GLA_GATE_NORM)
    tri = tri_ref[...]
    TB = tri.shape[0]
    b_all = jnp.concatenate([_dot_sel_l(tri, log_a[i * TB:(i + 1) * TB]) for i in range(T // TB)],
                            axis=0)

    k_head = lax.broadcasted_iota(jnp.int32, (C, GLA_QK), 1) // K
    v_head = lax.broadcasted_iota(jnp.int32, (C, GLA_V), 1) // V
    row = lax.broadcasted_iota(jnp.int32, (H * C, C), 0)
    causal = (row % C) >= lax.broadcasted_iota(jnp.int32, (H * C, C), 1)
    st_mask = (lax.broadcasted_iota(jnp.int32, (GLA_V, GLA_QK), 0) // V
               == lax.broadcasted_iota(jnp.int32, (GLA_V, GLA_QK), 1) // K)

    n_chunk = T // C
    chunks = [slice(c * C, (c + 1) * C) for c in range(n_chunk)]
    b_c = [b_all[rs] for rs in chunks]
    q_t = [q[rs] * jnp.exp(b) for rs, b in zip(chunks, b_c)]
    k_t = [k[rs] * jnp.exp(-b) for rs, b in zip(chunks, b_c)]
    k_dec = [k[rs] * jnp.exp(b[C - 1:C] - b) for rs, b in zip(chunks, b_c)]
    att = []
    for c in range(n_chunk):
        q_stack = jnp.concatenate([jnp.where(k_head == h, q_t[c], 0.0) for h in range(H)], axis=0)
        att.append(jnp.where(causal, _dot_nt(q_stack, k_t[c]), 0.0))
    kv_t = [jnp.where(st_mask, _dot(v[rs].astype(F32).T, k_dec[c]), 0.0)
            for c, rs in enumerate(chunks)]
    o_all = [_dot(att[c], v[rs]) for c, rs in enumerate(chunks)]
    st = st_ref[...]
    outs = []
    for c in range(n_chunk):
        o = _dot_nt(q_t[c], st)
        for h in range(H):
            o = o + jnp.where(v_head == h, o_all[c][h * C:(h + 1) * C], 0.0)
        st = st * jnp.exp(b_c[c][C - 1:C]) + kv_t[c]
        outs.append(o)
    st_ref[...] = st

    o = jnp.concatenate(outs, axis=0)
    ms = _dot_sel_r(o * o, ones_ref[...]) * (1.0 / V)
    o = o * lax.rsqrt(ms + EPS) * nw_ref[...]
    o_ref[...] = (o * _silu(g)).astype(o_ref.dtype)


def _gla(gla, glr, wa, ba, nw, layer):
    B, S, W = gla.shape
    T, C, TB = T_GLA, GLA_CHUNK, GLA_CUMSUM_ROWS
    idx = np.arange(TB)
    tri = ((idx[:, None] // C == idx[None, :] // C) & (idx[:, None] >= idx[None, :]))
    hv = np.arange(GLA_V) // GLA_DV
    ones = hv[:, None] == hv[None, :]
    return pl.pallas_call(
        _gla_kernel,
        grid=(B, S // T),
        in_specs=[pl.BlockSpec((None, T, W), lambda b, i: (b, i, 0)),
                  pl.BlockSpec((None, T, LANE), lambda b, i: (b, i, 0)),
                  _layer_spec((LANE, GLA_QK), layer), _layer_spec((1, GLA_QK), layer),
                  _layer_spec((1, GLA_V), layer), _const_spec((TB, TB)), _const_spec((GLA_V, GLA_V))],
        out_specs=pl.BlockSpec((None, T, GLA_V), lambda b, i: (b, i, 0)),
        out_shape=jax.ShapeDtypeStruct((B, S, GLA_V), BF16),
        scratch_shapes=[pltpu.VMEM((GLA_V, GLA_QK), F32)],
        compiler_params=_cparams(("parallel", "arbitrary"), 32),
        name="gla_mixer",
    )(gla, glr, wa, ba, nw, jnp.asarray(tri, BF16), jnp.asarray(ones, BF16))


def _dil_kernel(q_ref, kp_ref, k_ref, vp_ref, v_ref, bias_ref, o_ref, lse_ref):
    H, E, BLK = DIL_HEADS, DIL_DIM, DIL_STEPS
    n_res, n_blk = q_ref.shape[0], q_ref.shape[1] // BLK
    first = jnp.minimum(pl.program_id(2), 1)
    head = lax.broadcasted_iota(jnp.int32, (BLK, H * E), 1) // E
    lane = lax.broadcasted_iota(jnp.int32, (BLK, LANE), 1)
    kcat = [jnp.concatenate([kp_ref[r], k_ref[r]], axis=0) for r in range(n_res)]
    vcat = [jnp.concatenate([vp_ref[r], v_ref[r]], axis=0) for r in range(n_res)]
    blocks = [(r, i) for r in range(n_res) for i in range(n_blk)]

    def scores(r, i):
        q = q_ref[r, i * BLK:(i + 1) * BLK, :] * jnp.asarray(E ** -0.5, BF16)
        q_stack = jnp.concatenate([jnp.where(head == h, q, jnp.zeros_like(q)) for h in range(H)], axis=0)
        bias = bias_ref[first] if i == 0 else bias_ref[1]
        return _dot_nt(q_stack, kcat[r][i * BLK:(i + 2) * BLK]) + bias

    pending = [scores(*blk) for blk in blocks[:DIL_LOOKAHEAD]]
    for n, (r, i) in enumerate(blocks):
        s = pending.pop(0)
        if n + DIL_LOOKAHEAD < len(blocks):
            pending.append(scores(*blocks[n + DIL_LOOKAHEAD]))
        m = jnp.max(s, axis=-1, keepdims=True)
        p = jnp.exp(s - m)
        l = jnp.sum(p, axis=-1, keepdims=True)
        o_all = _dot(p, vcat[r][i * BLK:(i + 2) * BLK])
        lse = m + jnp.log(l)
        inv_l = 1.0 / l
        o = o_all[0:BLK] * inv_l[0:BLK]
        lse_out = jnp.zeros((BLK, LANE), F32)
        for h in range(H):
            hs = slice(h * BLK, (h + 1) * BLK)
            if h > 0:
                o = jnp.where(head == h, o_all[hs] * inv_l[hs], o)
            lse_out = jnp.where(lane == h, lse[hs], lse_out)
        rows = slice(i * BLK, (i + 1) * BLK)
        o_ref[r, rows, :] = o.astype(o_ref.dtype)
        lse_ref[r, rows, :] = lse_out


def _dil_bias(window, dil, slopes):
    BLK = DIL_STEPS
    steps = np.arange(BLK)[:, None] + BLK - np.arange(2 * BLK)[None, :]
    ok = (steps >= 0) & (steps <= window // dil)
    bias = -slopes[:, None, None] * (steps * dil)[None].astype(np.float32)
    full = np.where(ok[None], bias, NEG).astype(np.float32)
    first = np.where((np.arange(2 * BLK) >= BLK)[None, None, :], full, NEG)
    return np.stack([first.reshape(-1, 2 * BLK), full.reshape(-1, 2 * BLK)]).astype(np.float32)


def _dil_pattern(qkv, window, dil, slopes):
    B, _, n, _ = qkv.shape
    HE, BLK = DIL_OUT, DIL_STEPS
    n_blk = min(DIL_QBLOCKS, n // BLK)
    n_res = min(dil, DIL_QBLOCKS // n_blk)
    rows = n_blk * BLK
    assert n % rows == 0 and dil % n_res == 0
    cur = lambda t: pl.BlockSpec((None, n_res, rows, HE), lambda b, r, j: (b, r, j, t))
    prev = lambda t: pl.BlockSpec((None, n_res, BLK, HE),
                                  lambda b, r, j: (b, r, jnp.maximum(j * n_blk - 1, 0), t))
    return pl.pallas_call(
        _dil_kernel,
        grid=(B, dil // n_res, n // rows),
        in_specs=[cur(0), prev(1), cur(1), prev(2), cur(2),
                  _const_spec((2, DIL_HEADS * BLK, 2 * BLK))],
        out_specs=[pl.BlockSpec((None, n_res, rows, HE), lambda b, r, j: (b, r, j, 0)),
                   pl.BlockSpec((None, n_res, rows, LANE), lambda b, r, j: (b, r, j, 0))],
        out_shape=[jax.ShapeDtypeStruct((B, dil, n, HE), BF16),
                   jax.ShapeDtypeStruct((B, dil, n, LANE), F32)],
        compiler_params=_cparams(("parallel", "parallel", "arbitrary"), 32),
        name=f"dil_attn_d{dil}",
    )(qkv, qkv, qkv, qkv, qkv, jnp.asarray(_dil_bias(window, dil, slopes)))


def _dilated(qkvs):
    n_pat = len(DIL_PATTERNS)
    n_h = n_pat * DIL_HEADS
    slopes = (2.0 ** (-8.0 * np.arange(1, n_h + 1) / n_h)).astype(np.float32).reshape(n_pat, DIL_HEADS)
    outs = []
    for p, (w, d) in enumerate(DIL_PATTERNS):
        qkv = qkvs[p] if qkvs[p].ndim == 4 else qkvs[p][:, None]
        assert qkv.shape[1] == d
        outs.append(_dil_pattern(qkv, w, d, slopes[p]))
    return [o for o, _ in outs], [l for _, l in outs]


def _gm_kernel(gm_ref, lnw_ref, lnb_ref, w_ref, bs_ref, o_ref):
    G, E, C = GM_GROUPS, GM_DIM, GM_CHUNK
    T = gm_ref.shape[0]
    blk = gm_ref[...].astype(F32)
    u = _gelu(blk[:, :GM_W])
    v = _gelu(blk[:, GM_W:])
    mu = jnp.mean(v, axis=-1, keepdims=True)
    var = jnp.mean(jnp.square(v - mu), axis=-1, keepdims=True)
    v = (v - mu) * lax.rsqrt(var + EPS) * lnw_ref[...] + lnb_ref[...]
    group = lax.broadcasted_iota(jnp.int32, (C, GM_W), 1) // E
    w = w_ref[...]
    bs = bs_ref[...]
    for c in range(T // C):
        rs = slice(c * C, (c + 1) * C)
        r = _dot(w, v[rs])
        sv = bs
        for gi in range(G):
            sv = sv + jnp.where(group == gi, r[gi * C:(gi + 1) * C], 0.0)
        o_ref[rs, :] = (u[rs] * sv).astype(o_ref.dtype)


def _gmlp_params(ws, bs):
    C = GM_CHUNK
    tril = jnp.tril(jnp.ones((C, C), bool))
    w_stack = jnp.where(tril, ws, 0.0).reshape(-1, GM_GROUPS * C, C).astype(BF16)
    bs_exp = jnp.repeat(jnp.swapaxes(bs, 1, 2), GM_DIM, axis=2)
    return w_stack, bs_exp


def _gmlp(gm, lnw, lnb, w_stack, bs_exp, layer):
    B, S, W = gm.shape
    C = GM_CHUNK
    return pl.pallas_call(
        _gm_kernel,
        grid=(B, S // T_GM),
        in_specs=[pl.BlockSpec((None, T_GM, W), lambda b, i: (b, i, 0)),
                  _layer_spec((1, GM_W), layer), _layer_spec((1, GM_W), layer),
                  _layer_spec((GM_GROUPS * C, C), layer), _layer_spec((C, GM_W), layer)],
        out_specs=pl.BlockSpec((None, T_GM, GM_W), lambda b, i: (b, i, 0)),
        out_shape=jax.ShapeDtypeStruct((B, S, GM_W), BF16),
        compiler_params=_cparams(("parallel", "parallel"), 32),
        name="gmlp_mixer",
    )(gm, lnw, lnb, w_stack, bs_exp)


def _ssd_kernel(z_ref, xbc_ref, dt_ref, dtb_ref, alog_ref, dskip_ref, nw_ref, tri_ref, exp_ref,
                o_ref, st_ref):
    H, P, G, N, L = SSD_HEADS, SSD_P, SSD_GROUPS, SSD_N, SSD_CHUNK
    R = H // G
    GW = R * P
    T = z_ref.shape[0]

    @pl.when(pl.program_id(1) == 0)
    def _():
        st_ref[...] = jnp.zeros_like(st_ref)

    dt_all = _softplus(dt_ref[...] + dtb_ref[...])
    da_all = dt_all * (-jnp.exp(alog_ref[...]))
    tri = tri_ref[...]
    expand = exp_ref[...]
    lower = lax.broadcasted_iota(jnp.int32, (L, L), 0) >= lax.broadcasted_iota(jnp.int32, (L, L), 1)
    head_in_group = lax.broadcasted_iota(jnp.int32, (L, GW), 1) // P

    chunks = [slice(c * L, (c + 1) * L) for c in range(T // L)]
    groups = [slice(gi * GW, (gi + 1) * GW) for gi in range(G)]
    b_cols = [slice(SSD_INNER + gi * N, SSD_INNER + (gi + 1) * N) for gi in range(G)]
    c_cols = [slice(SSD_INNER + SSD_BC + gi * N, SSD_INNER + SSD_BC + (gi + 1) * N) for gi in range(G)]
    xs = [xbc_ref[rs, :SSD_INNER].astype(F32) for rs in chunks]
    cs = [_dot_sel_l(tri, da_all[rs]) for rs in chunks]
    dt_e = [_dot(dt_all[rs], expand) for rs in chunks]
    dec_e = [_dot(jnp.exp(cs_c[L - 1:L] - cs_c) * dt_all[rs], expand) for rs, cs_c in zip(chunks, cs)]
    ecs_e = [_dot_sel_r(jnp.exp(cs_c), expand) for cs_c in cs]
    x_dt = [(x * e).astype(BF16) for x, e in zip(xs, dt_e)]
    x_dec = [(x * e).astype(BF16) for x, e in zip(xs, dec_e)]
    cb = [[_dot_nt(xbc_ref[rs, c_cols[gi]], xbc_ref[rs, b_cols[gi]]) for gi in range(G)]
          for rs in chunks]
    bx = [[_dot(xbc_ref[rs, b_cols[gi]].astype(F32).T, x_dec[c][:, groups[gi]]) for gi in range(G)]
          for c, rs in enumerate(chunks)]
    y_intra = []
    for c in range(len(chunks)):
        cs_t = cs[c].T
        per_group = []
        for gi in range(G):
            acc = None
            for r in range(R):
                h = gi * R + r
                seg = cs[c][:, h:h + 1] - cs_t[h:h + 1, :]
                wgt = cb[c][gi] * jnp.exp(jnp.where(lower, seg, NEG))
                y_h = _dot(wgt, x_dt[c][:, groups[gi]])
                acc = y_h if r == 0 else jnp.where(head_in_group == r, y_h, acc)
            per_group.append(acc)
        y_intra.append(per_group)

    for c, rs in enumerate(chunks):
        ys = []
        for gi, gs in enumerate(groups):
            y_state = _dot(xbc_ref[rs, c_cols[gi]], st_ref[:, gs]) * ecs_e[c][:, gs]
            st_ref[:, gs] = st_ref[:, gs] * ecs_e[c][L - 1:L, gs] + bx[c][gi]
            ys.append(y_state + y_intra[c][gi])
        y = jnp.concatenate(ys, axis=1) + xs[c] * dskip_ref[...]
        y = y * _silu(z_ref[rs, :].astype(F32))
        normed = []
        for gs in groups:
            y_g = y[:, gs]
            normed.append(y_g * lax.rsqrt(jnp.mean(y_g * y_g, axis=-1, keepdims=True) + EPS))
        o_ref[rs, :] = (jnp.concatenate(normed, axis=1) * nw_ref[...]).astype(o_ref.dtype)


def _ssd_params(dt_bias, a_log, d_skip, norm_w):
    pad = lambda t: jnp.pad(t, ((0, 0), (0, LANE - t.shape[1])))[:, None, :]
    return (pad(dt_bias), pad(a_log), jnp.repeat(d_skip, SSD_P, axis=1)[:, None, :], norm_w[:, None, :])


def _ssd(z, xbc, dt, params, layer):
    B, S, _ = z.shape
    L = SSD_CHUNK
    tri = np.tril(np.ones((L, L), np.float32))
    expand = np.zeros((LANE, SSD_INNER), np.float32)
    for h in range(SSD_HEADS):
        expand[h, h * SSD_P:(h + 1) * SSD_P] = 1.0
    return pl.pallas_call(
        _ssd_kernel,
        grid=(B, S // T_SSD),
        in_specs=[pl.BlockSpec((None, T_SSD, SSD_INNER), lambda b, i: (b, i, 0)),
                  pl.BlockSpec((None, T_SSD, SSD_XBC), lambda b, i: (b, i, 0)),
                  pl.BlockSpec((None, T_SSD, LANE), lambda b, i: (b, i, 0)),
                  _layer_spec((1, LANE), layer), _layer_spec((1, LANE), layer),
                  _layer_spec((1, SSD_INNER), layer), _layer_spec((1, SSD_INNER), layer),
                  _const_spec((L, L)), _const_spec((LANE, SSD_INNER))],
        out_specs=pl.BlockSpec((None, T_SSD, SSD_INNER), lambda b, i: (b, i, 0)),
        out_shape=jax.ShapeDtypeStruct((B, S, SSD_INNER), BF16),
        scratch_shapes=[pltpu.VMEM((SSD_N, SSD_INNER), F32)],
        compiler_params=_cparams(("parallel", "arbitrary"), 48),
        name="ssd_mixer",
    )(z, xbc, dt, *params, jnp.asarray(tri, BF16), jnp.asarray(expand, BF16))


def _token_order(src_ref, stage_ref):
    dil, per, w = src_ref.shape
    for r in range(dil):
        plane = src_ref[r].astype(F32)
        for t in range(w // LANE):
            stage_ref[t, pl.ds(r, per, stride=dil), :] = plane[:, t * LANE:(t + 1) * LANE]
    return jnp.concatenate([stage_ref[t] for t in range(w // LANE)], axis=1)


def _tail_kernel(x_ref, g1_ref, gla_ref, d0_ref, d1_ref, d2_ref, l0_ref, l1_ref, l2_ref, gm_ref, ssd_ref,
                 gates_ref, wg_ref, wd_ref, wm_ref, ws_ref, wo_ref, exp_ref,
                 nw_ref, sc_ref, sh_ref, g2_ref, w1_ref, w2_ref, fw_ref, o_ref,
                 od1_ref, od2_ref, ol1_ref, ol2_ref, *, final_norm):
    D = D_MODEL
    d0, l0 = d0_ref[...].astype(F32), l0_ref[...]
    d1, l1 = _token_order(d1_ref, od1_ref), _token_order(l1_ref, ol1_ref)
    d2, l2 = _token_order(d2_ref, od2_ref), _token_order(l2_ref, ol2_ref)
    m = jnp.maximum(jnp.maximum(l0, l1), l2)
    e0, e1, e2 = jnp.exp(l0 - m), jnp.exp(l1 - m), jnp.exp(l2 - m)
    den = e0 + e1 + e2
    expand = exp_ref[...]
    o_dil = (_dot_sel_r(e0 / den, expand) * d0 + _dot_sel_r(e1 / den, expand) * d1
             + _dot_sel_r(e2 / den, expand) * d2)
    branches = ((gla_ref[...], wg_ref), (o_dil, wd_ref), (gm_ref[...], wm_ref), (ssd_ref[...], ws_ref))
    merged = None
    for i, (o_b, w_ref) in enumerate(branches):
        g = gates_ref[:, i * D:(i + 1) * D]
        gate = (jnp.tanh(g * jnp.asarray(0.5, g.dtype)) * jnp.asarray(0.5, g.dtype)
                + jnp.asarray(0.5, g.dtype)).astype(F32)
        term = gate * _dot(o_b, w_ref[...])
        merged = term if merged is None else merged + term
    x = x_ref[...] + g1_ref[...] * _dot(merged, wo_ref[...])

    h = _modulated_norm(x, nw_ref[...], sc_ref[...], sh_ref[...]).astype(BF16)
    acc = jnp.zeros(x.shape, F32)
    for c0 in range(0, MLP_HIDDEN, D_MODEL):
        a = jnp.maximum(jnp.dot(h, w1_ref[:, c0:c0 + D_MODEL], preferred_element_type=F32), 0.0)
        acc = acc + _dot(a * a, w2_ref[c0:c0 + D_MODEL, :])
    y = x + g2_ref[...] * acc
    if final_norm:
        y = y * lax.rsqrt(jnp.mean(y * y, axis=-1, keepdims=True) + EPS) * fw_ref[...]
    o_ref[...] = y


def _mod_spec(layer, which):
    return pl.BlockSpec((None, None, None, 1, D_MODEL), lambda b, i: (layer, b, which, 0, 0))


def _tail(x, mod, o_gla, o_dil, lse_dil, o_gm, o_ssd, gates, w_gla, w_dil, w_gm, w_ssd, w_out,
          nw2, w1, w2, fw, layer, final_norm):
    B, S, D = x.shape
    TM = TM_MLP
    tok = lambda w: pl.BlockSpec((None, TM, w), lambda b, i: (b, i, 0))

    def plane(w, dil):
        if dil == 1:
            return pl.BlockSpec((None, None, TM, w), lambda b, i: (b, 0, i, 0))
        return pl.BlockSpec((None, dil, TM // dil, w), lambda b, i: (b, 0, i, 0))

    dils = [d for _, d in DIL_PATTERNS]
    assert dils[0] == 1 and len(dils) == 3
    expand = np.zeros((LANE, DIL_OUT), np.float32)
    for h in range(DIL_HEADS):
        expand[h, h * DIL_DIM:(h + 1) * DIL_DIM] = 1.0
    return pl.pallas_call(
        functools.partial(_tail_kernel, final_norm=final_norm),
        grid=(B, S // TM),
        in_specs=[tok(D), _mod_spec(layer, 2), tok(GLA_V)]
                 + [plane(DIL_OUT, d) for d in dils] + [plane(LANE, d) for d in dils]
                 + [tok(GM_W), tok(SSD_INNER), tok(N_BRANCH * D),
                    _layer_spec((GLA_V, D), layer), _layer_spec((DIL_OUT, D), layer),
                    _layer_spec((GM_W, D), layer), _layer_spec((SSD_INNER, D), layer),
                    _layer_spec((D, D), layer), _const_spec((LANE, DIL_OUT)),
                    _layer_spec((1, D), layer), _mod_spec(layer, 4), _mod_spec(layer, 3),
                    _mod_spec(layer, 5), _layer_spec((D, MLP_HIDDEN), layer),
                    _layer_spec((MLP_HIDDEN, D), layer), _const_spec((1, D))],
        out_specs=tok(D),
        out_shape=jax.ShapeDtypeStruct((B, S, D), F32),
        scratch_shapes=[pltpu.VMEM((DIL_OUT // LANE, TM, LANE), F32), pltpu.VMEM((DIL_OUT // LANE, TM, LANE), F32),
                        pltpu.VMEM((1, TM, LANE), F32), pltpu.VMEM((1, TM, LANE), F32)],
        compiler_params=_cparams(("parallel", "parallel"), 58),
        name="tail_final" if final_norm else "tail",
    )(x, mod, o_gla, *o_dil, *lse_dil, o_gm, o_ssd, gates, w_gla, w_dil, w_gm, w_ssd, w_out,
      jnp.asarray(expand, BF16), nw2, mod, mod, mod, w1, w2, fw)


def _repack_plan():
    whole = lambda a, b: [(_SP[a], _SP[b] - _SP[a], _SP[b] - _SP[a])]
    plan = {"gla": whole(0, 4), "glr": [(_SP[4], LANE, GLA_RANK)], "gm": whole(8, 10),
            "ssd_z": whole(10, 11), "ssd_xbc": whole(11, 12), "dt": [(_SP[12], LANE, SSD_HEADS)],
            "gates": whole(13, 14)}
    for p in range(len(DIL_PATTERNS)):
        plan[f"dil{p}"] = [(_SP[t] + p * DIL_OUT, DIL_OUT, DIL_OUT) for t in (5, 6, 7)]
    plan = [plan[name] for name in PROJ_NAMES]
    assert [sum(w for _, w, _ in pieces) for pieces in plan] == [o[1] for o in PROJ_OUTS]
    return plan


def _repack_kernel(wt_ref, o_ref, *, valid):
    v = wt_ref[0]
    if valid < v.shape[0]:
        v = jnp.where(lax.broadcasted_iota(jnp.int32, v.shape, 0) < valid, v, 0.0)
    o_ref[...] = v.T.astype(o_ref.dtype)


def _pack_w_in(w_in):
    L, D, W = w_in.shape
    w_t = jnp.swapaxes(w_in, 1, 2)
    outs = []
    for (name, width, _, _), pieces in zip(PROJ_OUTS, _repack_plan()):
        start, piece_w, valid = pieces[0]
        if len(pieces) > 1:
            tile, stride = piece_w, pieces[1][0] - start
            assert all(p == (start + i * stride, tile, tile) for i, p in enumerate(pieces))
        else:
            tile = next(t for t in (512, 256, LANE) if piece_w % t == 0)
            stride = tile
        assert start % 8 == 0 and stride % 8 == 0 and width % tile == 0
        outs.append(pl.pallas_call(
            functools.partial(_repack_kernel, valid=min(valid, tile)),
            grid=(L, width // tile),
            in_specs=[pl.BlockSpec((pl.Element(1), pl.Element(tile), pl.Element(D)),
                                   lambda l, t, start=start, stride=stride: (
                                       l, pl.multiple_of(start + t * stride, 8), 0))],
            out_specs=pl.BlockSpec((None, D, tile), lambda l, t: (l, 0, t)),
            out_shape=jax.ShapeDtypeStruct((L, D, width), BF16),
            compiler_params=_cparams(("parallel", "parallel"), 32),
            name=f"repack_{name}",
        )(w_t))
    return outs


def kernel(x, c, w_ada, b_ada, norm1_w, norm2_w, w_in, gla_w_a2, gla_b_a, gla_norm_w, gm_ln_w, gm_ln_b, gm_ws, gm_bs, ssd_conv_w, ssd_conv_b, ssd_dt_bias, ssd_a_log, ssd_d, ssd_norm_w, w_br_gla, w_br_dil, w_br_gm, w_br_ssd, w_out, w_mlp1, w_mlp2, final_norm_w):
    mod = _ada_mod(c, w_ada, b_ada)
    w_parts = _pack_w_in(w_in)
    wa = jnp.pad(gla_w_a2, ((0, 0), (0, LANE - GLA_RANK), (0, 0))).astype(BF16)
    w_gla, w_dil, w_gm, w_ssd, w_o, w1, w2 = (t.astype(BF16) for t in (
        w_br_gla, w_br_dil, w_br_gm, w_br_ssd, w_out, w_mlp1, w_mlp2))
    rows = lambda t: t[:, None, :]
    n1, n2 = rows(norm1_w), rows(norm2_w)
    gla_ba, gla_nw = rows(gla_b_a), rows(jnp.tile(gla_norm_w, (1, GLA_HEADS)))
    gm_lnw, gm_lnb = rows(gm_ln_w), rows(gm_ln_b)
    gm_w, gm_b = _gmlp_params(gm_ws, gm_bs)
    ssd_params = _ssd_params(ssd_dt_bias, ssd_a_log, ssd_d, ssd_norm_w)
    conv_b = rows(ssd_conv_b)
    fw = final_norm_w.reshape(1, -1)
    for l in range(DEPTH):
        p = dict(zip(PROJ_NAMES, _inproj(x, n1, mod, ssd_conv_w, conv_b, w_parts, l)))
        o_gla = _gla(p["gla"], p["glr"], wa, gla_ba, gla_nw, l)
        o_dil, lse_dil = _dilated([p["dil0"], p["dil1"], p["dil2"]])
        o_gm = _gmlp(p["gm"], gm_lnw, gm_lnb, gm_w, gm_b, l)
        o_ssd = _ssd(p["ssd_z"], p["ssd_xbc"], p["dt"], ssd_params, l)
        x = _tail(x, mod, o_gla, o_dil, lse_dil, o_gm, o_ssd, p["gates"], w_gla, w_dil, w_gm, w_ssd, w_o,
                  n2, w1, w2, fw, l, final_norm=(l == DEPTH - 1))
    return x
```

```python
import functools

import numpy as np
import jax
import jax.numpy as jnp
from jax import lax
from jax.experimental import pallas as pl
from jax.experimental.pallas import tpu as pltpu

F32 = jnp.float32
BF16 = jnp.bfloat16

D_MODEL = 1024
DEPTH = 4
EPS = 1e-6

GLA_HEADS, GLA_DK, GLA_DV, GLA_RANK, GLA_CHUNK = 4, 32, 64, 16, 64
GLA_GATE_NORM = 16.0
DIL_PATTERNS = ((128, 1), (512, 4), (2048, 16))
DIL_HEADS, DIL_DIM, DIL_STEPS = 4, 64, 128
GM_GROUPS, GM_DIM, GM_CHUNK = 4, 64, 128
SSD_HEADS, SSD_P, SSD_GROUPS, SSD_N, SSD_CONV, SSD_CHUNK = 8, 64, 2, 128, 4, 128
N_BRANCH = 4
MLP_HIDDEN = 4 * D_MODEL

GLA_QK = GLA_HEADS * GLA_DK
GLA_V = GLA_HEADS * GLA_DV
DIL_W = len(DIL_PATTERNS) * DIL_HEADS * DIL_DIM
DIL_OUT = DIL_HEADS * DIL_DIM
GM_W = GM_GROUPS * GM_DIM
SSD_INNER = SSD_HEADS * SSD_P
SSD_BC = SSD_GROUPS * SSD_N
SSD_XBC = SSD_INNER + 2 * SSD_BC
IN_SPLITS = (GLA_QK, GLA_QK, GLA_V, GLA_V, GLA_RANK, DIL_W, DIL_W, DIL_W, GM_W, GM_W,
             SSD_INNER, SSD_XBC, SSD_HEADS, N_BRANCH * D_MODEL)
_SP = tuple(int(v) for v in np.cumsum((0,) + IN_SPLITS))

LANE = 128
NEG = -1e30

PROJ_OUTS = (("gla", 2 * GLA_QK + 2 * GLA_V, BF16, 1), ("glr", LANE, F32, 1),
             ("dil0", 3 * DIL_OUT, BF16, DIL_PATTERNS[0][1]), ("dil1", 3 * DIL_OUT, BF16, DIL_PATTERNS[1][1]),
             ("dil2", 3 * DIL_OUT, BF16, DIL_PATTERNS[2][1]),
             ("gm", 2 * GM_W, BF16, 1), ("ssd_z", SSD_INNER, BF16, 1), ("ssd_xbc", SSD_XBC, BF16, 1),
             ("dt", LANE, F32, 1), ("gates", N_BRANCH * D_MODEL, BF16, 1))
PROJ_NAMES = tuple(o[0] for o in PROJ_OUTS)

TM_PROJ = 512
TM_MLP = 512
T_GLA = 512
GLA_CUMSUM_ROWS = 256
T_GM = 1024
T_SSD = 512
COL_CHUNK = 512
DIL_QBLOCKS = 8
DIL_LOOKAHEAD = 2


V7X_VMEM_BYTES = 64 * 1024 * 1024


def _cparams(sem, vmem_mib):
    assert vmem_mib * 1024 * 1024 <= V7X_VMEM_BYTES
    return pltpu.CompilerParams(dimension_semantics=sem, vmem_limit_bytes=V7X_VMEM_BYTES)


def _const_spec(shape):
    nd = len(shape)
    return pl.BlockSpec(shape, lambda *_: (0,) * nd, pipeline_mode=pl.Buffered(1))


def _layer_spec(shape, layer):
    nd = len(shape)
    return pl.BlockSpec((None,) + tuple(shape), lambda *_: (layer,) + (0,) * nd,
                        pipeline_mode=pl.Buffered(1))


def _dot(a, b):
    return jnp.dot(a.astype(BF16), b.astype(BF16), preferred_element_type=F32)


def _dot_nt(a, b):
    return lax.dot_general(a.astype(BF16), b.astype(BF16), (((1,), (1,)), ((), ())),
                           preferred_element_type=F32)


def _split(a):
    hi = a.astype(BF16)
    lo = (a - hi.astype(F32)).astype(BF16)
    return hi, lo


def _dot_sel_r(a, sel):
    hi, lo = _split(a)
    return (jnp.dot(hi, sel, preferred_element_type=F32) + jnp.dot(lo, sel, preferred_element_type=F32))


def _dot_sel_l(sel, a):
    hi, lo = _split(a)
    return (jnp.dot(sel, hi, preferred_element_type=F32) + jnp.dot(sel, lo, preferred_element_type=F32))


def _sigmoid(x):
    return 1.0 / (1.0 + jnp.exp(-x))


def _silu(x):
    return x * _sigmoid(x)


def _gelu(x):
    return 0.5 * x * (1.0 + lax.erf(x * (2.0 ** -0.5)))


def _softplus(x):
    return jnp.maximum(x, 0.0) + jnp.log1p(jnp.exp(-jnp.abs(x)))


def _modulated_norm(x, nw, sc, sh):
    y = x * lax.rsqrt(jnp.mean(x * x, axis=-1, keepdims=True) + EPS)
    return (y * nw) * (1.0 + sc) + sh


def _ada_kernel(c_ref, w_ref, b_ref, o_ref):
    o_ref[...] = _dot(_silu(c_ref[...]), w_ref[...]) + b_ref[...]


def _ada_mod(c, w_ada, b_ada):
    B, D = c.shape
    rows = 16
    c_pad = jnp.pad(c, ((0, rows - B), (0, 0)))
    n_col = w_ada.shape[-1] // D
    out = pl.pallas_call(
        _ada_kernel,
        grid=(DEPTH, n_col),
        in_specs=[pl.BlockSpec((rows, D), lambda l, j: (0, 0)),
                  pl.BlockSpec((None, D, D), lambda l, j: (l, 0, j)),
                  pl.BlockSpec((None, 1, D), lambda l, j: (l, 0, j))],
        out_specs=pl.BlockSpec((None, rows, D), lambda l, j: (l, 0, j)),
        out_shape=jax.ShapeDtypeStruct((DEPTH, rows, n_col * D), F32),
        compiler_params=_cparams(("arbitrary", "arbitrary"), 32),
        name="ada_mod",
    )(c_pad, w_ada, b_ada.reshape(DEPTH, 1, -1))
    return out[:, :B].reshape(DEPTH, B, n_col, 1, D)


def _inproj_kernel(x_ref, nw_ref, sc_ref, sh_ref, cw_ref, cb_ref, *refs):
    n_out = len(PROJ_OUTS)
    w_refs, o_refs, (stage_ref, carry_ref, win_ref) = refs[:n_out], refs[n_out:2 * n_out], refs[2 * n_out:]
    TM = x_ref.shape[0]
    KEEP = carry_ref.shape[0]

    @pl.when(pl.program_id(1) == 0)
    def _():
        carry_ref[...] = jnp.zeros_like(carry_ref)

    h = _modulated_norm(x_ref[...], nw_ref[...], sc_ref[...], sh_ref[...]).astype(BF16)
    for w_ref, o_ref, (name, width, dtype, dil) in zip(w_refs, o_refs, PROJ_OUTS):
        for c0 in range(0, width, COL_CHUNK):
            c1 = min(c0 + COL_CHUNK, width)
            res = jnp.dot(h, w_ref[:, c0:c1], preferred_element_type=F32)
            if name == "ssd_xbc":
                win_ref[0:KEEP, :] = carry_ref[:, c0:c1]
                win_ref[KEEP:, :] = res
                carry_ref[:, c0:c1] = res[TM - KEEP:]
                conv = cb_ref[:, c0:c1]
                for j in range(SSD_CONV):
                    off = KEEP - (SSD_CONV - 1) + j
                    conv = conv + cw_ref[j:j + 1, c0:c1] * win_ref[off:off + TM, :]
                o_ref[:, c0:c1] = _silu(conv).astype(dtype)
            elif dil == 1:
                o_ref[:, c0:c1] = res.astype(dtype)
            else:
                for t in range((c1 - c0) // LANE):
                    stage_ref[c0 // LANE + t] = res[:, t * LANE:(t + 1) * LANE]
        if dil > 1:
            for r in range(dil):
                for t in range(width // LANE):
                    o_ref[r, :, t * LANE:(t + 1) * LANE] = stage_ref[
                        t, pl.ds(r, TM // dil, stride=dil), :].astype(dtype)


def _inproj(x, nw, mod, conv_w, conv_b, w_parts, layer):
    B, S, D = x.shape
    TM = TM_PROJ
    out_specs, out_shape = [], []
    for _, w, dt, dil in PROJ_OUTS:
        if dil == 1:
            out_specs.append(pl.BlockSpec((None, TM, w), lambda b, i: (b, i, 0)))
            out_shape.append(jax.ShapeDtypeStruct((B, S, w), dt))
        else:
            out_specs.append(pl.BlockSpec((None, dil, TM // dil, w), lambda b, i: (b, 0, i, 0)))
            out_shape.append(jax.ShapeDtypeStruct((B, dil, S // dil, w), dt))
    return pl.pallas_call(
        _inproj_kernel,
        grid=(B, S // TM),
        in_specs=[pl.BlockSpec((None, TM, D), lambda b, i: (b, i, 0)),
                  _layer_spec((1, D), layer), _mod_spec(layer, 1), _mod_spec(layer, 0),
                  _layer_spec((SSD_CONV, SSD_XBC), layer), _layer_spec((1, SSD_XBC), layer)]
                 + [_layer_spec((D, o[1]), layer) for o in PROJ_OUTS],
        out_specs=out_specs,
        out_shape=out_shape,
        scratch_shapes=[pltpu.VMEM((3 * DIL_OUT // LANE, TM, LANE), F32),
                        pltpu.VMEM((8, SSD_XBC), F32),
                        pltpu.VMEM((TM + 8, COL_CHUNK), F32)],
        compiler_params=_cparams(("parallel", "arbitrary"), 58),
        name="inproj",
    )(x, nw, mod, mod, conv_w, conv_b, *w_parts)


def _gla_kernel(gla_ref, glr_ref, wa_ref, ba_ref, nw_ref, tri_ref, ones_ref, o_ref, st_ref):
    C, H, K, V = GLA_CHUNK, GLA_HEADS, GLA_DK, GLA_DV
    T = gla_ref.shape[0]

    @pl.when(pl.program_id(1) == 0)
    def _():
        st_ref[...] = jnp.zeros_like(st_ref)

    blk = gla_ref[...]
    q = blk[:, 0:GLA_QK].astype(F32) * (K ** -0.5)
    k = blk[:, GLA_QK:2 * GLA_QK].astype(F32)
    v = blk[:, 2 * GLA_QK:2 * GLA_QK + GLA_V]
    g = blk[:, 2 * GLA_QK + GLA_V:].astype(F32)
    pre = _dot(glr_ref[...], wa_ref[...]) + ba_ref[...]
    log_a = -_softplus(-pre) * (1.0 / GLA_GATE_NORM)
    tri = tri_ref[...]
    TB = tri.shape[0]
    b_all = jnp.concatenate([_dot_sel_l(tri, log_a[i * TB:(i + 1) * TB]) for i in range(T // TB)],
                            axis=0)

    k_head = lax.broadcasted_iota(jnp.int32, (C, GLA_QK), 1) // K
    v_head = lax.broadcasted_iota(jnp.int32, (C, GLA_V), 1) // V
    row = lax.broadcasted_iota(jnp.int32, (H * C, C), 0)
    causal = (row % C) >= lax.broadcasted_iota(jnp.int32, (H * C, C), 1)
    st_mask = (lax.broadcasted_iota(jnp.int32, (GLA_V, GLA_QK), 0) // V
               == lax.broadcasted_iota(jnp.int32, (GLA_V, GLA_QK), 1) // K)

    n_chunk = T // C
    chunks = [slice(c * C, (c + 1) * C) for c in range(n_chunk)]
    b_c = [b_all[rs] for rs in chunks]
    q_t = [q[rs] * jnp.exp(b) for rs, b in zip(chunks, b_c)]
    k_t = [k[rs] * jnp.exp(-b) for rs, b in zip(chunks, b_c)]
    k_dec = [k[rs] * jnp.exp(b[C - 1:C] - b) for rs, b in zip(chunks, b_c)]
    att = []
    for c in range(n_chunk):
        q_stack = jnp.concatenate([jnp.where(k_head == h, q_t[c], 0.0) for h in range(H)], axis=0)
        att.append(jnp.where(causal, _dot_nt(q_stack, k_t[c]), 0.0))
    kv_t = [jnp.where(st_mask, _dot(v[rs].astype(F32).T, k_dec[c]), 0.0)
            for c, rs in enumerate(chunks)]
    o_all = [_dot(att[c], v[rs]) for c, rs in enumerate(chunks)]
    st = st_ref[...]
    outs = []
    for c in range(n_chunk):
        o = _dot_nt(q_t[c], st)
        for h in range(H):
            o = o + jnp.where(v_head == h, o_all[c][h * C:(h + 1) * C], 0.0)
        st = st * jnp.exp(b_c[c][C - 1:C]) + kv_t[c]
        outs.append(o)
    st_ref[...] = st

    o = jnp.concatenate(outs, axis=0)
    ms = _dot_sel_r(o * o, ones_ref[...]) * (1.0 / V)
    o = o * lax.rsqrt(ms + EPS) * nw_ref[...]
    o_ref[...] = (o * _silu(g)).astype(o_ref.dtype)


def _gla(gla, glr, wa, ba, nw, layer):
    B, S, W = gla.shape
    T, C, TB = T_GLA, GLA_CHUNK, GLA_CUMSUM_ROWS
    idx = np.arange(TB)
    tri = ((idx[:, None] // C == idx[None, :] // C) & (idx[:, None] >= idx[None, :]))
    hv = np.arange(GLA_V) // GLA_DV
    ones = hv[:, None] == hv[None, :]
    return pl.pallas_call(
        _gla_kernel,
        grid=(B, S // T),
        in_specs=[pl.BlockSpec((None, T, W), lambda b, i: (b, i, 0)),
                  pl.BlockSpec((None, T, LANE), lambda b, i: (b, i, 0)),
                  _layer_spec((LANE, GLA_QK), layer), _layer_spec((1, GLA_QK), layer),
                  _layer_spec((1, GLA_V), layer), _const_spec((TB, TB)), _const_spec((GLA_V, GLA_V))],
        out_specs=pl.BlockSpec((None, T, GLA_V), lambda b, i: (b, i, 0)),
        out_shape=jax.ShapeDtypeStruct((B, S, GLA_V), BF16),
        scratch_shapes=[pltpu.VMEM((GLA_V, GLA_QK), F32)],
        compiler_params=_cparams(("parallel", "arbitrary"), 32),
        name="gla_mixer",
    )(gla, glr, wa, ba, nw, jnp.asarray(tri, BF16), jnp.asarray(ones, BF16))


def _dil_kernel(q_ref, kp_ref, k_ref, vp_ref, v_ref, bias_ref, o_ref, stat_ref):
    H, E, BLK = DIL_HEADS, DIL_DIM, DIL_STEPS
    n_res, n_blk = q_ref.shape[0], q_ref.shape[1] // BLK
    first = jnp.minimum(pl.program_id(2), 1)
    head = lax.broadcasted_iota(jnp.int32, (BLK, H * E), 1) // E
    lane = lax.broadcasted_iota(jnp.int32, (BLK, LANE), 1)
    kcat = [jnp.concatenate([kp_ref[r], k_ref[r]], axis=0) for r in range(n_res)]
    vcat = [jnp.concatenate([vp_ref[r], v_ref[r]], axis=0) for r in range(n_res)]
    v_head = lax.broadcasted_iota(jnp.int32, vcat[0].shape, 1) // E
    v_masked = [[jnp.where(v_head == h, vc, jnp.zeros_like(vc)) for h in range(H)] for vc in vcat]
    blocks =[(r, i) for r in range(n_res) for i in range(n_blk)]

    def scores(r, i):
        q = q_ref[r, i * BLK:(i + 1) * BLK, :] * jnp.asarray(E ** -0.5, BF16)
        q_stack = jnp.concatenate([jnp.where(head == h, q, jnp.zeros_like(q)) for h in range(H)], axis=0)
        bias = bias_ref[first] if i == 0 else bias_ref[1]
        return _dot_nt(q_stack, kcat[r][i * BLK:(i + 2) * BLK]) + bias

    pending = [scores(*blk) for blk in blocks[:DIL_LOOKAHEAD]]
    for n, (r, i) in enumerate(blocks):
        s = pending.pop(0)
        if n + DIL_LOOKAHEAD < len(blocks):
            pending.append(scores(*blocks[n + DIL_LOOKAHEAD]))
        m = jnp.max(s, axis=-1, keepdims=True)
        p = jnp.exp(s - m)
        l = jnp.sum(p, axis=-1, keepdims=True)
        p = p.astype(BF16)
        p_heads = jnp.concatenate([p[h * BLK:(h + 1) * BLK] for h in range(H)], axis=1)
        v_heads = jnp.concatenate([vm[i * BLK:(i + 2) * BLK] for vm in v_masked[r]], axis=0)
        o = jnp.dot(p_heads, v_heads, preferred_element_type=F32)
        stat = jnp.zeros((BLK, LANE), F32)
        for h in range(H):
            hs = slice(h * BLK, (h + 1) * BLK)
            stat = jnp.where(lane == h, m[hs], jnp.where(lane == H + h, l[hs], stat))
        rows = slice(i * BLK, (i + 1) * BLK)
        o_ref[r, rows, :] = o.astype(o_ref.dtype)
        stat_ref[r, rows, :] = stat


def _dil_bias(window, dil, slopes):
    BLK = DIL_STEPS
    steps = np.arange(BLK)[:, None] + BLK - np.arange(2 * BLK)[None, :]
    ok = (steps >= 0) & (steps <= window // dil)
    bias = -slopes[:, None, None] * (steps * dil)[None].astype(np.float32)
    full = np.where(ok[None], bias, NEG).astype(np.float32)
    first = np.where((np.arange(2 * BLK) >= BLK)[None, None, :], full, NEG)
    return np.stack([first.reshape(-1, 2 * BLK), full.reshape(-1, 2 * BLK)]).astype(np.float32)


def _dil_pattern(qkv, window, dil, slopes):
    B, _, n, _ = qkv.shape
    HE, BLK = DIL_OUT, DIL_STEPS
    n_blk = min(DIL_QBLOCKS, n // BLK)
    n_res = min(dil, DIL_QBLOCKS // n_blk)
    rows = n_blk * BLK
    assert n % rows == 0 and dil % n_res == 0
    cur = lambda t: pl.BlockSpec((None, n_res, rows, HE), lambda b, r, j: (b, r, j, t))
    prev = lambda t: pl.BlockSpec((None, n_res, BLK, HE),
                                  lambda b, r, j: (b, r, jnp.maximum(j * n_blk - 1, 0), t))
    return pl.pallas_call(
        _dil_kernel,
        grid=(B, dil // n_res, n // rows),
        in_specs=[cur(0), prev(1), cur(1), prev(2), cur(2),
                  _const_spec((2, DIL_HEADS * BLK, 2 * BLK))],
        out_specs=[pl.BlockSpec((None, n_res, rows, HE), lambda b, r, j: (b, r, j, 0)),
                   pl.BlockSpec((None, n_res, rows, LANE), lambda b, r, j: (b, r, j, 0))],
        out_shape=[jax.ShapeDtypeStruct((B, dil, n, HE), BF16),
                   jax.ShapeDtypeStruct((B, dil, n, LANE), F32)],
        compiler_params=_cparams(("parallel", "parallel", "arbitrary"), 32),
        name=f"dil_attn_d{dil}",
    )(qkv, qkv, qkv, qkv, qkv, jnp.asarray(_dil_bias(window, dil, slopes)))


def _dilated(qkvs):
    n_pat = len(DIL_PATTERNS)
    n_h = n_pat * DIL_HEADS
    slopes = (2.0 ** (-8.0 * np.arange(1, n_h + 1) / n_h)).astype(np.float32).reshape(n_pat, DIL_HEADS)
    outs = []
    for p, (w, d) in enumerate(DIL_PATTERNS):
        qkv = qkvs[p] if qkvs[p].ndim == 4 else qkvs[p][:, None]
        assert qkv.shape[1] == d
        outs.append(_dil_pattern(qkv, w, d, slopes[p]))
    return [o for o, _ in outs], [l for _, l in outs]


def _gm_kernel(gm_ref, lnw_ref, lnb_ref, w_ref, bs_ref, o_ref):
    G, E, C = GM_GROUPS, GM_DIM, GM_CHUNK
    T = gm_ref.shape[0]
    blk = gm_ref[...].astype(F32)
    u = _gelu(blk[:, :GM_W])
    v = _gelu(blk[:, GM_W:])
    mu = jnp.mean(v, axis=-1, keepdims=True)
    var = jnp.mean(jnp.square(v - mu), axis=-1, keepdims=True)
    v = (v - mu) * lax.rsqrt(var + EPS) * lnw_ref[...] + lnb_ref[...]
    group = lax.broadcasted_iota(jnp.int32, (C, GM_W), 1) // E
    w = w_ref[...]
    bs = bs_ref[...]
    for c in range(T // C):
        rs = slice(c * C, (c + 1) * C)
        r = _dot(w, v[rs])
        sv = bs
        for gi in range(G):
            sv = sv + jnp.where(group == gi, r[gi * C:(gi + 1) * C], 0.0)
        o_ref[rs, :] = (u[rs] * sv).astype(o_ref.dtype)


def _gmlp_params(ws, bs):
    C = GM_CHUNK
    tril = jnp.tril(jnp.ones((C, C), bool))
    w_stack = jnp.where(tril, ws, 0.0).reshape(-1, GM_GROUPS * C, C).astype(BF16)
    bs_exp = jnp.repeat(jnp.swapaxes(bs, 1, 2), GM_DIM, axis=2)
    return w_stack, bs_exp


def _gmlp(gm, lnw, lnb, w_stack, bs_exp, layer):
    B, S, W = gm.shape
    C = GM_CHUNK
    return pl.pallas_call(
        _gm_kernel,
        grid=(B, S // T_GM),
        in_specs=[pl.BlockSpec((None, T_GM, W), lambda b, i: (b, i, 0)),
                  _layer_spec((1, GM_W), layer), _layer_spec((1, GM_W), layer),
                  _layer_spec((GM_GROUPS * C, C), layer), _layer_spec((C, GM_W), layer)],
        out_specs=pl.BlockSpec((None, T_GM, GM_W), lambda b, i: (b, i, 0)),
        out_shape=jax.ShapeDtypeStruct((B, S, GM_W), BF16),
        compiler_params=_cparams(("parallel", "parallel"), 32),
        name="gmlp_mixer",
    )(gm, lnw, lnb, w_stack, bs_exp)


def _ssd_kernel(z_ref, xbc_ref, dt_ref, dtb_ref, alog_ref, dskip_ref, nw_ref, tri_ref, exp_ref,
                o_ref, st_ref):
    H, P, G, N, L = SSD_HEADS, SSD_P, SSD_GROUPS, SSD_N, SSD_CHUNK
    R = H // G
    GW = R * P
    T = z_ref.shape[0]

    @pl.when(pl.program_id(1) == 0)
    def _():
        st_ref[...] = jnp.zeros_like(st_ref)

    dt_all = _softplus(dt_ref[...] + dtb_ref[...])
    da_all = dt_all * (-jnp.exp(alog_ref[...]))
    tri = tri_ref[...]
    expand = exp_ref[...]
    lower = lax.broadcasted_iota(jnp.int32, (L, L), 0) >= lax.broadcasted_iota(jnp.int32, (L, L), 1)
    head_in_group = lax.broadcasted_iota(jnp.int32, (L, GW), 1) // P

    chunks = [slice(c * L, (c + 1) * L) for c in range(T // L)]
    groups = [slice(gi * GW, (gi + 1) * GW) for gi in range(G)]
    b_cols = [slice(SSD_INNER + gi * N, SSD_INNER + (gi + 1) * N) for gi in range(G)]
    c_cols = [slice(SSD_INNER + SSD_BC + gi * N, SSD_INNER + SSD_BC + (gi + 1) * N) for gi in range(G)]
    xs = [xbc_ref[rs, :SSD_INNER].astype(F32) for rs in chunks]
    cs = [_dot_sel_l(tri, da_all[rs]) for rs in chunks]
    dt_e = [_dot(dt_all[rs], expand) for rs in chunks]
    dec_e = [_dot(jnp.exp(cs_c[L - 1:L] - cs_c) * dt_all[rs], expand) for rs, cs_c in zip(chunks, cs)]
    ecs_e = [_dot_sel_r(jnp.exp(cs_c), expand) for cs_c in cs]
    x_dt = [(x * e).astype(BF16) for x, e in zip(xs, dt_e)]
    x_dec = [(x * e).astype(BF16) for x, e in zip(xs, dec_e)]
    cb = [[_dot_nt(xbc_ref[rs, c_cols[gi]], xbc_ref[rs, b_cols[gi]]) for gi in range(G)]
          for rs in chunks]
    bx = [[_dot(xbc_ref[rs, b_cols[gi]].astype(F32).T, x_dec[c][:, groups[gi]]) for gi in range(G)]
          for c, rs in enumerate(chunks)]
    y_intra = []
    for c in range(len(chunks)):
        cs_t = cs[c].T
        per_group = []
        for gi in range(G):
            acc = None
            for r in range(R):
                h = gi * R + r
                seg = cs[c][:, h:h + 1] - cs_t[h:h + 1, :]
                wgt = cb[c][gi] * jnp.exp(jnp.where(lower, seg, NEG))
                y_h = _dot(wgt, x_dt[c][:, groups[gi]])
                acc = y_h if r == 0 else jnp.where(head_in_group == r, y_h, acc)
            per_group.append(acc)
        y_intra.append(per_group)

    for c, rs in enumerate(chunks):
        ys = []
        for gi, gs in enumerate(groups):
            y_state = _dot(xbc_ref[rs, c_cols[gi]], st_ref[:, gs]) * ecs_e[c][:, gs]
            st_ref[:, gs] = st_ref[:, gs] * ecs_e[c][L - 1:L, gs] + bx[c][gi]
            ys.append(y_state + y_intra[c][gi])
        y = jnp.concatenate(ys, axis=1) + xs[c] * dskip_ref[...]
        y = y * _silu(z_ref[rs, :].astype(F32))
        normed = []
        for gs in groups:
            y_g = y[:, gs]
            normed.append(y_g * lax.rsqrt(jnp.mean(y_g * y_g, axis=-1, keepdims=True) + EPS))
        o_ref[rs, :] = (jnp.concatenate(normed, axis=1) * nw_ref[...]).astype(o_ref.dtype)


def _ssd_params(dt_bias, a_log, d_skip, norm_w):
    pad = lambda t: jnp.pad(t, ((0, 0), (0, LANE - t.shape[1])))[:, None, :]
    return (pad(dt_bias), pad(a_log), jnp.repeat(d_skip, SSD_P, axis=1)[:, None, :], norm_w[:, None, :])


def _ssd(z, xbc, dt, params, layer):
    B, S, _ = z.shape
    L = SSD_CHUNK
    tri = np.tril(np.ones((L, L), np.float32))
    expand = np.zeros((LANE, SSD_INNER), np.float32)
    for h in range(SSD_HEADS):
        expand[h, h * SSD_P:(h + 1) * SSD_P] = 1.0
    return pl.pallas_call(
        _ssd_kernel,
        grid=(B, S // T_SSD),
        in_specs=[pl.BlockSpec((None, T_SSD, SSD_INNER), lambda b, i: (b, i, 0)),
                  pl.BlockSpec((None, T_SSD, SSD_XBC), lambda b, i: (b, i, 0)),
                  pl.BlockSpec((None, T_SSD, LANE), lambda b, i: (b, i, 0)),
                  _layer_spec((1, LANE), layer), _layer_spec((1, LANE), layer),
                  _layer_spec((1, SSD_INNER), layer), _layer_spec((1, SSD_INNER), layer),
                  _const_spec((L, L)), _const_spec((LANE, SSD_INNER))],
        out_specs=pl.BlockSpec((None, T_SSD, SSD_INNER), lambda b, i: (b, i, 0)),
        out_shape=jax.ShapeDtypeStruct((B, S, SSD_INNER), BF16),
        scratch_shapes=[pltpu.VMEM((SSD_N, SSD_INNER), F32)],
        compiler_params=_cparams(("parallel", "arbitrary"), 48),
        name="ssd_mixer",
    )(z, xbc, dt, *params, jnp.asarray(tri, BF16), jnp.asarray(expand, BF16))


def _token_order(src_ref, stage_ref):
    dil, per, w = src_ref.shape
    for r in range(dil):
        plane = src_ref[r].astype(F32)
        for t in range(w // LANE):
            stage_ref[t, pl.ds(r, per, stride=dil), :] = plane[:, t * LANE:(t + 1) * LANE]
    return jnp.concatenate([stage_ref[t] for t in range(w // LANE)], axis=1)


def _tail_kernel(x_ref, g1_ref, gla_ref, d0_ref, d1_ref, d2_ref, l0_ref, l1_ref, l2_ref, gm_ref, ssd_ref,
                 gates_ref, wg_ref, wd_ref, wm_ref, ws_ref, wo_ref, exp_ref,
                 nw_ref, sc_ref, sh_ref, g2_ref, w1_ref, w2_ref, fw_ref, o_ref,
                 od1_ref, od2_ref, ol1_ref, ol2_ref, *, final_norm):
    D = D_MODEL
    outs = (d0_ref[...].astype(F32), _token_order(d1_ref, od1_ref), _token_order(d2_ref, od2_ref))
    stats = (l0_ref[...], _token_order(l1_ref, ol1_ref), _token_order(l2_ref, ol2_ref))
    dens = [pltpu.roll(s, LANE - DIL_HEADS, axis=1) for s in stats]
    m = jnp.maximum(jnp.maximum(stats[0], stats[1]), stats[2])
    es = [jnp.exp(s - m) for s in stats]
    total = es[0] * dens[0] + es[1] * dens[1] + es[2] * dens[2]
    is_head = lax.broadcasted_iota(jnp.int32, m.shape, 1) < DIL_HEADS
    expand = exp_ref[...]
    o_dil = None
    for e, o_p in zip(es, outs):
        term = _dot_sel_r(jnp.where(is_head, e / total, 0.0), expand) * o_p
        o_dil = term if o_dil is None else o_dil + term
    branches = ((gla_ref[...], wg_ref), (o_dil, wd_ref), (gm_ref[...], wm_ref), (ssd_ref[...], ws_ref))
    merged = None
    for i, (o_b, w_ref) in enumerate(branches):
        g = gates_ref[:, i * D:(i + 1) * D]
        gate = (jnp.tanh(g * jnp.asarray(0.5, g.dtype)) * jnp.asarray(0.5, g.dtype)
                + jnp.asarray(0.5, g.dtype)).astype(F32)
        term = gate * _dot(o_b, w_ref[...])
        merged = term if merged is None else merged + term
    x = x_ref[...] + g1_ref[...] * _dot(merged, wo_ref[...])

    h = _modulated_norm(x, nw_ref[...], sc_ref[...], sh_ref[...]).astype(BF16)
    acc = jnp.zeros(x.shape, F32)
    for c0 in range(0, MLP_HIDDEN, D_MODEL):
        a = jnp.maximum(jnp.dot(h, w1_ref[:, c0:c0 + D_MODEL], preferred_element_type=F32), 0.0)
        acc = acc + _dot(a * a, w2_ref[c0:c0 + D_MODEL, :])
    y = x + g2_ref[...] * acc
    if final_norm:
        y = y * lax.rsqrt(jnp.mean(y * y, axis=-1, keepdims=True) + EPS) * fw_ref[...]
    o_ref[...] = y


def _mod_spec(layer, which):
    return pl.BlockSpec((None, None, None, 1, D_MODEL), lambda b, i: (layer, b, which, 0, 0))


def _tail(x, mod, o_gla, o_dil, lse_dil, o_gm, o_ssd, gates, w_gla, w_dil, w_gm, w_ssd, w_out,
          nw2, w1, w2, fw, layer, final_norm):
    B, S, D = x.shape
    TM = TM_MLP
    tok = lambda w: pl.BlockSpec((None, TM, w), lambda b, i: (b, i, 0))

    def plane(w, dil):
        if dil == 1:
            return pl.BlockSpec((None, None, TM, w), lambda b, i: (b, 0, i, 0))
        return pl.BlockSpec((None, dil, TM // dil, w), lambda b, i: (b, 0, i, 0))

    dils = [d for _, d in DIL_PATTERNS]
    assert dils[0] == 1 and len(dils) == 3
    expand = np.zeros((LANE, DIL_OUT), np.float32)
    for h in range(DIL_HEADS):
        expand[h, h * DIL_DIM:(h + 1) * DIL_DIM] = 1.0
    return pl.pallas_call(
        functools.partial(_tail_kernel, final_norm=final_norm),
        grid=(B, S // TM),
        in_specs=[tok(D), _mod_spec(layer, 2), tok(GLA_V)]
                 + [plane(DIL_OUT, d) for d in dils] + [plane(LANE, d) for d in dils]
                 + [tok(GM_W), tok(SSD_INNER), tok(N_BRANCH * D),
                    _layer_spec((GLA_V, D), layer), _layer_spec((DIL_OUT, D), layer),
                    _layer_spec((GM_W, D), layer), _layer_spec((SSD_INNER, D), layer),
                    _layer_spec((D, D), layer), _const_spec((LANE, DIL_OUT)),
                    _layer_spec((1, D), layer), _mod_spec(layer, 4), _mod_spec(layer, 3),
                    _mod_spec(layer, 5), _layer_spec((D, MLP_HIDDEN), layer),
                    _layer_spec((MLP_HIDDEN, D), layer), _const_spec((1, D))],
        out_specs=tok(D),
        out_shape=jax.ShapeDtypeStruct((B, S, D), F32),
        scratch_shapes=[pltpu.VMEM((DIL_OUT // LANE, TM, LANE), F32), pltpu.VMEM((DIL_OUT // LANE, TM, LANE), F32),
                        pltpu.VMEM((1, TM, LANE), F32), pltpu.VMEM((1, TM, LANE), F32)],
        compiler_params=_cparams(("parallel", "parallel"), 58),
        name="tail_final" if final_norm else "tail",
    )(x, mod, o_gla, *o_dil, *lse_dil, o_gm, o_ssd, gates, w_gla, w_dil, w_gm, w_ssd, w_out,
      jnp.asarray(expand, BF16), nw2, mod, mod, mod, w1, w2, fw)


def _repack_plan():
    whole = lambda a, b: [(_SP[a], _SP[b] - _SP[a], _SP[b] - _SP[a])]
    plan = {"gla": whole(0, 4), "glr": [(_SP[4], LANE, GLA_RANK)], "gm": whole(8, 10),
            "ssd_z": whole(10, 11), "ssd_xbc": whole(11, 12), "dt": [(_SP[12], LANE, SSD_HEADS)],
            "gates": whole(13, 14)}
    for p in range(len(DIL_PATTERNS)):
        plan[f"dil{p}"] = [(_SP[t] + p * DIL_OUT, DIL_OUT, DIL_OUT) for t in (5, 6, 7)]
    plan = [plan[name] for name in PROJ_NAMES]
    assert [sum(w for _, w, _ in pieces) for pieces in plan] == [o[1] for o in PROJ_OUTS]
    return plan


def _repack_kernel(wt_ref, o_ref, *, valid):
    v = wt_ref[0]
    if valid < v.shape[0]:
        v = jnp.where(lax.broadcasted_iota(jnp.int32, v.shape, 0) < valid, v, 0.0)
    o_ref[...] = v.T.astype(o_ref.dtype)


def _pack_w_in(w_in):
    L, D, W = w_in.shape
    w_t = jnp.swapaxes(w_in, 1, 2)
    outs = []
    for (name, width, _, _), pieces in zip(PROJ_OUTS, _repack_plan()):
        start, piece_w, valid = pieces[0]
        if len(pieces) > 1:
            tile, stride = piece_w, pieces[1][0] - start
            assert all(p == (start + i * stride, tile, tile) for i, p in enumerate(pieces))
        else:
            tile = next(t for t in (512, 256, LANE) if piece_w % t == 0)
            stride = tile
        assert start % 8 == 0 and stride % 8 == 0 and width % tile == 0
        outs.append(pl.pallas_call(
            functools.partial(_repack_kernel, valid=min(valid, tile)),
            grid=(L, width // tile),
            in_specs=[pl.BlockSpec((pl.Element(1), pl.Element(tile), pl.Element(D)),
                                   lambda l, t, start=start, stride=stride: (
                                       l, pl.multiple_of(start + t * stride, 8), 0))],
            out_specs=pl.BlockSpec((None, D, tile), lambda l, t: (l, 0, t)),
            out_shape=jax.ShapeDtypeStruct((L, D, width), BF16),
            compiler_params=_cparams(("parallel", "parallel"), 32),
            name=f"repack_{name}",
        )(w_t))
    return outs


def kernel(x, c, w_ada, b_ada, norm1_w, norm2_w, w_in, gla_w_a2, gla_b_a, gla_norm_w, gm_ln_w, gm_ln_b, gm_ws, gm_bs, ssd_conv_w, ssd_conv_b, ssd_dt_bias, ssd_a_log, ssd_d, ssd_norm_w, w_br_gla, w_br_dil, w_br_gm, w_br_ssd, w_out, w_mlp1, w_mlp2, final_norm_w):
    mod = _ada_mod(c, w_ada, b_ada)
    w_parts = _pack_w_in(w_in)
    wa = jnp.pad(gla_w_a2, ((0, 0), (0, LANE - GLA_RANK), (0, 0))).astype(BF16)
    w_gla, w_dil, w_gm, w_ssd, w_o, w1, w2 = (t.astype(BF16) for t in (
        w_br_gla, w_br_dil, w_br_gm, w_br_ssd, w_out, w_mlp1, w_mlp2))
    rows = lambda t: t[:, None, :]
    n1, n2 = rows(norm1_w), rows(norm2_w)
    gla_ba, gla_nw = rows(gla_b_a), rows(jnp.tile(gla_norm_w, (1, GLA_HEADS)))
    gm_lnw, gm_lnb = rows(gm_ln_w), rows(gm_ln_b)
    gm_w, gm_b = _gmlp_params(gm_ws, gm_bs)
    ssd_params = _ssd_params(ssd_dt_bias, ssd_a_log, ssd_d, ssd_norm_w)
    conv_b = rows(ssd_conv_b)
    fw = final_norm_w.reshape(1, -1)
    for l in range(DEPTH):
        p = dict(zip(PROJ_NAMES, _inproj(x, n1, mod, ssd_conv_w, conv_b, w_parts, l)))
        o_gla = _gla(p["gla"], p["glr"], wa, gla_ba, gla_nw, l)
        o_dil, lse_dil = _dilated([p["dil0"], p["dil1"], p["dil2"]])
        o_gm = _gmlp(p["gm"], gm_lnw, gm_lnb, gm_w, gm_b, l)
        o_ssd = _ssd(p["ssd_z"], p["ssd_xbc"], p["dt"], ssd_params, l)
        x = _tail(x, mod, o_gla, o_dil, lse_dil, o_gm, o_ssd, p["gates"], w_gla, w_dil, w_gm, w_ssd, w_o,
                  n2, w1, w2, fw, l, final_norm=(l == DEPTH - 1))
    return x
```

```python
import functools

import numpy as np
import jax
import jax.numpy as jnp
from jax import lax
from jax.experimental import pallas as pl
from jax.experimental.pallas import tpu as pltpu

F32 = jnp.float32
BF16 = jnp.bfloat16

D_MODEL = 1024
DEPTH = 4
EPS = 1e-6

GLA_HEADS, GLA_DK, GLA_DV, GLA_RANK, GLA_CHUNK = 4, 32, 64, 16, 64
GLA_GATE_NORM = 16.0
DIL_PATTERNS = ((128, 1), (512, 4), (2048, 16))
DIL_HEADS, DIL_DIM, DIL_STEPS = 4, 64, 128
GM_GROUPS, GM_DIM, GM_CHUNK = 4, 64, 128
SSD_HEADS, SSD_P, SSD_GROUPS, SSD_N, SSD_CONV, SSD_CHUNK = 8, 64, 2, 128, 4, 128
N_BRANCH = 4
MLP_HIDDEN = 4 * D_MODEL

GLA_QK = GLA_HEADS * GLA_DK
GLA_V = GLA_HEADS * GLA_DV
DIL_W = len(DIL_PATTERNS) * DIL_HEADS * DIL_DIM
DIL_OUT = DIL_HEADS * DIL_DIM
GM_W = GM_GROUPS * GM_DIM
SSD_INNER = SSD_HEADS * SSD_P
SSD_BC = SSD_GROUPS * SSD_N
SSD_XBC = SSD_INNER + 2 * SSD_BC
IN_SPLITS = (GLA_QK, GLA_QK, GLA_V, GLA_V, GLA_RANK, DIL_W, DIL_W, DIL_W, GM_W, GM_W,
             SSD_INNER, SSD_XBC, SSD_HEADS, N_BRANCH * D_MODEL)
_SP = tuple(int(v) for v in np.cumsum((0,) + IN_SPLITS))

LANE = 128
NEG = -1e30

PROJ_OUTS = (("gla", 2 * GLA_QK + 2 * GLA_V, BF16, 1), ("glr", LANE, F32, 1),
             ("dil0", 3 * DIL_OUT, BF16, DIL_PATTERNS[0][1]), ("dil1", 3 * DIL_OUT, BF16, DIL_PATTERNS[1][1]),
             ("dil2", 3 * DIL_OUT, BF16, DIL_PATTERNS[2][1]),
             ("gm", 2 * GM_W, BF16, 1), ("ssd_z", SSD_INNER, BF16, 1), ("ssd_xbc", SSD_XBC, BF16, 1),
             ("dt", LANE, F32, 1), ("gates", N_BRANCH * D_MODEL, BF16, 1))
PROJ_NAMES = tuple(o[0] for o in PROJ_OUTS)

TM_PROJ = 512
TM_MLP = 512
T_GLA = 512
GLA_CUMSUM_ROWS = 256
T_GM = 1024
T_SSD = 512
COL_CHUNK = 512
DIL_QBLOCKS = 8
DIL_LOOKAHEAD = 2


V7X_VMEM_BYTES = 64 * 1024 * 1024


def _cparams(sem, vmem_mib):
    assert vmem_mib * 1024 * 1024 <= V7X_VMEM_BYTES
    return pltpu.CompilerParams(dimension_semantics=sem, vmem_limit_bytes=V7X_VMEM_BYTES)


def _const_spec(shape):
    nd = len(shape)
    return pl.BlockSpec(shape, lambda *_: (0,) * nd, pipeline_mode=pl.Buffered(1))


def _layer_spec(shape, layer):
    nd = len(shape)
    return pl.BlockSpec((None,) + tuple(shape), lambda *_: (layer,) + (0,) * nd,
                        pipeline_mode=pl.Buffered(1))


def _dot(a, b):
    return jnp.dot(a.astype(BF16), b.astype(BF16), preferred_element_type=F32)


def _dot_nt(a, b):
    return lax.dot_general(a.astype(BF16), b.astype(BF16), (((1,), (1,)), ((), ())),
                           preferred_element_type=F32)


def _split(a):
    hi = a.astype(BF16)
    lo = (a - hi.astype(F32)).astype(BF16)
    return hi, lo


def _dot_sel_r(a, sel):
    hi, lo = _split(a)
    return (jnp.dot(hi, sel, preferred_element_type=F32) + jnp.dot(lo, sel, preferred_element_type=F32))


def _dot_sel_l(sel, a):
    hi, lo = _split(a)
    return (jnp.dot(sel, hi, preferred_element_type=F32) + jnp.dot(sel, lo, preferred_element_type=F32))


def _sigmoid(x):
    return 1.0 / (1.0 + jnp.exp(-x))


def _silu(x):
    return x * _sigmoid(x)


def _gelu(x):
    return 0.5 * x * (1.0 + lax.erf(x * (2.0 ** -0.5)))


def _softplus(x):
    return jnp.maximum(x, 0.0) + jnp.log1p(jnp.exp(-jnp.abs(x)))


def _modulated_norm(x, nw, sc, sh):
    y = x * lax.rsqrt(jnp.mean(x * x, axis=-1, keepdims=True) + EPS)
    return (y * nw) * (1.0 + sc) + sh


def _ada_kernel(c_ref, w_ref, b_ref, o_ref):
    o_ref[...] = _dot(_silu(c_ref[...]), w_ref[...]) + b_ref[...]


def _ada_mod(c, w_ada, b_ada):
    B, D = c.shape
    rows = 16
    c_pad = jnp.pad(c, ((0, rows - B), (0, 0)))
    n_col = w_ada.shape[-1] // D
    out = pl.pallas_call(
        _ada_kernel,
        grid=(DEPTH, n_col),
        in_specs=[pl.BlockSpec((rows, D), lambda l, j: (0, 0)),
                  pl.BlockSpec((None, D, D), lambda l, j: (l, 0, j)),
                  pl.BlockSpec((None, 1, D), lambda l, j: (l, 0, j))],
        out_specs=pl.BlockSpec((None, rows, D), lambda l, j: (l, 0, j)),
        out_shape=jax.ShapeDtypeStruct((DEPTH, rows, n_col * D), F32),
        compiler_params=_cparams(("arbitrary", "arbitrary"), 32),
        name="ada_mod",
    )(c_pad, w_ada, b_ada.reshape(DEPTH, 1, -1))
    return out[:, :B].reshape(DEPTH, B, n_col, 1, D)


def _inproj_kernel(x_ref, nw_ref, sc_ref, sh_ref, cw_ref, cb_ref, *refs):
    n_out = len(PROJ_OUTS)
    w_refs, o_refs, (stage_ref, carry_ref, win_ref) = refs[:n_out], refs[n_out:2 * n_out], refs[2 * n_out:]
    TM = x_ref.shape[0]
    KEEP = carry_ref.shape[0]

    @pl.when(pl.program_id(1) == 0)
    def _():
        carry_ref[...] = jnp.zeros_like(carry_ref)

    h = _modulated_norm(x_ref[...], nw_ref[...], sc_ref[...], sh_ref[...]).astype(BF16)
    for w_ref, o_ref, (name, width, dtype, dil) in zip(w_refs, o_refs, PROJ_OUTS):
        for c0 in range(0, width, COL_CHUNK):
            c1 = min(c0 + COL_CHUNK, width)
            res = jnp.dot(h, w_ref[:, c0:c1], preferred_element_type=F32)
            if name == "ssd_xbc":
                win_ref[0:KEEP, :] = carry_ref[:, c0:c1]
                win_ref[KEEP:, :] = res
                carry_ref[:, c0:c1] = res[TM - KEEP:]
                conv = cb_ref[:, c0:c1]
                for j in range(SSD_CONV):
                    off = KEEP - (SSD_CONV - 1) + j
                    conv = conv + cw_ref[j:j + 1, c0:c1] * win_ref[off:off + TM, :]
                o_ref[:, c0:c1] = _silu(conv).astype(dtype)
            elif dil == 1:
                o_ref[:, c0:c1] = res.astype(dtype)
            else:
                for t in range((c1 - c0) // LANE):
                    stage_ref[c0 // LANE + t] = res[:, t * LANE:(t + 1) * LANE]
        if dil > 1:
            for r in range(dil):
                for t in range(width // LANE):
                    o_ref[r, :, t * LANE:(t + 1) * LANE] = stage_ref[
                        t, pl.ds(r, TM // dil, stride=dil), :].astype(dtype)


def _inproj(x, nw, mod, conv_w, conv_b, w_parts, layer):
    B, S, D = x.shape
    TM = TM_PROJ
    out_specs, out_shape = [], []
    for _, w, dt, dil in PROJ_OUTS:
        if dil == 1:
            out_specs.append(pl.BlockSpec((None, TM, w), lambda b, i: (b, i, 0)))
            out_shape.append(jax.ShapeDtypeStruct((B, S, w), dt))
        else:
            out_specs.append(pl.BlockSpec((None, dil, TM // dil, w), lambda b, i: (b, 0, i, 0)))
            out_shape.append(jax.ShapeDtypeStruct((B, dil, S // dil, w), dt))
    return pl.pallas_call(
        _inproj_kernel,
        grid=(B, S // TM),
        in_specs=[pl.BlockSpec((None, TM, D), lambda b, i: (b, i, 0)),
                  _layer_spec((1, D), layer), _mod_spec(layer, 1), _mod_spec(layer, 0),
                  _layer_spec((SSD_CONV, SSD_XBC), layer), _layer_spec((1, SSD_XBC), layer)]
                 + [_layer_spec((D, o[1]), layer) for o in PROJ_OUTS],
        out_specs=out_specs,
        out_shape=out_shape,
        scratch_shapes=[pltpu.VMEM((3 * DIL_OUT // LANE, TM, LANE), F32),
                        pltpu.VMEM((8, SSD_XBC), F32),
                        pltpu.VMEM((TM + 8, COL_CHUNK), F32)],
        compiler_params=_cparams(("parallel", "arbitrary"), 58),
        name="inproj",
    )(x, nw, mod, mod, conv_w, conv_b, *w_parts)


def _gla_kernel(gla_ref, glr_ref, wa_ref, ba_ref, nw_ref, tri_ref, ones_ref, o_ref, st_ref):
    C, H, K, V = GLA_CHUNK, GLA_HEADS, GLA_DK, GLA_DV
    T = gla_ref.shape[0]

    @pl.when(pl.program_id(1) == 0)
    def _():
        st_ref[...] = jnp.zeros_like(st_ref)

    blk = gla_ref[...]
    q = blk[:, 0:GLA_QK].astype(F32) * (K ** -0.5)
    k = blk[:, GLA_QK:2 * GLA_QK].astype(F32)
    v = blk[:, 2 * GLA_QK:2 * GLA_QK + GLA_V]
    g = blk[:, 2 * GLA_QK + GLA_V:].astype(F32)
    pre = _dot(glr_ref[...], wa_ref[...]) + ba_ref[...]
    log_a = -_softplus(-pre) * (1.0 / GLA_GATE_NORM)
    tri = tri_ref[...]
    TB = tri.shape[0]
    b_all = jnp.concatenate([_dot_sel_l(tri, log_a[i * TB:(i + 1) * TB]) for i in range(T // TB)],
                            axis=0)

    k_head = lax.broadcasted_iota(jnp.int32, (C, GLA_QK), 1) // K
    v_head = lax.broadcasted_iota(jnp.int32, (C, GLA_V), 1) // V
    row = lax.broadcasted_iota(jnp.int32, (H * C, C), 0)
    causal = (row % C) >= lax.broadcasted_iota(jnp.int32, (H * C, C), 1)
    st_mask = (lax.broadcasted_iota(jnp.int32, (GLA_V, GLA_QK), 0) // V
               == lax.broadcasted_iota(jnp.int32, (GLA_V, GLA_QK), 1) // K)

    n_chunk = T // C
    chunks = [slice(c * C, (c + 1) * C) for c in range(n_chunk)]
    b_c = [b_all[rs] for rs in chunks]
    q_t = [q[rs] * jnp.exp(b) for rs, b in zip(chunks, b_c)]
    k_t = [k[rs] * jnp.exp(-b) for rs, b in zip(chunks, b_c)]
    k_dec = [k[rs] * jnp.exp(b[C - 1:C] - b) for rs, b in zip(chunks, b_c)]
    att = []
    for c in range(n_chunk):
        q_stack = jnp.concatenate([jnp.where(k_head == h, q_t[c], 0.0) for h in range(H)], axis=0)
        att.append(jnp.where(causal, _dot_nt(q_stack, k_t[c]), 0.0))
    kv_t = [jnp.where(st_mask, _dot(v[rs].astype(F32).T, k_dec[c]), 0.0)
            for c, rs in enumerate(chunks)]
    o_all = [_dot(att[c], v[rs]) for c, rs in enumerate(chunks)]
    st = st_ref[...]
    outs = []
    for c in range(n_chunk):
        o = _dot_nt(q_t[c], st)
        for h in range(H):
            o = o + jnp.where(v_head == h, o_all[c][h * C:(h + 1) * C], 0.0)
        st = st * jnp.exp(b_c[c][C - 1:C]) + kv_t[c]
        outs.append(o)
    st_ref[...] = st

    o = jnp.concatenate(outs, axis=0)
    ms = _dot_sel_r(o * o, ones_ref[...]) * (1.0 / V)
    o = o * lax.rsqrt(ms + EPS) * nw_ref[...]
    o_ref[...] = (o * _silu(g)).astype(o_ref.dtype)


def _gla(gla, glr, wa, ba, nw, layer):
    B, S, W = gla.shape
    T, C, TB = T_GLA, GLA_CHUNK, GLA_CUMSUM_ROWS
    idx = np.arange(TB)
    tri = ((idx[:, None] // C == idx[None, :] // C) & (idx[:, None] >= idx[None, :]))
    hv = np.arange(GLA_V) // GLA_DV
    ones = hv[:, None] == hv[None, :]
    return pl.pallas_call(
        _gla_kernel,
        grid=(B, S // T),
        in_specs=[pl.BlockSpec((None, T, W), lambda b, i: (b, i, 0)),
                  pl.BlockSpec((None, T, LANE), lambda b, i: (b, i, 0)),
                  _layer_spec((LANE, GLA_QK), layer), _layer_spec((1, GLA_QK), layer),
                  _layer_spec((1, GLA_V), layer), _const_spec((TB, TB)), _const_spec((GLA_V, GLA_V))],
        out_specs=pl.BlockSpec((None, T, GLA_V), lambda b, i: (b, i, 0)),
        out_shape=jax.ShapeDtypeStruct((B, S, GLA_V), BF16),
        scratch_shapes=[pltpu.VMEM((GLA_V, GLA_QK), F32)],
        compiler_params=_cparams(("parallel", "arbitrary"), 32),
        name="gla_mixer",
    )(gla, glr, wa, ba, nw, jnp.asarray(tri, BF16), jnp.asarray(ones, BF16))


def _dil_kernel(*refs, row_steps):
    n_pat = len(row_steps)
    for p in range(n_pat):
        first = jnp.minimum(pl.program_id(1) % row_steps[p], 1)
        _dil_planes(*refs[6 * p:6 * p + 6], *refs[6 * n_pat + 2 * p:6 * n_pat + 2 * p + 2], first)


def _dil_planes(q_ref, kp_ref, k_ref, vp_ref, v_ref, bias_ref, o_ref, stat_ref, first):
    H, E, BLK = DIL_HEADS, DIL_DIM, DIL_STEPS
    n_res, n_blk = q_ref.shape[0], q_ref.shape[1] // BLK
    head = lax.broadcasted_iota(jnp.int32, (BLK, H * E), 1) // E
    lane = lax.broadcasted_iota(jnp.int32, (BLK, LANE), 1)
    kcat = [jnp.concatenate([kp_ref[r], k_ref[r]], axis=0) for r in range(n_res)]
    vcat = [jnp.concatenate([vp_ref[r], v_ref[r]], axis=0) for r in range(n_res)]
    v_head = lax.broadcasted_iota(jnp.int32, vcat[0].shape, 1) // E
    v_masked = [[jnp.where(v_head == h, vc, jnp.zeros_like(vc)) for h in range(H)] for vc in vcat]
    blocks =[(r, i) for r in range(n_res) for i in range(n_blk)]

    def scores(r, i):
        q = q_ref[r, i * BLK:(i + 1) * BLK, :] * jnp.asarray(E ** -0.5, BF16)
        q_stack = jnp.concatenate([jnp.where(head == h, q, jnp.zeros_like(q)) for h in range(H)], axis=0)
        bias = bias_ref[first] if i == 0 else bias_ref[1]
        return _dot_nt(q_stack, kcat[r][i * BLK:(i + 2) * BLK]) + bias

    pending = [scores(*blk) for blk in blocks[:DIL_LOOKAHEAD]]
    for n, (r, i) in enumerate(blocks):
        s = pending.pop(0)
        if n + DIL_LOOKAHEAD < len(blocks):
            pending.append(scores(*blocks[n + DIL_LOOKAHEAD]))
        m = jnp.max(s, axis=-1, keepdims=True)
        p = jnp.exp(s - m)
        l = jnp.sum(p, axis=-1, keepdims=True)
        p = p.astype(BF16)
        p_heads = jnp.concatenate([p[h * BLK:(h + 1) * BLK] for h in range(H)], axis=1)
        v_heads = jnp.concatenate([vm[i * BLK:(i + 2) * BLK] for vm in v_masked[r]], axis=0)
        o = jnp.dot(p_heads, v_heads, preferred_element_type=F32)
        stat = jnp.zeros((BLK, LANE), F32)
        for h in range(H):
            hs = slice(h * BLK, (h + 1) * BLK)
            stat = jnp.where(lane == h, m[hs], jnp.where(lane == H + h, l[hs], stat))
        rows = slice(i * BLK, (i + 1) * BLK)
        o_ref[r, rows, :] = o.astype(o_ref.dtype)
        stat_ref[r, rows, :] = stat


def _dil_bias(window, dil, slopes):
    BLK = DIL_STEPS
    steps = np.arange(BLK)[:, None] + BLK - np.arange(2 * BLK)[None, :]
    ok = (steps >= 0) & (steps <= window // dil)
    bias = -slopes[:, None, None] * (steps * dil)[None].astype(np.float32)
    full = np.where(ok[None], bias, NEG).astype(np.float32)
    first = np.where((np.arange(2 * BLK) >= BLK)[None, None, :], full, NEG)
    return np.stack([first.reshape(-1, 2 * BLK), full.reshape(-1, 2 * BLK)]).astype(np.float32)


def _dil_pattern(qkv, window, dil, slopes):
    B, _, n, _ = qkv.shape
    HE, BLK = DIL_OUT, DIL_STEPS
    n_blk = min(DIL_QBLOCKS, n // BLK)
    n_res = min(dil, DIL_QBLOCKS // n_blk)
    rows = n_blk * BLK
    assert n % rows == 0 and dil % n_res == 0
    row_steps = n // rows
    at = lambda j: (j // row_steps, j % row_steps)
    cur = lambda t: pl.BlockSpec((None, n_res, rows, HE), lambda b, j: (b, *at(j), t))
    prev = lambda t: pl.BlockSpec((None, n_res, BLK, HE),
                                  lambda b, j: (b, at(j)[0], jnp.maximum(at(j)[1] * n_blk - 1, 0), t))
    in_specs = [cur(0), prev(1), cur(1), prev(2), cur(2), _const_spec((2, DIL_HEADS * BLK, 2 * BLK))]
    out_specs = [pl.BlockSpec((None, n_res, rows, HE), lambda b, j: (b, *at(j), 0)),
                 pl.BlockSpec((None, n_res, rows, LANE), lambda b, j: (b, *at(j), 0))]
    out_shape = [jax.ShapeDtypeStruct((B, dil, n, HE), BF16), jax.ShapeDtypeStruct((B, dil, n, LANE), F32)]
    args = [qkv, qkv, qkv, qkv, qkv, jnp.asarray(_dil_bias(window, dil, slopes))]
    return in_specs, out_specs, out_shape, args, row_steps, (dil // n_res) * row_steps


def _dilated(qkvs):
    n_pat = len(DIL_PATTERNS)
    n_h = n_pat * DIL_HEADS
    slopes = (2.0 ** (-8.0 * np.arange(1, n_h + 1) / n_h)).astype(np.float32).reshape(n_pat, DIL_HEADS)
    in_specs, out_specs, out_shape, args, row_steps, steps = [], [], [], [], [], set()
    for p, (w, d) in enumerate(DIL_PATTERNS):
        qkv = qkvs[p] if qkvs[p].ndim == 4 else qkvs[p][:, None]
        assert qkv.shape[1] == d
        i_s, o_s, o_sh, a, rs, n_steps = _dil_pattern(qkv, w, d, slopes[p])
        in_specs += i_s; out_specs += o_s; out_shape += o_sh; args += a
        row_steps.append(rs); steps.add(n_steps)
    assert len(steps) == 1, "patterns must split into equally many grid steps"
    outs = pl.pallas_call(
        functools.partial(_dil_kernel, row_steps=tuple(row_steps)),
        grid=(qkvs[0].shape[0], steps.pop()),
        in_specs=in_specs, out_specs=out_specs, out_shape=out_shape,
        compiler_params=_cparams(("parallel", "arbitrary"), 32),
        name="dil_attn",
    )(*args)
    return list(outs[0::2]), list(outs[1::2])


def _gm_kernel(gm_ref, lnw_ref, lnb_ref, w_ref, bs_ref, o_ref):
    G, E, C = GM_GROUPS, GM_DIM, GM_CHUNK
    T = gm_ref.shape[0]
    blk = gm_ref[...].astype(F32)
    u = _gelu(blk[:, :GM_W])
    v = _gelu(blk[:, GM_W:])
    mu = jnp.mean(v, axis=-1, keepdims=True)
    var = jnp.mean(jnp.square(v - mu), axis=-1, keepdims=True)
    v = (v - mu) * lax.rsqrt(var + EPS) * lnw_ref[...] + lnb_ref[...]
    group = lax.broadcasted_iota(jnp.int32, (C, GM_W), 1) // E
    w = w_ref[...]
    bs = bs_ref[...]
    for c in range(T // C):
        rs = slice(c * C, (c + 1) * C)
        r = _dot(w, v[rs])
        sv = bs
        for gi in range(G):
            sv = sv + jnp.where(group == gi, r[gi * C:(gi + 1) * C], 0.0)
        o_ref[rs, :] = (u[rs] * sv).astype(o_ref.dtype)


def _gmlp_params(ws, bs):
    C = GM_CHUNK
    tril = jnp.tril(jnp.ones((C, C), bool))
    w_stack = jnp.where(tril, ws, 0.0).reshape(-1, GM_GROUPS * C, C).astype(BF16)
    bs_exp = jnp.repeat(jnp.swapaxes(bs, 1, 2), GM_DIM, axis=2)
    return w_stack, bs_exp


def _gmlp(gm, lnw, lnb, w_stack, bs_exp, layer):
    B, S, W = gm.shape
    C = GM_CHUNK
    return pl.pallas_call(
        _gm_kernel,
        grid=(B, S // T_GM),
        in_specs=[pl.BlockSpec((None, T_GM, W), lambda b, i: (b, i, 0)),
                  _layer_spec((1, GM_W), layer), _layer_spec((1, GM_W), layer),
                  _layer_spec((GM_GROUPS * C, C), layer), _layer_spec((C, GM_W), layer)],
        out_specs=pl.BlockSpec((None, T_GM, GM_W), lambda b, i: (b, i, 0)),
        out_shape=jax.ShapeDtypeStruct((B, S, GM_W), BF16),
        compiler_params=_cparams(("parallel", "parallel"), 32),
        name="gmlp_mixer",
    )(gm, lnw, lnb, w_stack, bs_exp)


def _ssd_kernel(z_ref, xbc_ref, dt_ref, dtb_ref, alog_ref, dskip_ref, nw_ref, tri_ref, exp_ref,
                o_ref, st_ref):
    H, P, G, N, L = SSD_HEADS, SSD_P, SSD_GROUPS, SSD_N, SSD_CHUNK
    R = H // G
    GW = R * P
    T = z_ref.shape[0]

    @pl.when(pl.program_id(1) == 0)
    def _():
        st_ref[...] = jnp.zeros_like(st_ref)

    dt_all = _softplus(dt_ref[...] + dtb_ref[...])
    da_all = dt_all * (-jnp.exp(alog_ref[...]))
    tri = tri_ref[...]
    expand = exp_ref[...]
    lower = lax.broadcasted_iota(jnp.int32, (L, L), 0) >= lax.broadcasted_iota(jnp.int32, (L, L), 1)
    head_in_group = lax.broadcasted_iota(jnp.int32, (L, GW), 1) // P

    chunks = [slice(c * L, (c + 1) * L) for c in range(T // L)]
    groups = [slice(gi * GW, (gi + 1) * GW) for gi in range(G)]
    b_cols = [slice(SSD_INNER + gi * N, SSD_INNER + (gi + 1) * N) for gi in range(G)]
    c_cols = [slice(SSD_INNER + SSD_BC + gi * N, SSD_INNER + SSD_BC + (gi + 1) * N) for gi in range(G)]
    xs = [xbc_ref[rs, :SSD_INNER].astype(F32) for rs in chunks]
    cs = [_dot_sel_l(tri, da_all[rs]) for rs in chunks]
    dt_e = [_dot(dt_all[rs], expand) for rs in chunks]
    dec_e = [_dot(jnp.exp(cs_c[L - 1:L] - cs_c) * dt_all[rs], expand) for rs, cs_c in zip(chunks, cs)]
    ecs_e = [_dot_sel_r(jnp.exp(cs_c), expand) for cs_c in cs]
    x_dt = [(x * e).astype(BF16) for x, e in zip(xs, dt_e)]
    x_dec = [(x * e).astype(BF16) for x, e in zip(xs, dec_e)]
    cb = [[_dot_nt(xbc_ref[rs, c_cols[gi]], xbc_ref[rs, b_cols[gi]]) for gi in range(G)]
          for rs in chunks]
    bx = [[_dot(xbc_ref[rs, b_cols[gi]].astype(F32).T, x_dec[c][:, groups[gi]]) for gi in range(G)]
          for c, rs in enumerate(chunks)]
    y_intra = []
    for c in range(len(chunks)):
        cs_t = cs[c].T
        per_group = []
        for gi in range(G):
            acc = None
            for r in range(R):
                h = gi * R + r
                seg = cs[c][:, h:h + 1] - cs_t[h:h + 1, :]
                wgt = cb[c][gi] * jnp.exp(jnp.where(lower, seg, NEG))
                y_h = _dot(wgt, x_dt[c][:, groups[gi]])
                acc = y_h if r == 0 else jnp.where(head_in_group == r, y_h, acc)
            per_group.append(acc)
        y_intra.append(per_group)

    for c, rs in enumerate(chunks):
        ys = []
        for gi, gs in enumerate(groups):
            y_state = _dot(xbc_ref[rs, c_cols[gi]], st_ref[:, gs]) * ecs_e[c][:, gs]
            st_ref[:, gs] = st_ref[:, gs] * ecs_e[c][L - 1:L, gs] + bx[c][gi]
            ys.append(y_state + y_intra[c][gi])
        y = jnp.concatenate(ys, axis=1) + xs[c] * dskip_ref[...]
        y = y * _silu(z_ref[rs, :].astype(F32))
        normed = []
        for gs in groups:
            y_g = y[:, gs]
            normed.append(y_g * lax.rsqrt(jnp.mean(y_g * y_g, axis=-1, keepdims=True) + EPS))
        o_ref[rs, :] = (jnp.concatenate(normed, axis=1) * nw_ref[...]).astype(o_ref.dtype)


def _ssd_params(dt_bias, a_log, d_skip, norm_w):
    pad = lambda t: jnp.pad(t, ((0, 0), (0, LANE - t.shape[1])))[:, None, :]
    return (pad(dt_bias), pad(a_log), jnp.repeat(d_skip, SSD_P, axis=1)[:, None, :], norm_w[:, None, :])


def _ssd(z, xbc, dt, params, layer):
    B, S, _ = z.shape
    L = SSD_CHUNK
    tri = np.tril(np.ones((L, L), np.float32))
    expand = np.zeros((LANE, SSD_INNER), np.float32)
    for h in range(SSD_HEADS):
        expand[h, h * SSD_P:(h + 1) * SSD_P] = 1.0
    return pl.pallas_call(
        _ssd_kernel,
        grid=(B, S // T_SSD),
        in_specs=[pl.BlockSpec((None, T_SSD, SSD_INNER), lambda b, i: (b, i, 0)),
                  pl.BlockSpec((None, T_SSD, SSD_XBC), lambda b, i: (b, i, 0)),
                  pl.BlockSpec((None, T_SSD, LANE), lambda b, i: (b, i, 0)),
                  _layer_spec((1, LANE), layer), _layer_spec((1, LANE), layer),
                  _layer_spec((1, SSD_INNER), layer), _layer_spec((1, SSD_INNER), layer),
                  _const_spec((L, L)), _const_spec((LANE, SSD_INNER))],
        out_specs=pl.BlockSpec((None, T_SSD, SSD_INNER), lambda b, i: (b, i, 0)),
        out_shape=jax.ShapeDtypeStruct((B, S, SSD_INNER), BF16),
        scratch_shapes=[pltpu.VMEM((SSD_N, SSD_INNER), F32)],
        compiler_params=_cparams(("parallel", "arbitrary"), 48),
        name="ssd_mixer",
    )(z, xbc, dt, *params, jnp.asarray(tri, BF16), jnp.asarray(expand, BF16))


def _token_order(src_ref, stage_ref):
    dil, per, w = src_ref.shape
    for r in range(dil):
        plane = src_ref[r].astype(F32)
        for t in range(w // LANE):
            stage_ref[t, pl.ds(r, per, stride=dil), :] = plane[:, t * LANE:(t + 1) * LANE]
    return jnp.concatenate([stage_ref[t] for t in range(w // LANE)], axis=1)


def _tail_kernel(x_ref, g1_ref, gla_ref, d0_ref, d1_ref, d2_ref, l0_ref, l1_ref, l2_ref, gm_ref, ssd_ref,
                 gates_ref, wg_ref, wd_ref, wm_ref, ws_ref, wo_ref, exp_ref,
                 nw_ref, sc_ref, sh_ref, g2_ref, w1_ref, w2_ref, fw_ref, o_ref,
                 od1_ref, od2_ref, ol1_ref, ol2_ref, *, final_norm):
    D = D_MODEL
    outs = (d0_ref[...].astype(F32), _token_order(d1_ref, od1_ref), _token_order(d2_ref, od2_ref))
    stats = (l0_ref[...], _token_order(l1_ref, ol1_ref), _token_order(l2_ref, ol2_ref))
    dens = [pltpu.roll(s, LANE - DIL_HEADS, axis=1) for s in stats]
    m = jnp.maximum(jnp.maximum(stats[0], stats[1]), stats[2])
    es = [jnp.exp(s - m) for s in stats]
    total = es[0] * dens[0] + es[1] * dens[1] + es[2] * dens[2]
    is_head = lax.broadcasted_iota(jnp.int32, m.shape, 1) < DIL_HEADS
    expand = exp_ref[...]
    o_dil = None
    for e, o_p in zip(es, outs):
        term = _dot_sel_r(jnp.where(is_head, e / total, 0.0), expand) * o_p
        o_dil = term if o_dil is None else o_dil + term
    branches = ((gla_ref[...], wg_ref), (o_dil, wd_ref), (gm_ref[...], wm_ref), (ssd_ref[...], ws_ref))
    merged = None
    for i, (o_b, w_ref) in enumerate(branches):
        g = gates_ref[:, i * D:(i + 1) * D]
        gate = (jnp.tanh(g * jnp.asarray(0.5, g.dtype)) * jnp.asarray(0.5, g.dtype)
                + jnp.asarray(0.5, g.dtype)).astype(F32)
        term = gate * _dot(o_b, w_ref[...])
        merged = term if merged is None else merged + term
    x = x_ref[...] + g1_ref[...] * _dot(merged, wo_ref[...])

    h = _modulated_norm(x, nw_ref[...], sc_ref[...], sh_ref[...]).astype(BF16)
    acc = jnp.zeros(x.shape, F32)
    for c0 in range(0, MLP_HIDDEN, D_MODEL):
        a = jnp.maximum(jnp.dot(h, w1_ref[:, c0:c0 + D_MODEL], preferred_element_type=F32), 0.0)
        acc = acc + _dot(a * a, w2_ref[c0:c0 + D_MODEL, :])
    y = x + g2_ref[...] * acc
    if final_norm:
        y = y * lax.rsqrt(jnp.mean(y * y, axis=-1, keepdims=True) + EPS) * fw_ref[...]
    o_ref[...] = y


def _mod_spec(layer, which):
    return pl.BlockSpec((None, None, None, 1, D_MODEL), lambda b, i: (layer, b, which, 0, 0))


def _tail(x, mod, o_gla, o_dil, lse_dil, o_gm, o_ssd, gates, w_gla, w_dil, w_gm, w_ssd, w_out,
          nw2, w1, w2, fw, layer, final_norm):
    B, S, D = x.shape
    TM = TM_MLP
    tok = lambda w: pl.BlockSpec((None, TM, w), lambda b, i: (b, i, 0))

    def plane(w, dil):
        if dil == 1:
            return pl.BlockSpec((None, None, TM, w), lambda b, i: (b, 0, i, 0))
        return pl.BlockSpec((None, dil, TM // dil, w), lambda b, i: (b, 0, i, 0))

    dils = [d for _, d in DIL_PATTERNS]
    assert dils[0] == 1 and len(dils) == 3
    expand = np.zeros((LANE, DIL_OUT), np.float32)
    for h in range(DIL_HEADS):
        expand[h, h * DIL_DIM:(h + 1) * DIL_DIM] = 1.0
    return pl.pallas_call(
        functools.partial(_tail_kernel, final_norm=final_norm),
        grid=(B, S // TM),
        in_specs=[tok(D), _mod_spec(layer, 2), tok(GLA_V)]
                 + [plane(DIL_OUT, d) for d in dils] + [plane(LANE, d) for d in dils]
                 + [tok(GM_W), tok(SSD_INNER), tok(N_BRANCH * D),
                    _layer_spec((GLA_V, D), layer), _layer_spec((DIL_OUT, D), layer),
                    _layer_spec((GM_W, D), layer), _layer_spec((SSD_INNER, D), layer),
                    _layer_spec((D, D), layer), _const_spec((LANE, DIL_OUT)),
                    _layer_spec((1, D), layer), _mod_spec(layer, 4), _mod_spec(layer, 3),
                    _mod_spec(layer, 5), _layer_spec((D, MLP_HIDDEN), layer),
                    _layer_spec((MLP_HIDDEN, D), layer), _const_spec((1, D))],
        out_specs=tok(D),
        out_shape=jax.ShapeDtypeStruct((B, S, D), F32),
        scratch_shapes=[pltpu.VMEM((DIL_OUT // LANE, TM, LANE), F32), pltpu.VMEM((DIL_OUT // LANE, TM, LANE), F32),
                        pltpu.VMEM((1, TM, LANE), F32), pltpu.VMEM((1, TM, LANE), F32)],
        compiler_params=_cparams(("parallel", "parallel"), 58),
        name="tail_final" if final_norm else "tail",
    )(x, mod, o_gla, *o_dil, *lse_dil, o_gm, o_ssd, gates, w_gla, w_dil, w_gm, w_ssd, w_out,
      jnp.asarray(expand, BF16), nw2, mod, mod, mod, w1, w2, fw)


def _repack_plan():
    whole = lambda a, b: [(_SP[a], _SP[b] - _SP[a], _SP[b] - _SP[a])]
    plan = {"gla": whole(0, 4), "glr": [(_SP[4], LANE, GLA_RANK)], "gm": whole(8, 10),
            "ssd_z": whole(10, 11), "ssd_xbc": whole(11, 12), "dt": [(_SP[12], LANE, SSD_HEADS)],
            "gates": whole(13, 14)}
    for p in range(len(DIL_PATTERNS)):
        plan[f"dil{p}"] = [(_SP[t] + p * DIL_OUT, DIL_OUT, DIL_OUT) for t in (5, 6, 7)]
    plan = [plan[name] for name in PROJ_NAMES]
    assert [sum(w for _, w, _ in pieces) for pieces in plan] == [o[1] for o in PROJ_OUTS]
    return plan


def _repack_kernel(wt_ref, o_ref, *, valid):
    v = wt_ref[0]
    if valid < v.shape[0]:
        v = jnp.where(lax.broadcasted_iota(jnp.int32, v.shape, 0) < valid, v, 0.0)
    o_ref[...] = v.T.astype(o_ref.dtype)


def _pack_w_in(w_in):
    L, D, W = w_in.shape
    w_t = jnp.swapaxes(w_in, 1, 2)
    outs = []
    for (name, width, _, _), pieces in zip(PROJ_OUTS, _repack_plan()):
        start, piece_w, valid = pieces[0]
        if len(pieces) > 1:
            tile, stride = piece_w, pieces[1][0] - start
            assert all(p == (start + i * stride, tile, tile) for i, p in enumerate(pieces))
        else:
            tile = next(t for t in (512, 256, LANE) if piece_w % t == 0)
            stride = tile
        assert start % 8 == 0 and stride % 8 == 0 and width % tile == 0
        outs.append(pl.pallas_call(
            functools.partial(_repack_kernel, valid=min(valid, tile)),
            grid=(L, width // tile),
            in_specs=[pl.BlockSpec((pl.Element(1), pl.Element(tile), pl.Element(D)),
                                   lambda l, t, start=start, stride=stride: (
                                       l, pl.multiple_of(start + t * stride, 8), 0))],
            out_specs=pl.BlockSpec((None, D, tile), lambda l, t: (l, 0, t)),
            out_shape=jax.ShapeDtypeStruct((L, D, width), BF16),
            compiler_params=_cparams(("parallel", "parallel"), 32),
            name=f"repack_{name}",
        )(w_t))
    return outs


def kernel(x, c, w_ada, b_ada, norm1_w, norm2_w, w_in, gla_w_a2, gla_b_a, gla_norm_w, gm_ln_w, gm_ln_b, gm_ws, gm_bs, ssd_conv_w, ssd_conv_b, ssd_dt_bias, ssd_a_log, ssd_d, ssd_norm_w, w_br_gla, w_br_dil, w_br_gm, w_br_ssd, w_out, w_mlp1, w_mlp2, final_norm_w):
    mod = _ada_mod(c, w_ada, b_ada)
    w_parts = _pack_w_in(w_in)
    wa = jnp.pad(gla_w_a2, ((0, 0), (0, LANE - GLA_RANK), (0, 0))).astype(BF16)
    w_gla, w_dil, w_gm, w_ssd, w_o, w1, w2 = (t.astype(BF16) for t in (
        w_br_gla, w_br_dil, w_br_gm, w_br_ssd, w_out, w_mlp1, w_mlp2))
    rows = lambda t: t[:, None, :]
    n1, n2 = rows(norm1_w), rows(norm2_w)
    gla_ba, gla_nw = rows(gla_b_a), rows(jnp.tile(gla_norm_w, (1, GLA_HEADS)))
    gm_lnw, gm_lnb = rows(gm_ln_w), rows(gm_ln_b)
    gm_w, gm_b = _gmlp_params(gm_ws, gm_bs)
    ssd_params = _ssd_params(ssd_dt_bias, ssd_a_log, ssd_d, ssd_norm_w)
    conv_b = rows(ssd_conv_b)
    fw = final_norm_w.reshape(1, -1)
    for l in range(DEPTH):
        p = dict(zip(PROJ_NAMES, _inproj(x, n1, mod, ssd_conv_w, conv_b, w_parts, l)))
        o_gla = _gla(p["gla"], p["glr"], wa, gla_ba, gla_nw, l)
        o_dil, lse_dil = _dilated([p["dil0"], p["dil1"], p["dil2"]])
        o_gm = _gmlp(p["gm"], gm_lnw, gm_lnb, gm_w, gm_b, l)
        o_ssd = _ssd(p["ssd_z"], p["ssd_xbc"], p["dt"], ssd_params, l)
        x = _tail(x, mod, o_gla, o_dil, lse_dil, o_gm, o_ssd, p["gates"], w_gla, w_dil, w_gm, w_ssd, w_o,
                  n2, w1, w2, fw, l, final_norm=(l == DEPTH - 1))
    return x
```

```python
import functools
from typing import Callable, NamedTuple

import numpy as np
import jax
import jax.numpy as jnp
from jax import lax
from jax.experimental import pallas as pl
from jax.experimental.pallas import tpu as pltpu

F32 = jnp.float32
BF16 = jnp.bfloat16

D_MODEL = 1024
DEPTH = 4
EPS = 1e-6

GLA_HEADS, GLA_DK, GLA_DV, GLA_RANK, GLA_CHUNK = 4, 32, 64, 16, 64
GLA_GATE_NORM = 16.0
DIL_PATTERNS = ((128, 1), (512, 4), (2048, 16))
DIL_HEADS, DIL_DIM, DIL_STEPS = 4, 64, 128
GM_GROUPS, GM_DIM, GM_CHUNK = 4, 64, 128
SSD_HEADS, SSD_P, SSD_GROUPS, SSD_N, SSD_CONV, SSD_CHUNK = 8, 64, 2, 128, 4, 128
N_BRANCH = 4
MLP_HIDDEN = 4 * D_MODEL

GLA_QK = GLA_HEADS * GLA_DK
GLA_V = GLA_HEADS * GLA_DV
DIL_W = len(DIL_PATTERNS) * DIL_HEADS * DIL_DIM
DIL_OUT = DIL_HEADS * DIL_DIM
GM_W = GM_GROUPS * GM_DIM
SSD_INNER = SSD_HEADS * SSD_P
SSD_BC = SSD_GROUPS * SSD_N
SSD_XBC = SSD_INNER + 2 * SSD_BC
IN_SPLITS = (GLA_QK, GLA_QK, GLA_V, GLA_V, GLA_RANK, DIL_W, DIL_W, DIL_W, GM_W, GM_W,
             SSD_INNER, SSD_XBC, SSD_HEADS, N_BRANCH * D_MODEL)
_SP = tuple(int(v) for v in np.cumsum((0,) + IN_SPLITS))

LANE = 128
NEG = -1e30

PROJ_OUTS = (("gla", 2 * GLA_QK + 2 * GLA_V, BF16, 1), ("glr", LANE, F32, 1),
             ("dil0", 3 * DIL_OUT, BF16, DIL_PATTERNS[0][1]), ("dil1", 3 * DIL_OUT, BF16, DIL_PATTERNS[1][1]),
             ("dil2", 3 * DIL_OUT, BF16, DIL_PATTERNS[2][1]),
             ("gm", 2 * GM_W, BF16, 1), ("ssd_z", SSD_INNER, BF16, 1), ("ssd_xbc", SSD_XBC, BF16, 1),
             ("dt", LANE, F32, 1), ("gates", N_BRANCH * D_MODEL, BF16, 1))
PROJ_NAMES = tuple(o[0] for o in PROJ_OUTS)

TM_PROJ = 512
TM_MLP = 512
T_MIX = 1024
GLA_CUMSUM_ROWS = 256
COL_CHUNK = 512
DIL_QBLOCKS = 8
DIL_LOOKAHEAD = 2


V7X_VMEM_BYTES = 64 * 1024 * 1024


def _cparams(sem, vmem_mib):
    assert vmem_mib * 1024 * 1024 <= V7X_VMEM_BYTES
    return pltpu.CompilerParams(dimension_semantics=sem, vmem_limit_bytes=V7X_VMEM_BYTES)


def _const_spec(shape):
    nd = len(shape)
    return pl.BlockSpec(shape, lambda *_: (0,) * nd, pipeline_mode=pl.Buffered(1))


def _layer_spec(shape, layer):
    nd = len(shape)
    return pl.BlockSpec((None,) + tuple(shape), lambda *_: (layer,) + (0,) * nd,
                        pipeline_mode=pl.Buffered(1))


class _Part(NamedTuple):
    kernel: Callable
    in_specs: list
    out_specs: list
    out_shape: list
    scratch: list
    args: list


def _fused_call(parts, grid, sem, vmem_mib, name):
    n_in = [len(p.in_specs) for p in parts]
    n_out = [len(p.out_specs) for p in parts]
    n_scr = [len(p.scratch) for p in parts]

    def body(*refs):
        ins, outs, scr = refs[:sum(n_in)], refs[sum(n_in):sum(n_in) + sum(n_out)], refs[sum(n_in) + sum(n_out):]
        for k, p in enumerate(parts):
            take = lambda group, counts: group[sum(counts[:k]):sum(counts[:k + 1])]
            p.kernel(*take(ins, n_in), *take(outs, n_out), *take(scr, n_scr))

    flat = pl.pallas_call(
        body, grid=grid,
        in_specs=[s for p in parts for s in p.in_specs],
        out_specs=[s for p in parts for s in p.out_specs],
        out_shape=[s for p in parts for s in p.out_shape],
        scratch_shapes=[s for p in parts for s in p.scratch],
        compiler_params=_cparams(sem, vmem_mib), name=name,
    )(*[a for p in parts for a in p.args])
    return [flat[sum(n_out[:k]):sum(n_out[:k + 1])] for k in range(len(parts))]


def _dot(a, b):
    return jnp.dot(a.astype(BF16), b.astype(BF16), preferred_element_type=F32)


def _dot_nt(a, b):
    return lax.dot_general(a.astype(BF16), b.astype(BF16), (((1,), (1,)), ((), ())),
                           preferred_element_type=F32)


def _split(a):
    hi = a.astype(BF16)
    lo = (a - hi.astype(F32)).astype(BF16)
    return hi, lo


def _dot_sel_r(a, sel):
    hi, lo = _split(a)
    return (jnp.dot(hi, sel, preferred_element_type=F32) + jnp.dot(lo, sel, preferred_element_type=F32))


def _dot_sel_l(sel, a):
    hi, lo = _split(a)
    return (jnp.dot(sel, hi, preferred_element_type=F32) + jnp.dot(sel, lo, preferred_element_type=F32))


def _sigmoid(x):
    return 1.0 / (1.0 + jnp.exp(-x))


def _silu(x):
    return x * _sigmoid(x)


def _gelu(x):
    return 0.5 * x * (1.0 + lax.erf(x * (2.0 ** -0.5)))


def _softplus(x):
    return jnp.maximum(x, 0.0) + jnp.log1p(jnp.exp(-jnp.abs(x)))


def _modulated_norm(x, nw, sc, sh):
    y = x * lax.rsqrt(jnp.mean(x * x, axis=-1, keepdims=True) + EPS)
    return (y * nw) * (1.0 + sc) + sh


def _ada_kernel(c_ref, w_ref, b_ref, o_ref):
    o_ref[...] = _dot(_silu(c_ref[...]), w_ref[...]) + b_ref[...]


def _ada_mod(c, w_ada, b_ada):
    B, D = c.shape
    rows = 16
    c_pad = jnp.pad(c, ((0, rows - B), (0, 0)))
    n_col = w_ada.shape[-1] // D
    out = pl.pallas_call(
        _ada_kernel,
        grid=(DEPTH, n_col),
        in_specs=[pl.BlockSpec((rows, D), lambda l, j: (0, 0)),
                  pl.BlockSpec((None, D, D), lambda l, j: (l, 0, j)),
                  pl.BlockSpec((None, 1, D), lambda l, j: (l, 0, j))],
        out_specs=pl.BlockSpec((None, rows, D), lambda l, j: (l, 0, j)),
        out_shape=jax.ShapeDtypeStruct((DEPTH, rows, n_col * D), F32),
        compiler_params=_cparams(("arbitrary", "arbitrary"), 32),
        name="ada_mod",
    )(c_pad, w_ada, b_ada.reshape(DEPTH, 1, -1))
    return out[:, :B].reshape(DEPTH, B, n_col, 1, D)


def _inproj_kernel(x_ref, nw_ref, sc_ref, sh_ref, cw_ref, cb_ref, *refs):
    n_out = len(PROJ_OUTS)
    w_refs, o_refs, (stage_ref, carry_ref, win_ref) = refs[:n_out], refs[n_out:2 * n_out], refs[2 * n_out:]
    TM = x_ref.shape[0]
    KEEP = carry_ref.shape[0]

    @pl.when(pl.program_id(1) == 0)
    def _():
        carry_ref[...] = jnp.zeros_like(carry_ref)

    h = _modulated_norm(x_ref[...], nw_ref[...], sc_ref[...], sh_ref[...]).astype(BF16)
    for w_ref, o_ref, (name, width, dtype, dil) in zip(w_refs, o_refs, PROJ_OUTS):
        for c0 in range(0, width, COL_CHUNK):
            c1 = min(c0 + COL_CHUNK, width)
            res = jnp.dot(h, w_ref[:, c0:c1], preferred_element_type=F32)
            if name == "ssd_xbc":
                win_ref[0:KEEP, :] = carry_ref[:, c0:c1]
                win_ref[KEEP:, :] = res
                carry_ref[:, c0:c1] = res[TM - KEEP:]
                conv = cb_ref[:, c0:c1]
                for j in range(SSD_CONV):
                    off = KEEP - (SSD_CONV - 1) + j
                    conv = conv + cw_ref[j:j + 1, c0:c1] * win_ref[off:off + TM, :]
                o_ref[:, c0:c1] = _silu(conv).astype(dtype)
            elif dil == 1:
                o_ref[:, c0:c1] = res.astype(dtype)
            else:
                for t in range((c1 - c0) // LANE):
                    stage_ref[c0 // LANE + t] = res[:, t * LANE:(t + 1) * LANE]
        if dil > 1:
            for r in range(dil):
                for t in range(width // LANE):
                    o_ref[r, :, t * LANE:(t + 1) * LANE] = stage_ref[
                        t, pl.ds(r, TM // dil, stride=dil), :].astype(dtype)


def _inproj(x, nw, mod, conv_w, conv_b, w_parts, layer):
    B, S, D = x.shape
    TM = TM_PROJ
    out_specs, out_shape = [], []
    for _, w, dt, dil in PROJ_OUTS:
        if dil == 1:
            out_specs.append(pl.BlockSpec((None, TM, w), lambda b, i: (b, i, 0)))
            out_shape.append(jax.ShapeDtypeStruct((B, S, w), dt))
        else:
            out_specs.append(pl.BlockSpec((None, dil, TM // dil, w), lambda b, i: (b, 0, i, 0)))
            out_shape.append(jax.ShapeDtypeStruct((B, dil, S // dil, w), dt))
    return pl.pallas_call(
        _inproj_kernel,
        grid=(B, S // TM),
        in_specs=[pl.BlockSpec((None, TM, D), lambda b, i: (b, i, 0)),
                  _layer_spec((1, D), layer), _mod_spec(layer, 1), _mod_spec(layer, 0),
                  _layer_spec((SSD_CONV, SSD_XBC), layer), _layer_spec((1, SSD_XBC), layer)]
                 + [_layer_spec((D, o[1]), layer) for o in PROJ_OUTS],
        out_specs=out_specs,
        out_shape=out_shape,
        scratch_shapes=[pltpu.VMEM((3 * DIL_OUT // LANE, TM, LANE), F32),
                        pltpu.VMEM((8, SSD_XBC), F32),
                        pltpu.VMEM((TM + 8, COL_CHUNK), F32)],
        compiler_params=_cparams(("parallel", "arbitrary"), 58),
        name="inproj",
    )(x, nw, mod, mod, conv_w, conv_b, *w_parts)


def _gla_kernel(gla_ref, glr_ref, wa_ref, ba_ref, nw_ref, tri_ref, ones_ref, o_ref, st_ref):
    C, H, K, V = GLA_CHUNK, GLA_HEADS, GLA_DK, GLA_DV
    T = gla_ref.shape[0]

    @pl.when(pl.program_id(1) == 0)
    def _():
        st_ref[...] = jnp.zeros_like(st_ref)

    blk = gla_ref[...]
    q = blk[:, 0:GLA_QK].astype(F32) * (K ** -0.5)
    k = blk[:, GLA_QK:2 * GLA_QK].astype(F32)
    v = blk[:, 2 * GLA_QK:2 * GLA_QK + GLA_V]
    g = blk[:, 2 * GLA_QK + GLA_V:].astype(F32)
    pre = _dot(glr_ref[...], wa_ref[...]) + ba_ref[...]
    log_a = -_softplus(-pre) * (1.0 / GLA_GATE_NORM)
    tri = tri_ref[...]
    TB = tri.shape[0]
    b_all = jnp.concatenate([_dot_sel_l(tri, log_a[i * TB:(i + 1) * TB]) for i in range(T // TB)],
                            axis=0)

    k_head = lax.broadcasted_iota(jnp.int32, (C, GLA_QK), 1) // K
    v_head = lax.broadcasted_iota(jnp.int32, (C, GLA_V), 1) // V
    row = lax.broadcasted_iota(jnp.int32, (H * C, C), 0)
    causal = (row % C) >= lax.broadcasted_iota(jnp.int32, (H * C, C), 1)
    st_mask = (lax.broadcasted_iota(jnp.int32, (GLA_V, GLA_QK), 0) // V
               == lax.broadcasted_iota(jnp.int32, (GLA_V, GLA_QK), 1) // K)

    n_chunk = T // C
    chunks = [slice(c * C, (c + 1) * C) for c in range(n_chunk)]
    b_c = [b_all[rs] for rs in chunks]
    q_t = [q[rs] * jnp.exp(b) for rs, b in zip(chunks, b_c)]
    k_t = [k[rs] * jnp.exp(-b) for rs, b in zip(chunks, b_c)]
    k_dec = [k[rs] * jnp.exp(b[C - 1:C] - b) for rs, b in zip(chunks, b_c)]
    att = []
    for c in range(n_chunk):
        q_stack = jnp.concatenate([jnp.where(k_head == h, q_t[c], 0.0) for h in range(H)], axis=0)
        att.append(jnp.where(causal, _dot_nt(q_stack, k_t[c]), 0.0))
    kv_t = [jnp.where(st_mask, _dot(v[rs].astype(F32).T, k_dec[c]), 0.0)
            for c, rs in enumerate(chunks)]
    o_all = [_dot(att[c], v[rs]) for c, rs in enumerate(chunks)]
    st = st_ref[...]
    outs = []
    for c in range(n_chunk):
        o = _dot_nt(q_t[c], st)
        for h in range(H):
            o = o + jnp.where(v_head == h, o_all[c][h * C:(h + 1) * C], 0.0)
        st = st * jnp.exp(b_c[c][C - 1:C]) + kv_t[c]
        outs.append(o)
    st_ref[...] = st

    o = jnp.concatenate(outs, axis=0)
    ms = _dot_sel_r(o * o, ones_ref[...]) * (1.0 / V)
    o = o * lax.rsqrt(ms + EPS) * nw_ref[...]
    o_ref[...] = (o * _silu(g)).astype(o_ref.dtype)


def _gla(gla, glr, wa, ba, nw, layer):
    B, S, W = gla.shape
    T, C, TB = T_MIX, GLA_CHUNK, GLA_CUMSUM_ROWS
    idx = np.arange(TB)
    tri = ((idx[:, None] // C == idx[None, :] // C) & (idx[:, None] >= idx[None, :]))
    hv = np.arange(GLA_V) // GLA_DV
    ones = hv[:, None] == hv[None, :]
    return _Part(
        _gla_kernel,
        in_specs=[pl.BlockSpec((None, T, W), lambda b, i: (b, i, 0)),
                  pl.BlockSpec((None, T, LANE), lambda b, i: (b, i, 0)),
                  _layer_spec((LANE, GLA_QK), layer), _layer_spec((1, GLA_QK), layer),
                  _layer_spec((1, GLA_V), layer), _const_spec((TB, TB)), _const_spec((GLA_V, GLA_V))],
        out_specs=[pl.BlockSpec((None, T, GLA_V), lambda b, i: (b, i, 0))],
        out_shape=[jax.ShapeDtypeStruct((B, S, GLA_V), BF16)],
        scratch=[pltpu.VMEM((GLA_V, GLA_QK), F32)],
        args=[gla, glr, wa, ba, nw, jnp.asarray(tri, BF16), jnp.asarray(ones, BF16)])


def _dil_kernel(*refs, row_steps):
    n_pat = len(row_steps)
    for p in range(n_pat):
        first = jnp.minimum(pl.program_id(1) % row_steps[p], 1)
        _dil_planes(*refs[6 * p:6 * p + 6], *refs[6 * n_pat + 2 * p:6 * n_pat + 2 * p + 2], first)


def _dil_planes(q_ref, kp_ref, k_ref, vp_ref, v_ref, bias_ref, o_ref, stat_ref, first):
    H, E, BLK = DIL_HEADS, DIL_DIM, DIL_STEPS
    n_res, n_blk = q_ref.shape[0], q_ref.shape[1] // BLK
    head = lax.broadcasted_iota(jnp.int32, (BLK, H * E), 1) // E
    lane = lax.broadcasted_iota(jnp.int32, (BLK, LANE), 1)
    kcat = [jnp.concatenate([kp_ref[r], k_ref[r]], axis=0) for r in range(n_res)]
    vcat = [jnp.concatenate([vp_ref[r], v_ref[r]], axis=0) for r in range(n_res)]
    v_head = lax.broadcasted_iota(jnp.int32, vcat[0].shape, 1) // E
    v_masked = [[jnp.where(v_head == h, vc, jnp.zeros_like(vc)) for h in range(H)] for vc in vcat]
    blocks =[(r, i) for r in range(n_res) for i in range(n_blk)]

    def scores(r, i):
        q = q_ref[r, i * BLK:(i + 1) * BLK, :] * jnp.asarray(E ** -0.5, BF16)
        q_stack = jnp.concatenate([jnp.where(head == h, q, jnp.zeros_like(q)) for h in range(H)], axis=0)
        bias = bias_ref[first] if i == 0 else bias_ref[1]
        return _dot_nt(q_stack, kcat[r][i * BLK:(i + 2) * BLK]) + bias

    pending = [scores(*blk) for blk in blocks[:DIL_LOOKAHEAD]]
    for n, (r, i) in enumerate(blocks):
        s = pending.pop(0)
        if n + DIL_LOOKAHEAD < len(blocks):
            pending.append(scores(*blocks[n + DIL_LOOKAHEAD]))
        m = jnp.max(s, axis=-1, keepdims=True)
        p = jnp.exp(s - m)
        l = jnp.sum(p, axis=-1, keepdims=True)
        p = p.astype(BF16)
        p_heads = jnp.concatenate([p[h * BLK:(h + 1) * BLK] for h in range(H)], axis=1)
        v_heads = jnp.concatenate([vm[i * BLK:(i + 2) * BLK] for vm in v_masked[r]], axis=0)
        o = jnp.dot(p_heads, v_heads, preferred_element_type=F32)
        stat = jnp.zeros((BLK, LANE), F32)
        for h in range(H):
            hs = slice(h * BLK, (h + 1) * BLK)
            stat = jnp.where(lane == h, m[hs], jnp.where(lane == H + h, l[hs], stat))
        rows = slice(i * BLK, (i + 1) * BLK)
        o_ref[r, rows, :] = o.astype(o_ref.dtype)
        stat_ref[r, rows, :] = stat


def _dil_bias(window, dil, slopes):
    BLK = DIL_STEPS
    steps = np.arange(BLK)[:, None] + BLK - np.arange(2 * BLK)[None, :]
    ok = (steps >= 0) & (steps <= window // dil)
    bias = -slopes[:, None, None] * (steps * dil)[None].astype(np.float32)
    full = np.where(ok[None], bias, NEG).astype(np.float32)
    first = np.where((np.arange(2 * BLK) >= BLK)[None, None, :], full, NEG)
    return np.stack([first.reshape(-1, 2 * BLK), full.reshape(-1, 2 * BLK)]).astype(np.float32)


def _dil_pattern(qkv, window, dil, slopes):
    B, _, n, _ = qkv.shape
    HE, BLK = DIL_OUT, DIL_STEPS
    n_blk = min(DIL_QBLOCKS, n // BLK)
    n_res = min(dil, DIL_QBLOCKS // n_blk)
    rows = n_blk * BLK
    assert n % rows == 0 and dil % n_res == 0
    row_steps = n // rows
    at = lambda j: (j // row_steps, j % row_steps)
    cur = lambda t: pl.BlockSpec((None, n_res, rows, HE), lambda b, j: (b, *at(j), t))
    prev = lambda t: pl.BlockSpec((None, n_res, BLK, HE),
                                  lambda b, j: (b, at(j)[0], jnp.maximum(at(j)[1] * n_blk - 1, 0), t))
    in_specs = [cur(0), prev(1), cur(1), prev(2), cur(2), _const_spec((2, DIL_HEADS * BLK, 2 * BLK))]
    out_specs = [pl.BlockSpec((None, n_res, rows, HE), lambda b, j: (b, *at(j), 0)),
                 pl.BlockSpec((None, n_res, rows, LANE), lambda b, j: (b, *at(j), 0))]
    out_shape = [jax.ShapeDtypeStruct((B, dil, n, HE), BF16), jax.ShapeDtypeStruct((B, dil, n, LANE), F32)]
    args = [qkv, qkv, qkv, qkv, qkv, jnp.asarray(_dil_bias(window, dil, slopes))]
    return in_specs, out_specs, out_shape, args, row_steps, (dil // n_res) * row_steps


def _dilated(qkvs):
    n_pat = len(DIL_PATTERNS)
    n_h = n_pat * DIL_HEADS
    slopes = (2.0 ** (-8.0 * np.arange(1, n_h + 1) / n_h)).astype(np.float32).reshape(n_pat, DIL_HEADS)
    in_specs, out_specs, out_shape, args, row_steps, steps = [], [], [], [], [], set()
    for p, (w, d) in enumerate(DIL_PATTERNS):
        qkv = qkvs[p] if qkvs[p].ndim == 4 else qkvs[p][:, None]
        assert qkv.shape[1] == d
        i_s, o_s, o_sh, a, rs, n_steps = _dil_pattern(qkv, w, d, slopes[p])
        in_specs += i_s; out_specs += o_s; out_shape += o_sh; args += a
        row_steps.append(rs); steps.add(n_steps)
    assert len(steps) == 1, "patterns must split into equally many grid steps"
    assert steps.pop() == qkvs[0].shape[1] // T_MIX
    return _Part(functools.partial(_dil_kernel, row_steps=tuple(row_steps)),
                 in_specs=in_specs, out_specs=out_specs, out_shape=out_shape, scratch=[], args=args)


def _gm_kernel(gm_ref, lnw_ref, lnb_ref, w_ref, bs_ref, o_ref):
    G, E, C = GM_GROUPS, GM_DIM, GM_CHUNK
    T = gm_ref.shape[0]
    blk = gm_ref[...].astype(F32)
    u = _gelu(blk[:, :GM_W])
    v = _gelu(blk[:, GM_W:])
    mu = jnp.mean(v, axis=-1, keepdims=True)
    var = jnp.mean(jnp.square(v - mu), axis=-1, keepdims=True)
    v = (v - mu) * lax.rsqrt(var + EPS) * lnw_ref[...] + lnb_ref[...]
    group = lax.broadcasted_iota(jnp.int32, (C, GM_W), 1) // E
    w = w_ref[...]
    bs = bs_ref[...]
    for c in range(T // C):
        rs = slice(c * C, (c + 1) * C)
        r = _dot(w, v[rs])
        sv = bs
        for gi in range(G):
            sv = sv + jnp.where(group == gi, r[gi * C:(gi + 1) * C], 0.0)
        o_ref[rs, :] = (u[rs] * sv).astype(o_ref.dtype)


def _gmlp_params(ws, bs):
    C = GM_CHUNK
    tril = jnp.tril(jnp.ones((C, C), bool))
    w_stack = jnp.where(tril, ws, 0.0).reshape(-1, GM_GROUPS * C, C).astype(BF16)
    bs_exp = jnp.repeat(jnp.swapaxes(bs, 1, 2), GM_DIM, axis=2)
    return w_stack, bs_exp


def _gmlp(gm, lnw, lnb, w_stack, bs_exp, layer):
    B, S, W = gm.shape
    C = GM_CHUNK
    return _Part(
        _gm_kernel,
        in_specs=[pl.BlockSpec((None, T_MIX, W), lambda b, i: (b, i, 0)),
                  _layer_spec((1, GM_W), layer), _layer_spec((1, GM_W), layer),
                  _layer_spec((GM_GROUPS * C, C), layer), _layer_spec((C, GM_W), layer)],
        out_specs=[pl.BlockSpec((None, T_MIX, GM_W), lambda b, i: (b, i, 0))],
        out_shape=[jax.ShapeDtypeStruct((B, S, GM_W), BF16)],
        scratch=[], args=[gm, lnw, lnb, w_stack, bs_exp])


def _ssd_kernel(z_ref, xbc_ref, dt_ref, dtb_ref, alog_ref, dskip_ref, nw_ref, tri_ref, exp_ref,
                o_ref, st_ref):
    H, P, G, N, L = SSD_HEADS, SSD_P, SSD_GROUPS, SSD_N, SSD_CHUNK
    R = H // G
    GW = R * P
    T = z_ref.shape[0]

    @pl.when(pl.program_id(1) == 0)
    def _():
        st_ref[...] = jnp.zeros_like(st_ref)

    dt_all = _softplus(dt_ref[...] + dtb_ref[...])
    da_all = dt_all * (-jnp.exp(alog_ref[...]))
    tri = tri_ref[...]
    expand = exp_ref[...]
    lower = lax.broadcasted_iota(jnp.int32, (L, L), 0) >= lax.broadcasted_iota(jnp.int32, (L, L), 1)
    head_in_group = lax.broadcasted_iota(jnp.int32, (L, GW), 1) // P

    chunks = [slice(c * L, (c + 1) * L) for c in range(T // L)]
    groups = [slice(gi * GW, (gi + 1) * GW) for gi in range(G)]
    b_cols = [slice(SSD_INNER + gi * N, SSD_INNER + (gi + 1) * N) for gi in range(G)]
    c_cols = [slice(SSD_INNER + SSD_BC + gi * N, SSD_INNER + SSD_BC + (gi + 1) * N) for gi in range(G)]
    xs = [xbc_ref[rs, :SSD_INNER].astype(F32) for rs in chunks]
    cs = [_dot_sel_l(tri, da_all[rs]) for rs in chunks]
    dt_e = [_dot(dt_all[rs], expand) for rs in chunks]
    dec_e = [_dot(jnp.exp(cs_c[L - 1:L] - cs_c) * dt_all[rs], expand) for rs, cs_c in zip(chunks, cs)]
    ecs_e = [_dot_sel_r(jnp.exp(cs_c), expand) for cs_c in cs]
    x_dt = [(x * e).astype(BF16) for x, e in zip(xs, dt_e)]
    x_dec = [(x * e).astype(BF16) for x, e in zip(xs, dec_e)]
    cb = [[_dot_nt(xbc_ref[rs, c_cols[gi]], xbc_ref[rs, b_cols[gi]]) for gi in range(G)]
          for rs in chunks]
    bx = [[_dot(xbc_ref[rs, b_cols[gi]].astype(F32).T, x_dec[c][:, groups[gi]]) for gi in range(G)]
          for c, rs in enumerate(chunks)]
    y_intra = []
    for c in range(len(chunks)):
        cs_t = cs[c].T
        per_group = []
        for gi in range(G):
            acc = None
            for r in range(R):
                h = gi * R + r
                seg = cs[c][:, h:h + 1] - cs_t[h:h + 1, :]
                wgt = cb[c][gi] * jnp.exp(jnp.where(lower, seg, NEG))
                y_h = _dot(wgt, x_dt[c][:, groups[gi]])
                acc = y_h if r == 0 else jnp.where(head_in_group == r, y_h, acc)
            per_group.append(acc)
        y_intra.append(per_group)

    for c, rs in enumerate(chunks):
        ys = []
        for gi, gs in enumerate(groups):
            y_state = _dot(xbc_ref[rs, c_cols[gi]], st_ref[:, gs]) * ecs_e[c][:, gs]
            st_ref[:, gs] = st_ref[:, gs] * ecs_e[c][L - 1:L, gs] + bx[c][gi]
            ys.append(y_state + y_intra[c][gi])
        y = jnp.concatenate(ys, axis=1) + xs[c] * dskip_ref[...]
        y = y * _silu(z_ref[rs, :].astype(F32))
        normed = []
        for gs in groups:
            y_g = y[:, gs]
            normed.append(y_g * lax.rsqrt(jnp.mean(y_g * y_g, axis=-1, keepdims=True) + EPS))
        o_ref[rs, :] = (jnp.concatenate(normed, axis=1) * nw_ref[...]).astype(o_ref.dtype)


def _ssd_params(dt_bias, a_log, d_skip, norm_w):
    pad = lambda t: jnp.pad(t, ((0, 0), (0, LANE - t.shape[1])))[:, None, :]
    return (pad(dt_bias), pad(a_log), jnp.repeat(d_skip, SSD_P, axis=1)[:, None, :], norm_w[:, None, :])


def _ssd(z, xbc, dt, params, layer):
    B, S, _ = z.shape
    L = SSD_CHUNK
    tri = np.tril(np.ones((L, L), np.float32))
    expand = np.zeros((LANE, SSD_INNER), np.float32)
    for h in range(SSD_HEADS):
        expand[h, h * SSD_P:(h + 1) * SSD_P] = 1.0
    return _Part(
        _ssd_kernel,
        in_specs=[pl.BlockSpec((None, T_MIX, SSD_INNER), lambda b, i: (b, i, 0)),
                  pl.BlockSpec((None, T_MIX, SSD_XBC), lambda b, i: (b, i, 0)),
                  pl.BlockSpec((None, T_MIX, LANE), lambda b, i: (b, i, 0)),
                  _layer_spec((1, LANE), layer), _layer_spec((1, LANE), layer),
                  _layer_spec((1, SSD_INNER), layer), _layer_spec((1, SSD_INNER), layer),
                  _const_spec((L, L)), _const_spec((LANE, SSD_INNER))],
        out_specs=[pl.BlockSpec((None, T_MIX, SSD_INNER), lambda b, i: (b, i, 0))],
        out_shape=[jax.ShapeDtypeStruct((B, S, SSD_INNER), BF16)],
        scratch=[pltpu.VMEM((SSD_N, SSD_INNER), F32)],
        args=[z, xbc, dt, *params, jnp.asarray(tri, BF16), jnp.asarray(expand, BF16)])


def _token_order(src_ref, stage_ref):
    dil, per, w = src_ref.shape
    for r in range(dil):
        plane = src_ref[r].astype(F32)
        for t in range(w // LANE):
            stage_ref[t, pl.ds(r, per, stride=dil), :] = plane[:, t * LANE:(t + 1) * LANE]
    return jnp.concatenate([stage_ref[t] for t in range(w // LANE)], axis=1)


def _tail_kernel(x_ref, g1_ref, gla_ref, d0_ref, d1_ref, d2_ref, l0_ref, l1_ref, l2_ref, gm_ref, ssd_ref,
                 gates_ref, wg_ref, wd_ref, wm_ref, ws_ref, wo_ref, exp_ref,
                 nw_ref, sc_ref, sh_ref, g2_ref, w1_ref, w2_ref, fw_ref, o_ref,
                 od1_ref, od2_ref, ol1_ref, ol2_ref, *, final_norm):
    D = D_MODEL
    outs = (d0_ref[...].astype(F32), _token_order(d1_ref, od1_ref), _token_order(d2_ref, od2_ref))
    stats = (l0_ref[...], _token_order(l1_ref, ol1_ref), _token_order(l2_ref, ol2_ref))
    dens = [pltpu.roll(s, LANE - DIL_HEADS, axis=1) for s in stats]
    m = jnp.maximum(jnp.maximum(stats[0], stats[1]), stats[2])
    es = [jnp.exp(s - m) for s in stats]
    total = es[0] * dens[0] + es[1] * dens[1] + es[2] * dens[2]
    is_head = lax.broadcasted_iota(jnp.int32, m.shape, 1) < DIL_HEADS
    expand = exp_ref[...]
    o_dil = None
    for e, o_p in zip(es, outs):
        term = _dot_sel_r(jnp.where(is_head, e / total, 0.0), expand) * o_p
        o_dil = term if o_dil is None else o_dil + term
    branches = ((gla_ref[...], wg_ref), (o_dil, wd_ref), (gm_ref[...], wm_ref), (ssd_ref[...], ws_ref))
    merged = None
    for i, (o_b, w_ref) in enumerate(branches):
        g = gates_ref[:, i * D:(i + 1) * D]
        gate = (jnp.tanh(g * jnp.asarray(0.5, g.dtype)) * jnp.asarray(0.5, g.dtype)
                + jnp.asarray(0.5, g.dtype)).astype(F32)
        term = gate * _dot(o_b, w_ref[...])
        merged = term if merged is None else merged + term
    x = x_ref[...] + g1_ref[...] * _dot(merged, wo_ref[...])

    h = _modulated_norm(x, nw_ref[...], sc_ref[...], sh_ref[...]).astype(BF16)
    acc = jnp.zeros(x.shape, F32)
    for c0 in range(0, MLP_HIDDEN, D_MODEL):
        a = jnp.maximum(jnp.dot(h, w1_ref[:, c0:c0 + D_MODEL], preferred_element_type=F32), 0.0)
        acc = acc + _dot(a * a, w2_ref[c0:c0 + D_MODEL, :])
    y = x + g2_ref[...] * acc
    if final_norm:
        y = y * lax.rsqrt(jnp.mean(y * y, axis=-1, keepdims=True) + EPS) * fw_ref[...]
    o_ref[...] = y


def _mod_spec(layer, which):
    return pl.BlockSpec((None, None, None, 1, D_MODEL), lambda b, i: (layer, b, which, 0, 0))


def _tail(x, mod, o_gla, o_dil, lse_dil, o_gm, o_ssd, gates, w_gla, w_dil, w_gm, w_ssd, w_out,
          nw2, w1, w2, fw, layer, final_norm):
    B, S, D = x.shape
    TM = TM_MLP
    tok = lambda w: pl.BlockSpec((None, TM, w), lambda b, i: (b, i, 0))

    def plane(w, dil):
        if dil == 1:
            return pl.BlockSpec((None, None, TM, w), lambda b, i: (b, 0, i, 0))
        return pl.BlockSpec((None, dil, TM // dil, w), lambda b, i: (b, 0, i, 0))

    dils = [d for _, d in DIL_PATTERNS]
    assert dils[0] == 1 and len(dils) == 3
    expand = np.zeros((LANE, DIL_OUT), np.float32)
    for h in range(DIL_HEADS):
        expand[h, h * DIL_DIM:(h + 1) * DIL_DIM] = 1.0
    return pl.pallas_call(
        functools.partial(_tail_kernel, final_norm=final_norm),
        grid=(B, S // TM),
        in_specs=[tok(D), _mod_spec(layer, 2), tok(GLA_V)]
                 + [plane(DIL_OUT, d) for d in dils] + [plane(LANE, d) for d in dils]
                 + [tok(GM_W), tok(SSD_INNER), tok(N_BRANCH * D),
                    _layer_spec((GLA_V, D), layer), _layer_spec((DIL_OUT, D), layer),
                    _layer_spec((GM_W, D), layer), _layer_spec((SSD_INNER, D), layer),
                    _layer_spec((D, D), layer), _const_spec((LANE, DIL_OUT)),
                    _layer_spec((1, D), layer), _mod_spec(layer, 4), _mod_spec(layer, 3),
                    _mod_spec(layer, 5), _layer_spec((D, MLP_HIDDEN), layer),
                    _layer_spec((MLP_HIDDEN, D), layer), _const_spec((1, D))],
        out_specs=tok(D),
        out_shape=jax.ShapeDtypeStruct((B, S, D), F32),
        scratch_shapes=[pltpu.VMEM((DIL_OUT // LANE, TM, LANE), F32), pltpu.VMEM((DIL_OUT // LANE, TM, LANE), F32),
                        pltpu.VMEM((1, TM, LANE), F32), pltpu.VMEM((1, TM, LANE), F32)],
        compiler_params=_cparams(("parallel", "parallel"), 58),
        name="tail_final" if final_norm else "tail",
    )(x, mod, o_gla, *o_dil, *lse_dil, o_gm, o_ssd, gates, w_gla, w_dil, w_gm, w_ssd, w_out,
      jnp.asarray(expand, BF16), nw2, mod, mod, mod, w1, w2, fw)


def _repack_plan():
    whole = lambda a, b: [(_SP[a], _SP[b] - _SP[a], _SP[b] - _SP[a])]
    plan = {"gla": whole(0, 4), "glr": [(_SP[4], LANE, GLA_RANK)], "gm": whole(8, 10),
            "ssd_z": whole(10, 11), "ssd_xbc": whole(11, 12), "dt": [(_SP[12], LANE, SSD_HEADS)],
            "gates": whole(13, 14)}
    for p in range(len(DIL_PATTERNS)):
        plan[f"dil{p}"] = [(_SP[t] + p * DIL_OUT, DIL_OUT, DIL_OUT) for t in (5, 6, 7)]
    plan = [plan[name] for name in PROJ_NAMES]
    assert [sum(w for _, w, _ in pieces) for pieces in plan] == [o[1] for o in PROJ_OUTS]
    return plan


def _repack_kernel(wt_ref, o_ref, *, valid):
    v = wt_ref[0]
    if valid < v.shape[0]:
        v = jnp.where(lax.broadcasted_iota(jnp.int32, v.shape, 0) < valid, v, 0.0)
    o_ref[...] = v.T.astype(o_ref.dtype)


def _pack_w_in(w_in):
    L, D, W = w_in.shape
    w_t = jnp.swapaxes(w_in, 1, 2)
    outs = []
    for (name, width, _, _), pieces in zip(PROJ_OUTS, _repack_plan()):
        start, piece_w, valid = pieces[0]
        if len(pieces) > 1:
            tile, stride = piece_w, pieces[1][0] - start
            assert all(p == (start + i * stride, tile, tile) for i, p in enumerate(pieces))
        else:
            tile = next(t for t in (512, 256, LANE) if piece_w % t == 0)
            stride = tile
        assert start % 8 == 0 and stride % 8 == 0 and width % tile == 0
        outs.append(pl.pallas_call(
            functools.partial(_repack_kernel, valid=min(valid, tile)),
            grid=(L, width // tile),
            in_specs=[pl.BlockSpec((pl.Element(1), pl.Element(tile), pl.Element(D)),
                                   lambda l, t, start=start, stride=stride: (
                                       l, pl.multiple_of(start + t * stride, 8), 0))],
            out_specs=pl.BlockSpec((None, D, tile), lambda l, t: (l, 0, t)),
            out_shape=jax.ShapeDtypeStruct((L, D, width), BF16),
            compiler_params=_cparams(("parallel", "parallel"), 32),
            name=f"repack_{name}",
        )(w_t))
    return outs


def kernel(x, c, w_ada, b_ada, norm1_w, norm2_w, w_in, gla_w_a2, gla_b_a, gla_norm_w, gm_ln_w, gm_ln_b, gm_ws, gm_bs, ssd_conv_w, ssd_conv_b, ssd_dt_bias, ssd_a_log, ssd_d, ssd_norm_w, w_br_gla, w_br_dil, w_br_gm, w_br_ssd, w_out, w_mlp1, w_mlp2, final_norm_w):
    mod = _ada_mod(c, w_ada, b_ada)
    w_parts = _pack_w_in(w_in)
    wa = jnp.pad(gla_w_a2, ((0, 0), (0, LANE - GLA_RANK), (0, 0))).astype(BF16)
    w_gla, w_dil, w_gm, w_ssd, w_o, w1, w2 = (t.astype(BF16) for t in (
        w_br_gla, w_br_dil, w_br_gm, w_br_ssd, w_out, w_mlp1, w_mlp2))
    rows = lambda t: t[:, None, :]
    n1, n2 = rows(norm1_w), rows(norm2_w)
    gla_ba, gla_nw = rows(gla_b_a), rows(jnp.tile(gla_norm_w, (1, GLA_HEADS)))
    gm_lnw, gm_lnb = rows(gm_ln_w), rows(gm_ln_b)
    gm_w, gm_b = _gmlp_params(gm_ws, gm_bs)
    ssd_params = _ssd_params(ssd_dt_bias, ssd_a_log, ssd_d, ssd_norm_w)
    conv_b = rows(ssd_conv_b)
    fw = final_norm_w.reshape(1, -1)
    for l in range(DEPTH):
        p = dict(zip(PROJ_NAMES, _inproj(x, n1, mod, ssd_conv_w, conv_b, w_parts, l)))
        mixers = [_gla(p["gla"], p["glr"], wa, gla_ba, gla_nw, l),
                  _gmlp(p["gm"], gm_lnw, gm_lnb, gm_w, gm_b, l),
                  _ssd(p["ssd_z"], p["ssd_xbc"], p["dt"], ssd_params, l),
                  _dilated([p["dil0"], p["dil1"], p["dil2"]])]
        (o_gla,), (o_gm,), (o_ssd,), dil_outs = _fused_call(
            mixers, (x.shape[0], x.shape[1] // T_MIX), ("parallel", "arbitrary"), 48, "mixers")
        o_dil, lse_dil = list(dil_outs[0::2]), list(dil_outs[1::2])
        x = _tail(x, mod, o_gla, o_dil, lse_dil, o_gm, o_ssd, p["gates"], w_gla, w_dil, w_gm, w_ssd, w_o,
                  n2, w1, w2, fw, l, final_norm=(l == DEPTH - 1))
    return x
```

```python
import functools
from typing import Callable, NamedTuple

import numpy as np
import jax
import jax.numpy as jnp
from jax import lax
from jax.experimental import pallas as pl
from jax.experimental.pallas import tpu as pltpu

F32 = jnp.float32
BF16 = jnp.bfloat16

D_MODEL = 1024
DEPTH = 4
EPS = 1e-6

GLA_HEADS, GLA_DK, GLA_DV, GLA_RANK, GLA_CHUNK = 4, 32, 64, 16, 64
GLA_GATE_NORM = 16.0
DIL_PATTERNS = ((128, 1), (512, 4), (2048, 16))
DIL_HEADS, DIL_DIM, DIL_STEPS = 4, 64, 128
GM_GROUPS, GM_DIM, GM_CHUNK = 4, 64, 128
SSD_HEADS, SSD_P, SSD_GROUPS, SSD_N, SSD_CONV, SSD_CHUNK = 8, 64, 2, 128, 4, 128
N_BRANCH = 4
MLP_HIDDEN = 4 * D_MODEL

GLA_QK = GLA_HEADS * GLA_DK
GLA_V = GLA_HEADS * GLA_DV
DIL_W = len(DIL_PATTERNS) * DIL_HEADS * DIL_DIM
DIL_OUT = DIL_HEADS * DIL_DIM
GM_W = GM_GROUPS * GM_DIM
SSD_INNER = SSD_HEADS * SSD_P
SSD_BC = SSD_GROUPS * SSD_N
SSD_XBC = SSD_INNER + 2 * SSD_BC
IN_SPLITS = (GLA_QK, GLA_QK, GLA_V, GLA_V, GLA_RANK, DIL_W, DIL_W, DIL_W, GM_W, GM_W,
             SSD_INNER, SSD_XBC, SSD_HEADS, N_BRANCH * D_MODEL)
_SP = tuple(int(v) for v in np.cumsum((0,) + IN_SPLITS))

LANE = 128
NEG = -1e30

PROJ_OUTS = (("gla", 2 * GLA_QK + 2 * GLA_V, BF16, 1), ("glr", LANE, F32, 1),
             ("dil0", 3 * DIL_OUT, BF16, DIL_PATTERNS[0][1]), ("dil1", 3 * DIL_OUT, BF16, DIL_PATTERNS[1][1]),
             ("dil2", 3 * DIL_OUT, BF16, DIL_PATTERNS[2][1]),
             ("gm", 2 * GM_W, BF16, 1), ("ssd_z", SSD_INNER, BF16, 1), ("ssd_xbc", SSD_XBC, BF16, 1),
             ("dt", LANE, F32, 1), ("gates", N_BRANCH * D_MODEL, BF16, 1))
PROJ_NAMES = tuple(o[0] for o in PROJ_OUTS)

TM_PROJ = 512
TM_MLP = 512
T_MIX = 1024
GLA_CUMSUM_ROWS = 256
COL_CHUNK = 512
MLP_CHUNK = 1024
CAST_BLOCK = (128, 1024)
DIL_QBLOCKS = 8
DIL_LOOKAHEAD = 2


V7X_VMEM_BYTES = 64 * 1024 * 1024


def _cparams(sem, vmem_mib):
    assert vmem_mib * 1024 * 1024 <= V7X_VMEM_BYTES
    return pltpu.CompilerParams(dimension_semantics=sem, vmem_limit_bytes=V7X_VMEM_BYTES)


def _const_spec(shape):
    nd = len(shape)
    return pl.BlockSpec(shape, lambda *_: (0,) * nd, pipeline_mode=pl.Buffered(1))


def _layer_spec(shape, layer):
    nd = len(shape)
    return pl.BlockSpec((None,) + tuple(shape), lambda *_: (layer,) + (0,) * nd,
                        pipeline_mode=pl.Buffered(1))


class _Part(NamedTuple):
    kernel: Callable
    in_specs: list
    out_specs: list
    out_shape: list
    scratch: list
    args: list


def _fused_call(parts, grid, sem, vmem_mib, name):
    n_in = [len(p.in_specs) for p in parts]
    n_out = [len(p.out_specs) for p in parts]
    n_scr = [len(p.scratch) for p in parts]

    def body(*refs):
        ins, outs, scr = refs[:sum(n_in)], refs[sum(n_in):sum(n_in) + sum(n_out)], refs[sum(n_in) + sum(n_out):]
        for k, p in enumerate(parts):
            take = lambda group, counts: group[sum(counts[:k]):sum(counts[:k + 1])]
            p.kernel(*take(ins, n_in), *take(outs, n_out), *take(scr, n_scr))

    flat = pl.pallas_call(
        body, grid=grid,
        in_specs=[s for p in parts for s in p.in_specs],
        out_specs=[s for p in parts for s in p.out_specs],
        out_shape=[s for p in parts for s in p.out_shape],
        scratch_shapes=[s for p in parts for s in p.scratch],
        compiler_params=_cparams(sem, vmem_mib), name=name,
    )(*[a for p in parts for a in p.args])
    return [flat[sum(n_out[:k]):sum(n_out[:k + 1])] for k in range(len(parts))]


def _dot(a, b):
    return jnp.dot(a.astype(BF16), b.astype(BF16), preferred_element_type=F32)


def _dot_nt(a, b):
    return lax.dot_general(a.astype(BF16), b.astype(BF16), (((1,), (1,)), ((), ())),
                           preferred_element_type=F32)


def _split(a):
    hi = a.astype(BF16)
    lo = (a - hi.astype(F32)).astype(BF16)
    return hi, lo


def _dot_sel_r(a, sel):
    hi, lo = _split(a)
    return (jnp.dot(hi, sel, preferred_element_type=F32) + jnp.dot(lo, sel, preferred_element_type=F32))


def _dot_sel_l(sel, a):
    hi, lo = _split(a)
    return (jnp.dot(sel, hi, preferred_element_type=F32) + jnp.dot(sel, lo, preferred_element_type=F32))


def _sigmoid(x):
    return 1.0 / (1.0 + jnp.exp(-x))


def _silu(x):
    return x * _sigmoid(x)


def _gelu(x):
    return 0.5 * x * (1.0 + lax.erf(x * (2.0 ** -0.5)))


def _softplus(x):
    return jnp.maximum(x, 0.0) + jnp.log1p(jnp.exp(-jnp.abs(x)))


def _modulated_norm(x, nw, sc, sh):
    y = x * lax.rsqrt(jnp.mean(x * x, axis=-1, keepdims=True) + EPS)
    return (y * nw) * (1.0 + sc) + sh


def _ada_kernel(c_ref, w_ref, b_ref, o_ref):
    o_ref[...] = _dot(_silu(c_ref[...]), w_ref[...]) + b_ref[...]


def _ada_mod(c, w_ada, b_ada):
    B, D = c.shape
    rows = 16
    c_pad = jnp.pad(c, ((0, rows - B), (0, 0)))
    n_col = w_ada.shape[-1] // D
    out = pl.pallas_call(
        _ada_kernel,
        grid=(DEPTH, n_col),
        in_specs=[pl.BlockSpec((rows, D), lambda l, j: (0, 0)),
                  pl.BlockSpec((None, D, D), lambda l, j: (l, 0, j)),
                  pl.BlockSpec((None, 1, D), lambda l, j: (l, 0, j))],
        out_specs=pl.BlockSpec((None, rows, D), lambda l, j: (l, 0, j)),
        out_shape=jax.ShapeDtypeStruct((DEPTH, rows, n_col * D), F32),
        compiler_params=_cparams(("arbitrary", "arbitrary"), 32),
        name="ada_mod",
    )(c_pad, w_ada, b_ada.reshape(DEPTH, 1, -1))
    return out[:, :B].reshape(DEPTH, B, n_col, 1, D)


def _inproj_kernel(x_ref, nw_ref, sc_ref, sh_ref, cw_ref, cb_ref, w1_ref, w2_ref, *refs):
    n_out = len(PROJ_OUTS)
    w_refs, o_refs = refs[:n_out], refs[n_out:2 * n_out]
    w1_bf_ref, w2_bf_ref, stage_ref, carry_ref, win_ref = refs[2 * n_out:]
    TM = x_ref.shape[0]
    w1_bf_ref[...] = w1_ref[...].astype(w1_bf_ref.dtype)
    w2_bf_ref[...] = w2_ref[...].astype(w2_bf_ref.dtype)
    KEEP = carry_ref.shape[0]

    @pl.when(pl.program_id(1) == 0)
    def _():
        carry_ref[...] = jnp.zeros_like(carry_ref)

    h = _modulated_norm(x_ref[...], nw_ref[...], sc_ref[...], sh_ref[...]).astype(BF16)
    for w_ref, o_ref, (name, width, dtype, dil) in zip(w_refs, o_refs, PROJ_OUTS):
        for c0 in range(0, width, COL_CHUNK):
            c1 = min(c0 + COL_CHUNK, width)
            res = jnp.dot(h, w_ref[:, c0:c1], preferred_element_type=F32)
            if name == "ssd_xbc":
                win_ref[0:KEEP, :] = carry_ref[:, c0:c1]
                win_ref[KEEP:, :] = res
                carry_ref[:, c0:c1] = res[TM - KEEP:]
                conv = cb_ref[:, c0:c1]
                for j in range(SSD_CONV):
                    off = KEEP - (SSD_CONV - 1) + j
                    conv = conv + cw_ref[j:j + 1, c0:c1] * win_ref[off:off + TM, :]
                o_ref[:, c0:c1] = _silu(conv).astype(dtype)
            elif dil == 1:
                o_ref[:, c0:c1] = res.astype(dtype)
            else:
                for t in range((c1 - c0) // LANE):
                    stage_ref[c0 // LANE + t] = res[:, t * LANE:(t + 1) * LANE]
        if dil > 1:
            for r in range(dil):
                for t in range(width // LANE):
                    o_ref[r, :, t * LANE:(t + 1) * LANE] = stage_ref[
                        t, pl.ds(r, TM // dil, stride=dil), :].astype(dtype)


def _inproj(x, nw, mod, conv_w, conv_b, w_mlp1, w_mlp2, w_parts, layer):
    B, S, D = x.shape
    TM = TM_PROJ
    out_specs, out_shape = [], []
    for _, w, dt, dil in PROJ_OUTS:
        if dil == 1:
            out_specs.append(pl.BlockSpec((None, TM, w), lambda b, i: (b, i, 0)))
            out_shape.append(jax.ShapeDtypeStruct((B, S, w), dt))
        else:
            out_specs.append(pl.BlockSpec((None, dil, TM // dil, w), lambda b, i: (b, 0, i, 0)))
            out_shape.append(jax.ShapeDtypeStruct((B, dil, S // dil, w), dt))
    n_i, (cr, cc) = S // TM, CAST_BLOCK
    cast_in, steps = [], B * n_i
    for w in (w_mlp1, w_mlp2):
        _, rows, cols = w.shape
        assert rows % cr == 0 and cols % cc == 0 and (rows // cr) * (cols // cc) == steps
        at = lambda b, i, nc=cols // cc: ((b * n_i + i) // nc, (b * n_i + i) % nc)
        cast_in.append(pl.BlockSpec((None, cr, cc), lambda b, i, at=at: (layer, *at(b, i))))
        out_specs.append(pl.BlockSpec((cr, cc), at))
        out_shape.append(jax.ShapeDtypeStruct((rows, cols), BF16))
    return pl.pallas_call(
        _inproj_kernel,
        grid=(B, S // TM),
        in_specs=[pl.BlockSpec((None, TM, D), lambda b, i: (b, i, 0)),
                  _layer_spec((1, D), layer), _mod_spec(layer, 1), _mod_spec(layer, 0),
                  _layer_spec((SSD_CONV, SSD_XBC), layer), _layer_spec((1, SSD_XBC), layer), *cast_in]
                 + [_layer_spec((D, o[1]), layer) for o in PROJ_OUTS],
        out_specs=out_specs,
        out_shape=out_shape,
        scratch_shapes=[pltpu.VMEM((3 * DIL_OUT // LANE, TM, LANE), F32),
                        pltpu.VMEM((8, SSD_XBC), F32),
                        pltpu.VMEM((TM + 8, COL_CHUNK), F32)],
        compiler_params=_cparams(("parallel", "arbitrary"), 58),
        name="inproj",
    )(x, nw, mod, mod, conv_w, conv_b, w_mlp1, w_mlp2, *w_parts)


def _gla_kernel(gla_ref, glr_ref, wa_ref, ba_ref, nw_ref, tri_ref, ones_ref, o_ref, st_ref):
    C, H, K, V = GLA_CHUNK, GLA_HEADS, GLA_DK, GLA_DV
    T = gla_ref.shape[0]

    @pl.when(pl.program_id(1) == 0)
    def _():
        st_ref[...] = jnp.zeros_like(st_ref)

    blk = gla_ref[...]
    q = blk[:, 0:GLA_QK].astype(F32) * (K ** -0.5)
    k = blk[:, GLA_QK:2 * GLA_QK].astype(F32)
    v = blk[:, 2 * GLA_QK:2 * GLA_QK + GLA_V]
    g = blk[:, 2 * GLA_QK + GLA_V:].astype(F32)
    pre = _dot(glr_ref[...], wa_ref[...]) + ba_ref[...]
    log_a = -_softplus(-pre) * (1.0 / GLA_GATE_NORM)
    tri = tri_ref[...]
    TB = tri.shape[0]
    b_all = jnp.concatenate([_dot_sel_l(tri, log_a[i * TB:(i + 1) * TB]) for i in range(T // TB)],
                            axis=0)

    k_head = lax.broadcasted_iota(jnp.int32, (C, GLA_QK), 1) // K
    v_head = lax.broadcasted_iota(jnp.int32, (C, GLA_V), 1) // V
    row = lax.broadcasted_iota(jnp.int32, (H * C, C), 0)
    causal = (row % C) >= lax.broadcasted_iota(jnp.int32, (H * C, C), 1)
    st_mask = (lax.broadcasted_iota(jnp.int32, (GLA_V, GLA_QK), 0) // V
               == lax.broadcasted_iota(jnp.int32, (GLA_V, GLA_QK), 1) // K)

    n_chunk = T // C
    chunks = [slice(c * C, (c + 1) * C) for c in range(n_chunk)]
    b_c = [b_all[rs] for rs in chunks]
    q_t = [q[rs] * jnp.exp(b) for rs, b in zip(chunks, b_c)]
    k_t = [k[rs] * jnp.exp(-b) for rs, b in zip(chunks, b_c)]
    k_dec = [k[rs] * jnp.exp(b[C - 1:C] - b) for rs, b in zip(chunks, b_c)]
    att = []
    for c in range(n_chunk):
        q_stack = jnp.concatenate([jnp.where(k_head == h, q_t[c], 0.0) for h in range(H)], axis=0)
        att.append(jnp.where(causal, _dot_nt(q_stack, k_t[c]), 0.0))
    kv_t = [jnp.where(st_mask, _dot(v[rs].astype(F32).T, k_dec[c]), 0.0)
            for c, rs in enumerate(chunks)]
    o_all = [_dot(att[c], v[rs]) for c, rs in enumerate(chunks)]
    st = st_ref[...]
    outs = []
    for c in range(n_chunk):
        o = _dot_nt(q_t[c], st)
        for h in range(H):
            o = o + jnp.where(v_head == h, o_all[c][h * C:(h + 1) * C], 0.0)
        st = st * jnp.exp(b_c[c][C - 1:C]) + kv_t[c]
        outs.append(o)
    st_ref[...] = st

    o = jnp.concatenate(outs, axis=0)
    ms = _dot_sel_r(o * o, ones_ref[...]) * (1.0 / V)
    o = o * lax.rsqrt(ms + EPS) * nw_ref[...]
    o_ref[...] = (o * _silu(g)).astype(o_ref.dtype)


def _gla(gla, glr, wa, ba, nw, layer):
    B, S, W = gla.shape
    T, C, TB = T_MIX, GLA_CHUNK, GLA_CUMSUM_ROWS
    idx = np.arange(TB)
    tri = ((idx[:, None] // C == idx[None, :] // C) & (idx[:, None] >= idx[None, :]))
    hv = np.arange(GLA_V) // GLA_DV
    ones = hv[:, None] == hv[None, :]
    return _Part(
        _gla_kernel,
        in_specs=[pl.BlockSpec((None, T, W), lambda b, i: (b, i, 0)),
                  pl.BlockSpec((None, T, LANE), lambda b, i: (b, i, 0)),
                  _layer_spec((LANE, GLA_QK), layer), _layer_spec((1, GLA_QK), layer),
                  _layer_spec((1, GLA_V), layer), _const_spec((TB, TB)), _const_spec((GLA_V, GLA_V))],
        out_specs=[pl.BlockSpec((None, T, GLA_V), lambda b, i: (b, i, 0))],
        out_shape=[jax.ShapeDtypeStruct((B, S, GLA_V), BF16)],
        scratch=[pltpu.VMEM((GLA_V, GLA_QK), F32)],
        args=[gla, glr, wa, ba, nw, jnp.asarray(tri, BF16), jnp.asarray(ones, BF16)])


def _dil_kernel(*refs, row_steps):
    n_pat = len(row_steps)
    for p in range(n_pat):
        first = jnp.minimum(pl.program_id(1) % row_steps[p], 1)
        _dil_planes(*refs[6 * p:6 * p + 6], *refs[6 * n_pat + 2 * p:6 * n_pat + 2 * p + 2], first)


def _dil_planes(q_ref, kp_ref, k_ref, vp_ref, v_ref, bias_ref, o_ref, stat_ref, first):
    H, E, BLK = DIL_HEADS, DIL_DIM, DIL_STEPS
    n_res, n_blk = q_ref.shape[0], q_ref.shape[1] // BLK
    head = lax.broadcasted_iota(jnp.int32, (BLK, H * E), 1) // E
    lane = lax.broadcasted_iota(jnp.int32, (BLK, LANE), 1)
    kcat = [jnp.concatenate([kp_ref[r], k_ref[r]], axis=0) for r in range(n_res)]
    vcat = [jnp.concatenate([vp_ref[r], v_ref[r]], axis=0) for r in range(n_res)]
    v_head = lax.broadcasted_iota(jnp.int32, vcat[0].shape, 1) // E
    v_masked = [[jnp.where(v_head == h, vc, jnp.zeros_like(vc)) for h in range(H)] for vc in vcat]
    blocks =[(r, i) for r in range(n_res) for i in range(n_blk)]

    def scores(r, i):
        q = q_ref[r, i * BLK:(i + 1) * BLK, :] * jnp.asarray(E ** -0.5, BF16)
        q_stack = jnp.concatenate([jnp.where(head == h, q, jnp.zeros_like(q)) for h in range(H)], axis=0)
        bias = bias_ref[first] if i == 0 else bias_ref[1]
        return _dot_nt(q_stack, kcat[r][i * BLK:(i + 2) * BLK]) + bias

    pending = [scores(*blk) for blk in blocks[:DIL_LOOKAHEAD]]
    for n, (r, i) in enumerate(blocks):
        s = pending.pop(0)
        if n + DIL_LOOKAHEAD < len(blocks):
            pending.append(scores(*blocks[n + DIL_LOOKAHEAD]))
        m = jnp.max(s, axis=-1, keepdims=True)
        p = jnp.exp(s - m)
        l = jnp.sum(p, axis=-1, keepdims=True)
        p = p.astype(BF16)
        p_heads = jnp.concatenate([p[h * BLK:(h + 1) * BLK] for h in range(H)], axis=1)
        v_heads = jnp.concatenate([vm[i * BLK:(i + 2) * BLK] for vm in v_masked[r]], axis=0)
        o = jnp.dot(p_heads, v_heads, preferred_element_type=F32)
        stat = jnp.zeros((BLK, LANE), F32)
        for h in range(H):
            hs = slice(h * BLK, (h + 1) * BLK)
            stat = jnp.where(lane == h, m[hs], jnp.where(lane == H + h, l[hs], stat))
        rows = slice(i * BLK, (i + 1) * BLK)
        o_ref[r, rows, :] = o.astype(o_ref.dtype)
        stat_ref[r, rows, :] = stat


def _dil_bias(window, dil, slopes):
    BLK = DIL_STEPS
    steps = np.arange(BLK)[:, None] + BLK - np.arange(2 * BLK)[None, :]
    ok = (steps >= 0) & (steps <= window // dil)
    bias = -slopes[:, None, None] * (steps * dil)[None].astype(np.float32)
    full = np.where(ok[None], bias, NEG).astype(np.float32)
    first = np.where((np.arange(2 * BLK) >= BLK)[None, None, :], full, NEG)
    return np.stack([first.reshape(-1, 2 * BLK), full.reshape(-1, 2 * BLK)]).astype(np.float32)


def _dil_pattern(qkv, window, dil, slopes):
    B, _, n, _ = qkv.shape
    HE, BLK = DIL_OUT, DIL_STEPS
    n_blk = min(DIL_QBLOCKS, n // BLK)
    n_res = min(dil, DIL_QBLOCKS // n_blk)
    rows = n_blk * BLK
    assert n % rows == 0 and dil % n_res == 0
    row_steps = n // rows
    at = lambda j: (j // row_steps, j % row_steps)
    cur = lambda t: pl.BlockSpec((None, n_res, rows, HE), lambda b, j: (b, *at(j), t))
    prev = lambda t: pl.BlockSpec((None, n_res, BLK, HE),
                                  lambda b, j: (b, at(j)[0], jnp.maximum(at(j)[1] * n_blk - 1, 0), t))
    in_specs = [cur(0), prev(1), cur(1), prev(2), cur(2), _const_spec((2, DIL_HEADS * BLK, 2 * BLK))]
    out_specs = [pl.BlockSpec((None, n_res, rows, HE), lambda b, j: (b, *at(j), 0)),
                 pl.BlockSpec((None, n_res, rows, LANE), lambda b, j: (b, *at(j), 0))]
    out_shape = [jax.ShapeDtypeStruct((B, dil, n, HE), BF16), jax.ShapeDtypeStruct((B, dil, n, LANE), F32)]
    args = [qkv, qkv, qkv, qkv, qkv, jnp.asarray(_dil_bias(window, dil, slopes))]
    return in_specs, out_specs, out_shape, args, row_steps, (dil // n_res) * row_steps


def _dilated(qkvs):
    n_pat = len(DIL_PATTERNS)
    n_h = n_pat * DIL_HEADS
    slopes = (2.0 ** (-8.0 * np.arange(1, n_h + 1) / n_h)).astype(np.float32).reshape(n_pat, DIL_HEADS)
    in_specs, out_specs, out_shape, args, row_steps, steps = [], [], [], [], [], set()
    for p, (w, d) in enumerate(DIL_PATTERNS):
        qkv = qkvs[p] if qkvs[p].ndim == 4 else qkvs[p][:, None]
        assert qkv.shape[1] == d
        i_s, o_s, o_sh, a, rs, n_steps = _dil_pattern(qkv, w, d, slopes[p])
        in_specs += i_s; out_specs += o_s; out_shape += o_sh; args += a
        row_steps.append(rs); steps.add(n_steps)
    assert len(steps) == 1, "patterns must split into equally many grid steps"
    assert steps.pop() == qkvs[0].shape[1] // T_MIX
    return _Part(functools.partial(_dil_kernel, row_steps=tuple(row_steps)),
                 in_specs=in_specs, out_specs=out_specs, out_shape=out_shape, scratch=[], args=args)


def _gm_kernel(gm_ref, lnw_ref, lnb_ref, w_ref, bs_ref, o_ref):
    G, E, C = GM_GROUPS, GM_DIM, GM_CHUNK
    T = gm_ref.shape[0]
    blk = gm_ref[...].astype(F32)
    u = _gelu(blk[:, :GM_W])
    v = _gelu(blk[:, GM_W:])
    mu = jnp.mean(v, axis=-1, keepdims=True)
    var = jnp.mean(jnp.square(v - mu), axis=-1, keepdims=True)
    v = (v - mu) * lax.rsqrt(var + EPS) * lnw_ref[...] + lnb_ref[...]
    group = lax.broadcasted_iota(jnp.int32, (C, GM_W), 1) // E
    w = w_ref[...]
    bs = bs_ref[...]
    for c in range(T // C):
        rs = slice(c * C, (c + 1) * C)
        r = _dot(w, v[rs])
        sv = bs
        for gi in range(G):
            sv = sv + jnp.where(group == gi, r[gi * C:(gi + 1) * C], 0.0)
        o_ref[rs, :] = (u[rs] * sv).astype(o_ref.dtype)


def _gmlp_params(ws, bs):
    C = GM_CHUNK
    tril = jnp.tril(jnp.ones((C, C), bool))
    w_stack = jnp.where(tril, ws, 0.0).reshape(-1, GM_GROUPS * C, C).astype(BF16)
    bs_exp = jnp.repeat(jnp.swapaxes(bs, 1, 2), GM_DIM, axis=2)
    return w_stack, bs_exp


def _gmlp(gm, lnw, lnb, w_stack, bs_exp, layer):
    B, S, W = gm.shape
    C = GM_CHUNK
    return _Part(
        _gm_kernel,
        in_specs=[pl.BlockSpec((None, T_MIX, W), lambda b, i: (b, i, 0)),
                  _layer_spec((1, GM_W), layer), _layer_spec((1, GM_W), layer),
                  _layer_spec((GM_GROUPS * C, C), layer), _layer_spec((C, GM_W), layer)],
        out_specs=[pl.BlockSpec((None, T_MIX, GM_W), lambda b, i: (b, i, 0))],
        out_shape=[jax.ShapeDtypeStruct((B, S, GM_W), BF16)],
        scratch=[], args=[gm, lnw, lnb, w_stack, bs_exp])


def _ssd_kernel(z_ref, xbc_ref, dt_ref, dtb_ref, alog_ref, dskip_ref, nw_ref, tri_ref, exp_ref,
                o_ref, st_ref):
    H, P, G, N, L = SSD_HEADS, SSD_P, SSD_GROUPS, SSD_N, SSD_CHUNK
    R = H // G
    GW = R * P
    T = z_ref.shape[0]

    @pl.when(pl.program_id(1) == 0)
    def _():
        st_ref[...] = jnp.zeros_like(st_ref)

    dt_all = _softplus(dt_ref[...] + dtb_ref[...])
    da_all = dt_all * (-jnp.exp(alog_ref[...]))
    tri = tri_ref[...]
    expand = exp_ref[...]
    lower = lax.broadcasted_iota(jnp.int32, (L, L), 0) >= lax.broadcasted_iota(jnp.int32, (L, L), 1)
    head_in_group = lax.broadcasted_iota(jnp.int32, (L, GW), 1) // P

    chunks = [slice(c * L, (c + 1) * L) for c in range(T // L)]
    groups = [slice(gi * GW, (gi + 1) * GW) for gi in range(G)]
    b_cols = [slice(SSD_INNER + gi * N, SSD_INNER + (gi + 1) * N) for gi in range(G)]
    c_cols = [slice(SSD_INNER + SSD_BC + gi * N, SSD_INNER + SSD_BC + (gi + 1) * N) for gi in range(G)]
    xs = [xbc_ref[rs, :SSD_INNER].astype(F32) for rs in chunks]
    cs = [_dot_sel_l(tri, da_all[rs]) for rs in chunks]
    dt_e = [_dot(dt_all[rs], expand) for rs in chunks]
    dec_e = [_dot(jnp.exp(cs_c[L - 1:L] - cs_c) * dt_all[rs], expand) for rs, cs_c in zip(chunks, cs)]
    ecs_e = [_dot_sel_r(jnp.exp(cs_c), expand) for cs_c in cs]
    x_dt = [(x * e).astype(BF16) for x, e in zip(xs, dt_e)]
    x_dec = [(x * e).astype(BF16) for x, e in zip(xs, dec_e)]
    cb = [[_dot_nt(xbc_ref[rs, c_cols[gi]], xbc_ref[rs, b_cols[gi]]) for gi in range(G)]
          for rs in chunks]
    bx = [[_dot(xbc_ref[rs, b_cols[gi]].astype(F32).T, x_dec[c][:, groups[gi]]) for gi in range(G)]
          for c, rs in enumerate(chunks)]
    y_intra = []
    for c in range(len(chunks)):
        cs_t = cs[c].T
        per_group = []
        for gi in range(G):
            acc = None
            for r in range(R):
                h = gi * R + r
                seg = cs[c][:, h:h + 1] - cs_t[h:h + 1, :]
                wgt = cb[c][gi] * jnp.exp(jnp.where(lower, seg, NEG))
                y_h = _dot(wgt, x_dt[c][:, groups[gi]])
                acc = y_h if r == 0 else jnp.where(head_in_group == r, y_h, acc)
            per_group.append(acc)
        y_intra.append(per_group)

    for c, rs in enumerate(chunks):
        ys = []
        for gi, gs in enumerate(groups):
            y_state = _dot(xbc_ref[rs, c_cols[gi]], st_ref[:, gs]) * ecs_e[c][:, gs]
            st_ref[:, gs] = st_ref[:, gs] * ecs_e[c][L - 1:L, gs] + bx[c][gi]
            ys.append(y_state + y_intra[c][gi])
        y = jnp.concatenate(ys, axis=1) + xs[c] * dskip_ref[...]
        y = y * _silu(z_ref[rs, :].astype(F32))
        normed = []
        for gs in groups:
            y_g = y[:, gs]
            normed.append(y_g * lax.rsqrt(jnp.mean(y_g * y_g, axis=-1, keepdims=True) + EPS))
        o_ref[rs, :] = (jnp.concatenate(normed, axis=1) * nw_ref[...]).astype(o_ref.dtype)


def _ssd_params(dt_bias, a_log, d_skip, norm_w):
    pad = lambda t: jnp.pad(t, ((0, 0), (0, LANE - t.shape[1])))[:, None, :]
    return (pad(dt_bias), pad(a_log), jnp.repeat(d_skip, SSD_P, axis=1)[:, None, :], norm_w[:, None, :])


def _ssd(z, xbc, dt, params, layer):
    B, S, _ = z.shape
    L = SSD_CHUNK
    tri = np.tril(np.ones((L, L), np.float32))
    expand = np.zeros((LANE, SSD_INNER), np.float32)
    for h in range(SSD_HEADS):
        expand[h, h * SSD_P:(h + 1) * SSD_P] = 1.0
    return _Part(
        _ssd_kernel,
        in_specs=[pl.BlockSpec((None, T_MIX, SSD_INNER), lambda b, i: (b, i, 0)),
                  pl.BlockSpec((None, T_MIX, SSD_XBC), lambda b, i: (b, i, 0)),
                  pl.BlockSpec((None, T_MIX, LANE), lambda b, i: (b, i, 0)),
                  _layer_spec((1, LANE), layer), _layer_spec((1, LANE), layer),
                  _layer_spec((1, SSD_INNER), layer), _layer_spec((1, SSD_INNER), layer),
                  _const_spec((L, L)), _const_spec((LANE, SSD_INNER))],
        out_specs=[pl.BlockSpec((None, T_MIX, SSD_INNER), lambda b, i: (b, i, 0))],
        out_shape=[jax.ShapeDtypeStruct((B, S, SSD_INNER), BF16)],
        scratch=[pltpu.VMEM((SSD_N, SSD_INNER), F32)],
        args=[z, xbc, dt, *params, jnp.asarray(tri, BF16), jnp.asarray(expand, BF16)])


def _token_order(src_ref, stage_ref):
    dil, per, w = src_ref.shape
    for r in range(dil):
        plane = src_ref[r].astype(F32)
        for t in range(w // LANE):
            stage_ref[t, pl.ds(r, per, stride=dil), :] = plane[:, t * LANE:(t + 1) * LANE]
    return jnp.concatenate([stage_ref[t] for t in range(w // LANE)], axis=1)


def _tail_kernel(x_ref, g1_ref, gla_ref, d0_ref, d1_ref, d2_ref, l0_ref, l1_ref, l2_ref, gm_ref, ssd_ref,
                 gates_ref, wg_ref, wd_ref, wm_ref, ws_ref, wo_ref, exp_ref,
                 nw_ref, sc_ref, sh_ref, g2_ref, w1_ref, w2_ref, fw_ref, o_ref,
                 od1_ref, od2_ref, ol1_ref, ol2_ref, *, final_norm):
    D = D_MODEL
    outs = (d0_ref[...].astype(F32), _token_order(d1_ref, od1_ref), _token_order(d2_ref, od2_ref))
    stats = (l0_ref[...], _token_order(l1_ref, ol1_ref), _token_order(l2_ref, ol2_ref))
    dens = [pltpu.roll(s, LANE - DIL_HEADS, axis=1) for s in stats]
    m = jnp.maximum(jnp.maximum(stats[0], stats[1]), stats[2])
    es = [jnp.exp(s - m) for s in stats]
    total = es[0] * dens[0] + es[1] * dens[1] + es[2] * dens[2]
    is_head = lax.broadcasted_iota(jnp.int32, m.shape, 1) < DIL_HEADS
    expand = exp_ref[...]
    o_dil = None
    for e, o_p in zip(es, outs):
        term = _dot_sel_r(jnp.where(is_head, e / total, 0.0), expand) * o_p
        o_dil = term if o_dil is None else o_dil + term
    branches = ((gla_ref[...], wg_ref), (o_dil, wd_ref), (gm_ref[...], wm_ref), (ssd_ref[...], ws_ref))
    merged = None
    for i, (o_b, w_ref) in enumerate(branches):
        g = gates_ref[:, i * D:(i + 1) * D]
        gate = (jnp.tanh(g * jnp.asarray(0.5, g.dtype)) * jnp.asarray(0.5, g.dtype)
                + jnp.asarray(0.5, g.dtype)).astype(F32)
        term = gate * _dot(o_b, w_ref[...])
        merged = term if merged is None else merged + term
    x = x_ref[...] + g1_ref[...] * _dot(merged, wo_ref[...])

    h = _modulated_norm(x, nw_ref[...], sc_ref[...], sh_ref[...]).astype(BF16)
    acc = jnp.zeros(x.shape, F32)
    for c0 in range(0, MLP_HIDDEN, MLP_CHUNK):
        a = jnp.maximum(jnp.dot(h, w1_ref[:, c0:c0 + MLP_CHUNK], preferred_element_type=F32), 0.0)
        acc = acc + _dot(a * a, w2_ref[c0:c0 + MLP_CHUNK, :])
    y = x + g2_ref[...] * acc
    if final_norm:
        y = y * lax.rsqrt(jnp.mean(y * y, axis=-1, keepdims=True) + EPS) * fw_ref[...]
    o_ref[...] = y


def _mod_spec(layer, which):
    return pl.BlockSpec((None, None, None, 1, D_MODEL), lambda b, i: (layer, b, which, 0, 0))


def _tail(x, mod, o_gla, o_dil, lse_dil, o_gm, o_ssd, gates, w_gla, w_dil, w_gm, w_ssd, w_out,
          nw2, w1, w2, fw, layer, final_norm):
    B, S, D = x.shape
    TM = TM_MLP
    tok = lambda w: pl.BlockSpec((None, TM, w), lambda b, i: (b, i, 0))

    def plane(w, dil):
        if dil == 1:
            return pl.BlockSpec((None, None, TM, w), lambda b, i: (b, 0, i, 0))
        return pl.BlockSpec((None, dil, TM // dil, w), lambda b, i: (b, 0, i, 0))

    dils = [d for _, d in DIL_PATTERNS]
    assert dils[0] == 1 and len(dils) == 3
    expand = np.zeros((LANE, DIL_OUT), np.float32)
    for h in range(DIL_HEADS):
        expand[h, h * DIL_DIM:(h + 1) * DIL_DIM] = 1.0
    return pl.pallas_call(
        functools.partial(_tail_kernel, final_norm=final_norm),
        grid=(B, S // TM),
        in_specs=[tok(D), _mod_spec(layer, 2), tok(GLA_V)]
                 + [plane(DIL_OUT, d) for d in dils] + [plane(LANE, d) for d in dils]
                 + [tok(GM_W), tok(SSD_INNER), tok(N_BRANCH * D),
                    _layer_spec((GLA_V, D), layer), _layer_spec((DIL_OUT, D), layer),
                    _layer_spec((GM_W, D), layer), _layer_spec((SSD_INNER, D), layer),
                    _layer_spec((D, D), layer), _const_spec((LANE, DIL_OUT)),
                    _layer_spec((1, D), layer), _mod_spec(layer, 4), _mod_spec(layer, 3),
                    _mod_spec(layer, 5), _const_spec((D, MLP_HIDDEN)),
                    _const_spec((MLP_HIDDEN, D)), _const_spec((1, D))],
        out_specs=tok(D),
        out_shape=jax.ShapeDtypeStruct((B, S, D), F32),
        scratch_shapes=[pltpu.VMEM((DIL_OUT // LANE, TM, LANE), F32), pltpu.VMEM((DIL_OUT // LANE, TM, LANE), F32),
                        pltpu.VMEM((1, TM, LANE), F32), pltpu.VMEM((1, TM, LANE), F32)],
        compiler_params=_cparams(("parallel", "parallel"), 58),
        name="tail_final" if final_norm else "tail",
    )(x, mod, o_gla, *o_dil, *lse_dil, o_gm, o_ssd, gates, w_gla, w_dil, w_gm, w_ssd, w_out,
      jnp.asarray(expand, BF16), nw2, mod, mod, mod, w1, w2, fw)


def _repack_plan():
    whole = lambda a, b: [(_SP[a], _SP[b] - _SP[a], _SP[b] - _SP[a])]
    plan = {"gla": whole(0, 4), "glr": [(_SP[4], LANE, GLA_RANK)], "gm": whole(8, 10),
            "ssd_z": whole(10, 11), "ssd_xbc": whole(11, 12), "dt": [(_SP[12], LANE, SSD_HEADS)],
            "gates": whole(13, 14)}
    for p in range(len(DIL_PATTERNS)):
        plan[f"dil{p}"] = [(_SP[t] + p * DIL_OUT, DIL_OUT, DIL_OUT) for t in (5, 6, 7)]
    plan = [plan[name] for name in PROJ_NAMES]
    assert [sum(w for _, w, _ in pieces) for pieces in plan] == [o[1] for o in PROJ_OUTS]
    return plan


def _repack_kernel(wt_ref, o_ref, *, valid):
    v = wt_ref[0]
    if valid < v.shape[0]:
        v = jnp.where(lax.broadcasted_iota(jnp.int32, v.shape, 0) < valid, v, 0.0)
    o_ref[...] = v.T.astype(o_ref.dtype)


def _pack_w_in(w_in):
    L, D, W = w_in.shape
    w_t = jnp.swapaxes(w_in, 1, 2)
    outs = []
    for (name, width, _, _), pieces in zip(PROJ_OUTS, _repack_plan()):
        start, piece_w, valid = pieces[0]
        if len(pieces) > 1:
            tile, stride = piece_w, pieces[1][0] - start
            assert all(p == (start + i * stride, tile, tile) for i, p in enumerate(pieces))
        else:
            tile = next(t for t in (512, 256, LANE) if piece_w % t == 0)
            stride = tile
        assert start % 8 == 0 and stride % 8 == 0 and width % tile == 0
        outs.append(pl.pallas_call(
            functools.partial(_repack_kernel, valid=min(valid, tile)),
            grid=(L, width // tile),
            in_specs=[pl.BlockSpec((pl.Element(1), pl.Element(tile), pl.Element(D)),
                                   lambda l, t, start=start, stride=stride: (
                                       l, pl.multiple_of(start + t * stride, 8), 0))],
            out_specs=pl.BlockSpec((None, D, tile), lambda l, t: (l, 0, t)),
            out_shape=jax.ShapeDtypeStruct((L, D, width), BF16),
            compiler_params=_cparams(("parallel", "parallel"), 32),
            name=f"repack_{name}",
        )(w_t))
    return outs


def kernel(x, c, w_ada, b_ada, norm1_w, norm2_w, w_in, gla_w_a2, gla_b_a, gla_norm_w, gm_ln_w, gm_ln_b, gm_ws, gm_bs, ssd_conv_w, ssd_conv_b, ssd_dt_bias, ssd_a_log, ssd_d, ssd_norm_w, w_br_gla, w_br_dil, w_br_gm, w_br_ssd, w_out, w_mlp1, w_mlp2, final_norm_w):
    mod = _ada_mod(c, w_ada, b_ada)
    w_parts = _pack_w_in(w_in)
    wa = jnp.pad(gla_w_a2, ((0, 0), (0, LANE - GLA_RANK), (0, 0))).astype(BF16)
    w_gla, w_dil, w_gm, w_ssd, w_o = (t.astype(BF16) for t in (
        w_br_gla, w_br_dil, w_br_gm, w_br_ssd, w_out))
    rows = lambda t: t[:, None, :]
    n1, n2 = rows(norm1_w), rows(norm2_w)
    gla_ba, gla_nw = rows(gla_b_a), rows(jnp.tile(gla_norm_w, (1, GLA_HEADS)))
    gm_lnw, gm_lnb = rows(gm_ln_w), rows(gm_ln_b)
    gm_w, gm_b = _gmlp_params(gm_ws, gm_bs)
    ssd_params = _ssd_params(ssd_dt_bias, ssd_a_log, ssd_d, ssd_norm_w)
    conv_b = rows(ssd_conv_b)
    fw = final_norm_w.reshape(1, -1)
    for l in range(DEPTH):
        *proj, w1, w2 = _inproj(x, n1, mod, ssd_conv_w, conv_b, w_mlp1, w_mlp2, w_parts, l)
        p = dict(zip(PROJ_NAMES, proj))
        mixers = [_gla(p["gla"], p["glr"], wa, gla_ba, gla_nw, l),
                  _gmlp(p["gm"], gm_lnw, gm_lnb, gm_w, gm_b, l),
                  _ssd(p["ssd_z"], p["ssd_xbc"], p["dt"], ssd_params, l),
                  _dilated([p["dil0"], p["dil1"], p["dil2"]])]
        (o_gla,), (o_gm,), (o_ssd,), dil_outs = _fused_call(
            mixers, (x.shape[0], x.shape[1] // T_MIX), ("parallel", "arbitrary"), 48, "mixers")
        o_dil, lse_dil = list(dil_outs[0::2]), list(dil_outs[1::2])
        x = _tail(x, mod, o_gla, o_dil, lse_dil, o_gm, o_ssd, p["gates"], w_gla, w_dil, w_gm, w_ssd, w_o,
                  n2, w1, w2, fw, l, final_norm=(l == DEPTH - 1))
    return x
```

```python
import functools
from typing import Callable, NamedTuple

import numpy as np
import jax
import jax.numpy as jnp
from jax import lax
from jax.experimental import pallas as pl
from jax.experimental.pallas import tpu as pltpu

F32 = jnp.float32
BF16 = jnp.bfloat16

D_MODEL = 1024
DEPTH = 4
EPS = 1e-6

GLA_HEADS, GLA_DK, GLA_DV, GLA_RANK, GLA_CHUNK = 4, 32, 64, 16, 64
GLA_GATE_NORM = 16.0
DIL_PATTERNS = ((128, 1), (512, 4), (2048, 16))
DIL_HEADS, DIL_DIM, DIL_STEPS = 4, 64, 128
GM_GROUPS, GM_DIM, GM_CHUNK = 4, 64, 128
SSD_HEADS, SSD_P, SSD_GROUPS, SSD_N, SSD_CONV, SSD_CHUNK = 8, 64, 2, 128, 4, 128
N_BRANCH = 4
MLP_HIDDEN = 4 * D_MODEL

GLA_QK = GLA_HEADS * GLA_DK
GLA_V = GLA_HEADS * GLA_DV
DIL_W = len(DIL_PATTERNS) * DIL_HEADS * DIL_DIM
DIL_OUT = DIL_HEADS * DIL_DIM
GM_W = GM_GROUPS * GM_DIM
SSD_INNER = SSD_HEADS * SSD_P
SSD_BC = SSD_GROUPS * SSD_N
SSD_XBC = SSD_INNER + 2 * SSD_BC
IN_SPLITS = (GLA_QK, GLA_QK, GLA_V, GLA_V, GLA_RANK, DIL_W, DIL_W, DIL_W, GM_W, GM_W,
             SSD_INNER, SSD_XBC, SSD_HEADS, N_BRANCH * D_MODEL)
_SP = tuple(int(v) for v in np.cumsum((0,) + IN_SPLITS))

LANE = 128
NEG = -1e30

PROJ_OUTS = (("gla", 2 * GLA_QK + 2 * GLA_V, BF16, 1), ("glr", LANE, F32, 1),
             ("dil0", 3 * DIL_OUT, BF16, DIL_PATTERNS[0][1]), ("dil1", 3 * DIL_OUT, BF16, DIL_PATTERNS[1][1]),
             ("dil2", 3 * DIL_OUT, BF16, DIL_PATTERNS[2][1]),
             ("gm", 2 * GM_W, BF16, 1), ("ssd_z", SSD_INNER, BF16, 1), ("ssd_xbc", SSD_XBC, BF16, 1),
             ("dt", LANE, F32, 1), ("gates", N_BRANCH * D_MODEL, BF16, 1))
PROJ_NAMES = tuple(o[0] for o in PROJ_OUTS)

TM_PROJ = 512
TM_MLP = 512
T_MIX = 1024
GLA_CUMSUM_ROWS = 256
COL_CHUNK = 512
MLP_CHUNK = 1024
CAST_BLOCK = (128, 1024)
DIL_QBLOCKS = 8
DIL_LOOKAHEAD = 2


V7X_VMEM_BYTES = 64 * 1024 * 1024


def _cparams(sem, vmem_mib):
    assert vmem_mib * 1024 * 1024 <= V7X_VMEM_BYTES
    return pltpu.CompilerParams(dimension_semantics=sem, vmem_limit_bytes=V7X_VMEM_BYTES)


def _const_spec(shape):
    nd = len(shape)
    return pl.BlockSpec(shape, lambda *_: (0,) * nd, pipeline_mode=pl.Buffered(1))


def _layer_spec(shape, layer):
    nd = len(shape)
    return pl.BlockSpec((None,) + tuple(shape), lambda *_: (layer,) + (0,) * nd,
                        pipeline_mode=pl.Buffered(1))


class _Part(NamedTuple):
    kernel: Callable
    in_specs: list
    out_specs: list
    out_shape: list
    scratch: list
    args: list


def _fused_call(parts, grid, sem, vmem_mib, name):
    n_in = [len(p.in_specs) for p in parts]
    n_out = [len(p.out_specs) for p in parts]
    n_scr = [len(p.scratch) for p in parts]

    def body(*refs):
        ins, outs, scr = refs[:sum(n_in)], refs[sum(n_in):sum(n_in) + sum(n_out)], refs[sum(n_in) + sum(n_out):]
        for k, p in enumerate(parts):
            take = lambda group, counts: group[sum(counts[:k]):sum(counts[:k + 1])]
            p.kernel(*take(ins, n_in), *take(outs, n_out), *take(scr, n_scr))

    flat = pl.pallas_call(
        body, grid=grid,
        in_specs=[s for p in parts for s in p.in_specs],
        out_specs=[s for p in parts for s in p.out_specs],
        out_shape=[s for p in parts for s in p.out_shape],
        scratch_shapes=[s for p in parts for s in p.scratch],
        compiler_params=_cparams(sem, vmem_mib), name=name,
    )(*[a for p in parts for a in p.args])
    return [flat[sum(n_out[:k]):sum(n_out[:k + 1])] for k in range(len(parts))]


def _dot(a, b):
    return jnp.dot(a.astype(BF16), b.astype(BF16), preferred_element_type=F32)


def _dot_nt(a, b):
    return lax.dot_general(a.astype(BF16), b.astype(BF16), (((1,), (1,)), ((), ())),
                           preferred_element_type=F32)


def _split(a):
    hi = a.astype(BF16)
    lo = (a - hi.astype(F32)).astype(BF16)
    return hi, lo


def _dot_sel_r(a, sel):
    hi, lo = _split(a)
    return (jnp.dot(hi, sel, preferred_element_type=F32) + jnp.dot(lo, sel, preferred_element_type=F32))


def _dot_sel_l(sel, a):
    hi, lo = _split(a)
    return (jnp.dot(sel, hi, preferred_element_type=F32) + jnp.dot(sel, lo, preferred_element_type=F32))


def _sigmoid(x):
    return 1.0 / (1.0 + jnp.exp(-x))


def _silu(x):
    return x * _sigmoid(x)


def _gelu(x):
    return 0.5 * x * (1.0 + lax.erf(x * (2.0 ** -0.5)))


def _softplus(x):
    return jnp.maximum(x, 0.0) + jnp.log1p(jnp.exp(-jnp.abs(x)))


def _modulated_norm(x, nw, sc, sh):
    y = x * lax.rsqrt(jnp.mean(x * x, axis=-1, keepdims=True) + EPS)
    return (y * nw) * (1.0 + sc) + sh


def _ada_kernel(c_ref, w_ref, b_ref, o_ref):
    o_ref[...] = _dot(_silu(c_ref[...]), w_ref[...]) + b_ref[...]


def _ada_mod(c, w_ada, b_ada):
    B, D = c.shape
    rows = 16
    c_pad = jnp.pad(c, ((0, rows - B), (0, 0)))
    n_col = w_ada.shape[-1] // D
    out = pl.pallas_call(
        _ada_kernel,
        grid=(DEPTH, n_col),
        in_specs=[pl.BlockSpec((rows, D), lambda l, j: (0, 0)),
                  pl.BlockSpec((None, D, D), lambda l, j: (l, 0, j)),
                  pl.BlockSpec((None, 1, D), lambda l, j: (l, 0, j))],
        out_specs=pl.BlockSpec((None, rows, D), lambda l, j: (l, 0, j)),
        out_shape=jax.ShapeDtypeStruct((DEPTH, rows, n_col * D), F32),
        compiler_params=_cparams(("arbitrary", "arbitrary"), 32),
        name="ada_mod",
    )(c_pad, w_ada, b_ada.reshape(DEPTH, 1, -1))
    return out[:, :B].reshape(DEPTH, B, n_col, 1, D)


def _inproj_kernel(x_ref, nw_ref, sc_ref, sh_ref, cw_ref, cb_ref, w1_ref, w2_ref, *refs):
    n_out = len(PROJ_OUTS)
    w_refs, o_refs = refs[:n_out], refs[n_out:2 * n_out]
    w1_bf_ref, w2_bf_ref, stage_ref, carry_ref, win_ref = refs[2 * n_out:]
    TM = x_ref.shape[0]
    w1_bf_ref[...] = w1_ref[...].astype(w1_bf_ref.dtype)
    w2_bf_ref[...] = w2_ref[...].astype(w2_bf_ref.dtype)
    KEEP = carry_ref.shape[0]

    @pl.when(pl.program_id(1) == 0)
    def _():
        carry_ref[...] = jnp.zeros_like(carry_ref)

    h = _modulated_norm(x_ref[...], nw_ref[...], sc_ref[...], sh_ref[...]).astype(BF16)
    for w_ref, o_ref, (name, width, dtype, dil) in zip(w_refs, o_refs, PROJ_OUTS):
        for c0 in range(0, width, COL_CHUNK):
            c1 = min(c0 + COL_CHUNK, width)
            res = jnp.dot(h, w_ref[:, c0:c1], preferred_element_type=F32)
            if name == "ssd_xbc":
                win_ref[0:KEEP, :] = carry_ref[:, c0:c1]
                win_ref[KEEP:, :] = res
                carry_ref[:, c0:c1] = res[TM - KEEP:]
                conv = cb_ref[:, c0:c1]
                for j in range(SSD_CONV):
                    off = KEEP - (SSD_CONV - 1) + j
                    conv = conv + cw_ref[j:j + 1, c0:c1] * win_ref[off:off + TM, :]
                o_ref[:, c0:c1] = _silu(conv).astype(dtype)
            elif dil == 1:
                o_ref[:, c0:c1] = res.astype(dtype)
            else:
                for t in range((c1 - c0) // LANE):
                    stage_ref[c0 // LANE + t] = res[:, t * LANE:(t + 1) * LANE]
        if dil > 1:
            for r in range(dil):
                for t in range(width // LANE):
                    o_ref[r, :, t * LANE:(t + 1) * LANE] = stage_ref[
                        t, pl.ds(r, TM // dil, stride=dil), :].astype(dtype)


def _inproj(x, nw, mod, conv_w, conv_b, w_mlp1, w_mlp2, w_parts, layer):
    B, S, D = x.shape
    TM = TM_PROJ
    out_specs, out_shape = [], []
    for _, w, dt, dil in PROJ_OUTS:
        if dil == 1:
            out_specs.append(pl.BlockSpec((None, TM, w), lambda b, i: (b, i, 0)))
            out_shape.append(jax.ShapeDtypeStruct((B, S, w), dt))
        else:
            out_specs.append(pl.BlockSpec((None, dil, TM // dil, w), lambda b, i: (b, 0, i, 0)))
            out_shape.append(jax.ShapeDtypeStruct((B, dil, S // dil, w), dt))
    n_i, (cr, cc) = S // TM, CAST_BLOCK
    cast_in, steps = [], B * n_i
    for w in (w_mlp1, w_mlp2):
        _, rows, cols = w.shape
        assert rows % cr == 0 and cols % cc == 0 and (rows // cr) * (cols // cc) == steps
        at = lambda b, i, nc=cols // cc: ((b * n_i + i) // nc, (b * n_i + i) % nc)
        cast_in.append(pl.BlockSpec((None, cr, cc), lambda b, i, at=at: (layer, *at(b, i))))
        out_specs.append(pl.BlockSpec((cr, cc), at))
        out_shape.append(jax.ShapeDtypeStruct((rows, cols), BF16))
    return pl.pallas_call(
        _inproj_kernel,
        grid=(B, S // TM),
        in_specs=[pl.BlockSpec((None, TM, D), lambda b, i: (b, i, 0)),
                  _layer_spec((1, D), layer), _mod_spec(layer, 1), _mod_spec(layer, 0),
                  _layer_spec((SSD_CONV, SSD_XBC), layer), _layer_spec((1, SSD_XBC), layer), *cast_in]
                 + [_layer_spec((D, o[1]), layer) for o in PROJ_OUTS],
        out_specs=out_specs,
        out_shape=out_shape,
        scratch_shapes=[pltpu.VMEM((3 * DIL_OUT // LANE, TM, LANE), F32),
                        pltpu.VMEM((8, SSD_XBC), F32),
                        pltpu.VMEM((TM + 8, COL_CHUNK), F32)],
        compiler_params=_cparams(("parallel", "arbitrary"), 58),
        name="inproj",
    )(x, nw, mod, mod, conv_w, conv_b, w_mlp1, w_mlp2, *w_parts)


def _gla_kernel(gla_ref, glr_ref, wa_ref, ba_ref, nw_ref, tri_ref, ones_ref, o_ref, st_ref):
    C, H, K, V = GLA_CHUNK, GLA_HEADS, GLA_DK, GLA_DV
    T = gla_ref.shape[0]

    @pl.when(pl.program_id(1) == 0)
    def _():
        st_ref[...] = jnp.zeros_like(st_ref)

    blk = gla_ref[...]
    q = blk[:, 0:GLA_QK].astype(F32) * (K ** -0.5)
    k = blk[:, GLA_QK:2 * GLA_QK].astype(F32)
    v = blk[:, 2 * GLA_QK:2 * GLA_QK + GLA_V]
    g = blk[:, 2 * GLA_QK + GLA_V:].astype(F32)
    pre = _dot(glr_ref[...], wa_ref[...]) + ba_ref[...]
    log_a = -_softplus(-pre) * (1.0 / GLA_GATE_NORM)
    tri = tri_ref[...]
    TB = tri.shape[0]
    b_all = jnp.concatenate([_dot_sel_l(tri, log_a[i * TB:(i + 1) * TB]) for i in range(T // TB)],
                            axis=0)

    k_head = lax.broadcasted_iota(jnp.int32, (C, GLA_QK), 1) // K
    v_head = lax.broadcasted_iota(jnp.int32, (C, GLA_V), 1) // V
    row = lax.broadcasted_iota(jnp.int32, (H * C, C), 0)
    causal = (row % C) >= lax.broadcasted_iota(jnp.int32, (H * C, C), 1)
    st_mask = (lax.broadcasted_iota(jnp.int32, (GLA_V, GLA_QK), 0) // V
               == lax.broadcasted_iota(jnp.int32, (GLA_V, GLA_QK), 1) // K)

    n_chunk = T // C
    chunks = [slice(c * C, (c + 1) * C) for c in range(n_chunk)]
    b_c = [b_all[rs] for rs in chunks]
    q_t = [q[rs] * jnp.exp(b) for rs, b in zip(chunks, b_c)]
    k_t = [k[rs] * jnp.exp(-b) for rs, b in zip(chunks, b_c)]
    k_dec = [k[rs] * jnp.exp(b[C - 1:C] - b) for rs, b in zip(chunks, b_c)]
    att = []
    for c in range(n_chunk):
        q_stack = jnp.concatenate([jnp.where(k_head == h, q_t[c], 0.0) for h in range(H)], axis=0)
        att.append(jnp.where(causal, _dot_nt(q_stack, k_t[c]), 0.0))
    kv_t = [jnp.where(st_mask, _dot(v[rs].astype(F32).T, k_dec[c]), 0.0)
            for c, rs in enumerate(chunks)]
    o_all = [_dot(att[c], v[rs]) for c, rs in enumerate(chunks)]
    st = st_ref[...]
    outs = []
    for c in range(n_chunk):
        o = _dot_nt(q_t[c], st)
        for h in range(H):
            o = o + jnp.where(v_head == h, o_all[c][h * C:(h + 1) * C], 0.0)
        st = st * jnp.exp(b_c[c][C - 1:C]) + kv_t[c]
        outs.append(o)
    st_ref[...] = st

    o = jnp.concatenate(outs, axis=0)
    ms = _dot_sel_r(o * o, ones_ref[...]) * (1.0 / V)
    o = o * lax.rsqrt(ms + EPS) * nw_ref[...]
    o_ref[...] = (o * _silu(g)).astype(o_ref.dtype)


def _gla(gla, glr, wa, ba, nw, layer):
    B, S, W = gla.shape
    T, C, TB = T_MIX, GLA_CHUNK, GLA_CUMSUM_ROWS
    idx = np.arange(TB)
    tri = ((idx[:, None] // C == idx[None, :] // C) & (idx[:, None] >= idx[None, :]))
    hv = np.arange(GLA_V) // GLA_DV
    ones = hv[:, None] == hv[None, :]
    return _Part(
        _gla_kernel,
        in_specs=[pl.BlockSpec((None, T, W), lambda b, i: (b, i, 0)),
                  pl.BlockSpec((None, T, LANE), lambda b, i: (b, i, 0)),
                  _layer_spec((LANE, GLA_QK), layer), _layer_spec((1, GLA_QK), layer),
                  _layer_spec((1, GLA_V), layer), _const_spec((TB, TB)), _const_spec((GLA_V, GLA_V))],
        out_specs=[pl.BlockSpec((None, T, GLA_V), lambda b, i: (b, i, 0))],
        out_shape=[jax.ShapeDtypeStruct((B, S, GLA_V), BF16)],
        scratch=[pltpu.VMEM((GLA_V, GLA_QK), F32)],
        args=[gla, glr, wa, ba, nw, jnp.asarray(tri, BF16), jnp.asarray(ones, BF16)])


def _dil_kernel(*refs, row_steps):
    n_pat = len(row_steps)
    for p in range(n_pat):
        first = jnp.minimum(pl.program_id(1) % row_steps[p], 1)
        _dil_planes(*refs[6 * p:6 * p + 6], *refs[6 * n_pat + 2 * p:6 * n_pat + 2 * p + 2], first)


def _dil_planes(q_ref, kp_ref, k_ref, vp_ref, v_ref, bias_ref, o_ref, stat_ref, first):
    H, E, BLK = DIL_HEADS, DIL_DIM, DIL_STEPS
    n_res, n_blk = q_ref.shape[0], q_ref.shape[1] // BLK
    head = lax.broadcasted_iota(jnp.int32, (BLK, H * E), 1) // E
    lane = lax.broadcasted_iota(jnp.int32, (BLK, LANE), 1)
    kcat = [jnp.concatenate([kp_ref[r], k_ref[r]], axis=0) for r in range(n_res)]
    vcat = [jnp.concatenate([vp_ref[r], v_ref[r]], axis=0) for r in range(n_res)]
    v_head = lax.broadcasted_iota(jnp.int32, vcat[0].shape, 1) // E
    v_masked = [[jnp.where(v_head == h, vc, jnp.zeros_like(vc)) for h in range(H)] for vc in vcat]
    blocks =[(r, i) for r in range(n_res) for i in range(n_blk)]

    def scores(r, i):
        q = q_ref[r, i * BLK:(i + 1) * BLK, :] * jnp.asarray(E ** -0.5, BF16)
        q_stack = jnp.concatenate([jnp.where(head == h, q, jnp.zeros_like(q)) for h in range(H)], axis=0)
        bias = bias_ref[first] if i == 0 else bias_ref[1]
        return _dot_nt(q_stack, kcat[r][i * BLK:(i + 2) * BLK]) + bias

    pending = [scores(*blk) for blk in blocks[:DIL_LOOKAHEAD]]
    for n, (r, i) in enumerate(blocks):
        s = pending.pop(0)
        if n + DIL_LOOKAHEAD < len(blocks):
            pending.append(scores(*blocks[n + DIL_LOOKAHEAD]))
        m = jnp.max(s, axis=-1, keepdims=True)
        p = jnp.exp(s - m)
        l = jnp.sum(p, axis=-1, keepdims=True)
        p = p.astype(BF16)
        p_heads = jnp.concatenate([p[h * BLK:(h + 1) * BLK] for h in range(H)], axis=1)
        v_heads = jnp.concatenate([vm[i * BLK:(i + 2) * BLK] for vm in v_masked[r]], axis=0)
        o = jnp.dot(p_heads, v_heads, preferred_element_type=F32)
        stat = jnp.zeros((BLK, LANE), F32)
        for h in range(H):
            hs = slice(h * BLK, (h + 1) * BLK)
            stat = jnp.where(lane == h, m[hs], jnp.where(lane == H + h, l[hs], stat))
        rows = slice(i * BLK, (i + 1) * BLK)
        o_ref[r, rows, :] = o.astype(o_ref.dtype)
        stat_ref[r, rows, :] = stat


def _dil_bias(window, dil, slopes):
    BLK = DIL_STEPS
    steps = np.arange(BLK)[:, None] + BLK - np.arange(2 * BLK)[None, :]
    ok = (steps >= 0) & (steps <= window // dil)
    bias = -slopes[:, None, None] * (steps * dil)[None].astype(np.float32)
    full = np.where(ok[None], bias, NEG).astype(np.float32)
    first = np.where((np.arange(2 * BLK) >= BLK)[None, None, :], full, NEG)
    return np.stack([first.reshape(-1, 2 * BLK), full.reshape(-1, 2 * BLK)]).astype(np.float32)


def _dil_pattern(qkv, window, dil, slopes):
    B, _, n, _ = qkv.shape
    HE, BLK = DIL_OUT, DIL_STEPS
    n_blk = min(DIL_QBLOCKS, n // BLK)
    n_res = min(dil, DIL_QBLOCKS // n_blk)
    rows = n_blk * BLK
    assert n % rows == 0 and dil % n_res == 0
    row_steps = n // rows
    at = lambda j: (j // row_steps, j % row_steps)
    cur = lambda t: pl.BlockSpec((None, n_res, rows, HE), lambda b, j: (b, *at(j), t))
    prev = lambda t: pl.BlockSpec((None, n_res, BLK, HE),
                                  lambda b, j: (b, at(j)[0], jnp.maximum(at(j)[1] * n_blk - 1, 0), t))
    in_specs = [cur(0), prev(1), cur(1), prev(2), cur(2), _const_spec((2, DIL_HEADS * BLK, 2 * BLK))]
    out_specs = [pl.BlockSpec((None, n_res, rows, HE), lambda b, j: (b, *at(j), 0)),
                 pl.BlockSpec((None, n_res, rows, LANE), lambda b, j: (b, *at(j), 0))]
    out_shape = [jax.ShapeDtypeStruct((B, dil, n, HE), BF16), jax.ShapeDtypeStruct((B, dil, n, LANE), F32)]
    args = [qkv, qkv, qkv, qkv, qkv, jnp.asarray(_dil_bias(window, dil, slopes))]
    return in_specs, out_specs, out_shape, args, row_steps, (dil // n_res) * row_steps


def _dilated(qkvs):
    n_pat = len(DIL_PATTERNS)
    n_h = n_pat * DIL_HEADS
    slopes = (2.0 ** (-8.0 * np.arange(1, n_h + 1) / n_h)).astype(np.float32).reshape(n_pat, DIL_HEADS)
    in_specs, out_specs, out_shape, args, row_steps, steps = [], [], [], [], [], set()
    for p, (w, d) in enumerate(DIL_PATTERNS):
        qkv = qkvs[p] if qkvs[p].ndim == 4 else qkvs[p][:, None]
        assert qkv.shape[1] == d
        i_s, o_s, o_sh, a, rs, n_steps = _dil_pattern(qkv, w, d, slopes[p])
        in_specs += i_s; out_specs += o_s; out_shape += o_sh; args += a
        row_steps.append(rs); steps.add(n_steps)
    assert len(steps) == 1, "patterns must split into equally many grid steps"
    assert steps.pop() == qkvs[0].shape[1] // T_MIX
    return _Part(functools.partial(_dil_kernel, row_steps=tuple(row_steps)),
                 in_specs=in_specs, out_specs=out_specs, out_shape=out_shape, scratch=[], args=args)


def _gm_kernel(gm_ref, lnw_ref, lnb_ref, w_ref, bs_ref, o_ref):
    G, E, C = GM_GROUPS, GM_DIM, GM_CHUNK
    T = gm_ref.shape[0]
    blk = gm_ref[...].astype(F32)
    u = _gelu(blk[:, :GM_W])
    v = _gelu(blk[:, GM_W:])
    mu = jnp.mean(v, axis=-1, keepdims=True)
    var = jnp.mean(jnp.square(v - mu), axis=-1, keepdims=True)
    v = (v - mu) * lax.rsqrt(var + EPS) * lnw_ref[...] + lnb_ref[...]
    group = lax.broadcasted_iota(jnp.int32, (C, GM_W), 1) // E
    w = w_ref[...]
    bs = bs_ref[...]
    for c in range(T // C):
        rs = slice(c * C, (c + 1) * C)
        r = _dot(w, v[rs])
        sv = bs
        for gi in range(G):
            sv = sv + jnp.where(group == gi, r[gi * C:(gi + 1) * C], 0.0)
        o_ref[rs, :] = (u[rs] * sv).astype(o_ref.dtype)


def _gmlp_params(ws, bs):
    C = GM_CHUNK
    tril = jnp.tril(jnp.ones((C, C), bool))
    w_stack = jnp.where(tril, ws, 0.0).reshape(-1, GM_GROUPS * C, C).astype(BF16)
    bs_exp = jnp.repeat(jnp.swapaxes(bs, 1, 2), GM_DIM, axis=2)
    return w_stack, bs_exp


def _gmlp(gm, lnw, lnb, w_stack, bs_exp, layer):
    B, S, W = gm.shape
    C = GM_CHUNK
    return _Part(
        _gm_kernel,
        in_specs=[pl.BlockSpec((None, T_MIX, W), lambda b, i: (b, i, 0)),
                  _layer_spec((1, GM_W), layer), _layer_spec((1, GM_W), layer),
                  _layer_spec((GM_GROUPS * C, C), layer), _layer_spec((C, GM_W), layer)],
        out_specs=[pl.BlockSpec((None, T_MIX, GM_W), lambda b, i: (b, i, 0))],
        out_shape=[jax.ShapeDtypeStruct((B, S, GM_W), BF16)],
        scratch=[], args=[gm, lnw, lnb, w_stack, bs_exp])


def _ssd_kernel(z_ref, xbc_ref, dt_ref, dtb_ref, alog_ref, dskip_ref, nw_ref, tri_ref, exp_ref,
                o_ref, st_ref):
    H, P, G, N, L = SSD_HEADS, SSD_P, SSD_GROUPS, SSD_N, SSD_CHUNK
    R = H // G
    GW = R * P
    T = z_ref.shape[0]

    @pl.when(pl.program_id(1) == 0)
    def _():
        st_ref[...] = jnp.zeros_like(st_ref)

    dt_all = _softplus(dt_ref[...] + dtb_ref[...])
    da_all = dt_all * (-jnp.exp(alog_ref[...]))
    tri = tri_ref[...]
    expand = exp_ref[...]
    lower = lax.broadcasted_iota(jnp.int32, (L, L), 0) >= lax.broadcasted_iota(jnp.int32, (L, L), 1)
    head_in_group = lax.broadcasted_iota(jnp.int32, (L, GW), 1) // P

    chunks = [slice(c * L, (c + 1) * L) for c in range(T // L)]
    groups = [slice(gi * GW, (gi + 1) * GW) for gi in range(G)]
    b_cols = [slice(SSD_INNER + gi * N, SSD_INNER + (gi + 1) * N) for gi in range(G)]
    c_cols = [slice(SSD_INNER + SSD_BC + gi * N, SSD_INNER + SSD_BC + (gi + 1) * N) for gi in range(G)]
    xs = [xbc_ref[rs, :SSD_INNER].astype(F32) for rs in chunks]
    cs = [_dot_sel_l(tri, da_all[rs]) for rs in chunks]
    dt_e = [_dot(dt_all[rs], expand) for rs in chunks]
    dec_e = [_dot(jnp.exp(cs_c[L - 1:L] - cs_c) * dt_all[rs], expand) for rs, cs_c in zip(chunks, cs)]
    ecs_e = [_dot_sel_r(jnp.exp(cs_c), expand) for cs_c in cs]
    x_dt = [(x * e).astype(BF16) for x, e in zip(xs, dt_e)]
    x_dec = [(x * e).astype(BF16) for x, e in zip(xs, dec_e)]
    cb = [[_dot_nt(xbc_ref[rs, c_cols[gi]], xbc_ref[rs, b_cols[gi]]) for gi in range(G)]
          for rs in chunks]
    bx = [[_dot(xbc_ref[rs, b_cols[gi]].astype(F32).T, x_dec[c][:, groups[gi]]) for gi in range(G)]
          for c, rs in enumerate(chunks)]
    y_intra = []
    for c in range(len(chunks)):
        cs_t = cs[c].T
        per_group = []
        for gi in range(G):
            acc = None
            for r in range(R):
                h = gi * R + r
                seg = cs[c][:, h:h + 1] - cs_t[h:h + 1, :]
                wgt = cb[c][gi] * jnp.exp(jnp.where(lower, seg, NEG))
                y_h = _dot(wgt, x_dt[c][:, groups[gi]])
                acc = y_h if r == 0 else jnp.where(head_in_group == r, y_h, acc)
            per_group.append(acc)
        y_intra.append(per_group)

    for c, rs in enumerate(chunks):
        ys = []
        for gi, gs in enumerate(groups):
            y_state = _dot(xbc_ref[rs, c_cols[gi]], st_ref[:, gs]) * ecs_e[c][:, gs]
            st_ref[:, gs] = st_ref[:, gs] * ecs_e[c][L - 1:L, gs] + bx[c][gi]
            ys.append(y_state + y_intra[c][gi])
        y = jnp.concatenate(ys, axis=1) + xs[c] * dskip_ref[...]
        y = y * _silu(z_ref[rs, :].astype(F32))
        normed = []
        for gs in groups:
            y_g = y[:, gs]
            normed.append(y_g * lax.rsqrt(jnp.mean(y_g * y_g, axis=-1, keepdims=True) + EPS))
        o_ref[rs, :] = (jnp.concatenate(normed, axis=1) * nw_ref[...]).astype(o_ref.dtype)


def _ssd_params(dt_bias, a_log, d_skip, norm_w):
    pad = lambda t: jnp.pad(t, ((0, 0), (0, LANE - t.shape[1])))[:, None, :]
    return (pad(dt_bias), pad(a_log), jnp.repeat(d_skip, SSD_P, axis=1)[:, None, :], norm_w[:, None, :])


def _ssd(z, xbc, dt, params, layer):
    B, S, _ = z.shape
    L = SSD_CHUNK
    tri = np.tril(np.ones((L, L), np.float32))
    expand = np.zeros((LANE, SSD_INNER), np.float32)
    for h in range(SSD_HEADS):
        expand[h, h * SSD_P:(h + 1) * SSD_P] = 1.0
    return _Part(
        _ssd_kernel,
        in_specs=[pl.BlockSpec((None, T_MIX, SSD_INNER), lambda b, i: (b, i, 0)),
                  pl.BlockSpec((None, T_MIX, SSD_XBC), lambda b, i: (b, i, 0)),
                  pl.BlockSpec((None, T_MIX, LANE), lambda b, i: (b, i, 0)),
                  _layer_spec((1, LANE), layer), _layer_spec((1, LANE), layer),
                  _layer_spec((1, SSD_INNER), layer), _layer_spec((1, SSD_INNER), layer),
                  _const_spec((L, L)), _const_spec((LANE, SSD_INNER))],
        out_specs=[pl.BlockSpec((None, T_MIX, SSD_INNER), lambda b, i: (b, i, 0))],
        out_shape=[jax.ShapeDtypeStruct((B, S, SSD_INNER), BF16)],
        scratch=[pltpu.VMEM((SSD_N, SSD_INNER), F32)],
        args=[z, xbc, dt, *params, jnp.asarray(tri, BF16), jnp.asarray(expand, BF16)])


def _token_order(src_ref, stage_ref):
    dil, per, w = src_ref.shape
    for r in range(dil):
        plane = src_ref[r].astype(F32)
        for t in range(w // LANE):
            stage_ref[t, pl.ds(r, per, stride=dil), :] = plane[:, t * LANE:(t + 1) * LANE]
    return jnp.concatenate([stage_ref[t] for t in range(w // LANE)], axis=1)


def _tail_kernel(x_ref, g1_ref, gla_ref, d0_ref, d1_ref, d2_ref, l0_ref, l1_ref, l2_ref, gm_ref, ssd_ref,
                 gates_ref, wg_ref, wd_ref, wm_ref, ws_ref, wo_ref, exp_ref,
                 nw_ref, sc_ref, sh_ref, g2_ref, w1_ref, w2_ref, fw_ref, o_ref,
                 od1_ref, od2_ref, ol1_ref, ol2_ref, *, final_norm):
    D = D_MODEL
    outs = (d0_ref[...].astype(F32), _token_order(d1_ref, od1_ref), _token_order(d2_ref, od2_ref))
    stats = (l0_ref[...], _token_order(l1_ref, ol1_ref), _token_order(l2_ref, ol2_ref))
    dens = [pltpu.roll(s, LANE - DIL_HEADS, axis=1) for s in stats]
    m = jnp.maximum(jnp.maximum(stats[0], stats[1]), stats[2])
    es = [jnp.exp(s - m) for s in stats]
    total = es[0] * dens[0] + es[1] * dens[1] + es[2] * dens[2]
    is_head = lax.broadcasted_iota(jnp.int32, m.shape, 1) < DIL_HEADS
    expand = exp_ref[...]
    o_dil = None
    for e, o_p in zip(es, outs):
        term = _dot_sel_r(jnp.where(is_head, e / total, 0.0), expand) * o_p
        o_dil = term if o_dil is None else o_dil + term
    branches = ((gla_ref[...], wg_ref), (o_dil, wd_ref), (gm_ref[...], wm_ref), (ssd_ref[...], ws_ref))
    merged = None
    for i, (o_b, w_ref) in enumerate(branches):
        g = gates_ref[:, i * D:(i + 1) * D]
        gate = (jnp.tanh(g * jnp.asarray(0.5, g.dtype)) * jnp.asarray(0.5, g.dtype)
                + jnp.asarray(0.5, g.dtype)).astype(F32)
        term = gate * _dot(o_b, w_ref[...])
        merged = term if merged is None else merged + term
    x = x_ref[...] + g1_ref[...] * _dot(merged, wo_ref[...])

    h = _modulated_norm(x, nw_ref[...], sc_ref[...], sh_ref[...]).astype(BF16)
    acc = jnp.zeros(x.shape, F32)
    for c0 in range(0, MLP_HIDDEN, MLP_CHUNK):
        a = jnp.maximum(jnp.dot(h, w1_ref[:, c0:c0 + MLP_CHUNK], preferred_element_type=F32), 0.0)
        acc = acc + _dot(a * a, w2_ref[c0:c0 + MLP_CHUNK, :])
    y = x + g2_ref[...] * acc
    if final_norm:
        y = y * lax.rsqrt(jnp.mean(y * y, axis=-1, keepdims=True) + EPS) * fw_ref[...]
    o_ref[...] = y


def _mod_spec(layer, which):
    return pl.BlockSpec((None, None, None, 1, D_MODEL), lambda b, i: (layer, b, which, 0, 0))


def _tail(x, mod, o_gla, o_dil, lse_dil, o_gm, o_ssd, gates, w_gla, w_dil, w_gm, w_ssd, w_out,
          nw2, w1, w2, fw, layer, final_norm):
    B, S, D = x.shape
    TM = TM_MLP
    tok = lambda w: pl.BlockSpec((None, TM, w), lambda b, i: (b, i, 0))

    def plane(w, dil):
        if dil == 1:
            return pl.BlockSpec((None, None, TM, w), lambda b, i: (b, 0, i, 0))
        return pl.BlockSpec((None, dil, TM // dil, w), lambda b, i: (b, 0, i, 0))

    dils = [d for _, d in DIL_PATTERNS]
    assert dils[0] == 1 and len(dils) == 3
    expand = np.zeros((LANE, DIL_OUT), np.float32)
    for h in range(DIL_HEADS):
        expand[h, h * DIL_DIM:(h + 1) * DIL_DIM] = 1.0
    return pl.pallas_call(
        functools.partial(_tail_kernel, final_norm=final_norm),
        grid=(B, S // TM),
        in_specs=[tok(D), _mod_spec(layer, 2), tok(GLA_V)]
                 + [plane(DIL_OUT, d) for d in dils] + [plane(LANE, d) for d in dils]
                 + [tok(GM_W), tok(SSD_INNER), tok(N_BRANCH * D),
                    _layer_spec((GLA_V, D), layer), _layer_spec((DIL_OUT, D), layer),
                    _layer_spec((GM_W, D), layer), _layer_spec((SSD_INNER, D), layer),
                    _layer_spec((D, D), layer), _const_spec((LANE, DIL_OUT)),
                    _layer_spec((1, D), layer), _mod_spec(layer, 4), _mod_spec(layer, 3),
                    _mod_spec(layer, 5), _const_spec((D, MLP_HIDDEN)),
                    _const_spec((MLP_HIDDEN, D)), _const_spec((1, D))],
        out_specs=tok(D),
        out_shape=jax.ShapeDtypeStruct((B, S, D), F32),
        scratch_shapes=[pltpu.VMEM((DIL_OUT // LANE, TM, LANE), F32), pltpu.VMEM((DIL_OUT // LANE, TM, LANE), F32),
                        pltpu.VMEM((1, TM, LANE), F32), pltpu.VMEM((1, TM, LANE), F32)],
        compiler_params=_cparams(("parallel", "parallel"), 58),
        name="tail_final" if final_norm else "tail",
    )(x, mod, o_gla, *o_dil, *lse_dil, o_gm, o_ssd, gates, w_gla, w_dil, w_gm, w_ssd, w_out,
      jnp.asarray(expand, BF16), nw2, mod, mod, mod, w1, w2, fw)


def _repack_plan():
    whole = lambda a, b: [(_SP[a], _SP[b] - _SP[a], _SP[b] - _SP[a])]
    plan = {"gla": whole(0, 4), "glr": [(_SP[4], LANE, GLA_RANK)], "gm": whole(8, 10),
            "ssd_z": whole(10, 11), "ssd_xbc": whole(11, 12), "dt": [(_SP[12], LANE, SSD_HEADS)],
            "gates": whole(13, 14)}
    for p in range(len(DIL_PATTERNS)):
        plan[f"dil{p}"] = [(_SP[t] + p * DIL_OUT, DIL_OUT, DIL_OUT) for t in (5, 6, 7)]
    plan = [plan[name] for name in PROJ_NAMES]
    assert [sum(w for _, w, _ in pieces) for pieces in plan] == [o[1] for o in PROJ_OUTS]
    return plan


def _repack_kernel(wt_ref, o_ref, *, valid):
    v = wt_ref[0]
    if valid < v.shape[0]:
        v = jnp.where(lax.broadcasted_iota(jnp.int32, v.shape, 0) < valid, v, 0.0)
    o_ref[...] = v.T.astype(o_ref.dtype)


def _pack_w_in(w_in):
    L, D, W = w_in.shape
    w_t = jnp.swapaxes(w_in, 1, 2)
    outs = []
    for (name, width, _, _), pieces in zip(PROJ_OUTS, _repack_plan()):
        start, piece_w, valid = pieces[0]
        if len(pieces) > 1:
            tile, stride = piece_w, pieces[1][0] - start
            assert all(p == (start + i * stride, tile, tile) for i, p in enumerate(pieces))
        else:
            tile = next(t for t in (1024, 768, 512, 256, LANE) if piece_w % t == 0)
            stride = tile
        assert start % 8 == 0 and stride % 8 == 0 and width % tile == 0
        outs.append(pl.pallas_call(
            functools.partial(_repack_kernel, valid=min(valid, tile)),
            grid=(L, width // tile),
            in_specs=[pl.BlockSpec((pl.Element(1), pl.Element(tile), pl.Element(D)),
                                   lambda l, t, start=start, stride=stride: (
                                       l, pl.multiple_of(start + t * stride, 8), 0))],
            out_specs=pl.BlockSpec((None, D, tile), lambda l, t: (l, 0, t)),
            out_shape=jax.ShapeDtypeStruct((L, D, width), BF16),
            compiler_params=_cparams(("parallel", "parallel"), 32),
            name=f"repack_{name}",
        )(w_t))
    return outs


def kernel(x, c, w_ada, b_ada, norm1_w, norm2_w, w_in, gla_w_a2, gla_b_a, gla_norm_w, gm_ln_w, gm_ln_b, gm_ws, gm_bs, ssd_conv_w, ssd_conv_b, ssd_dt_bias, ssd_a_log, ssd_d, ssd_norm_w, w_br_gla, w_br_dil, w_br_gm, w_br_ssd, w_out, w_mlp1, w_mlp2, final_norm_w):
    mod = _ada_mod(c, w_ada, b_ada)
    w_parts = _pack_w_in(w_in)
    wa = jnp.pad(gla_w_a2, ((0, 0), (0, LANE - GLA_RANK), (0, 0))).astype(BF16)
    w_gla, w_dil, w_gm, w_ssd, w_o = (t.astype(BF16) for t in (
        w_br_gla, w_br_dil, w_br_gm, w_br_ssd, w_out))
    rows = lambda t: t[:, None, :]
    n1, n2 = rows(norm1_w), rows(norm2_w)
    gla_ba, gla_nw = rows(gla_b_a), rows(jnp.tile(gla_norm_w, (1, GLA_HEADS)))
    gm_lnw, gm_lnb = rows(gm_ln_w), rows(gm_ln_b)
    gm_w, gm_b = _gmlp_params(gm_ws, gm_bs)
    ssd_params = _ssd_params(ssd_dt_bias, ssd_a_log, ssd_d, ssd_norm_w)
    conv_b = rows(ssd_conv_b)
    fw = final_norm_w.reshape(1, -1)
    for l in range(DEPTH):
        *proj, w1, w2 = _inproj(x, n1, mod, ssd_conv_w, conv_b, w_mlp1, w_mlp2, w_parts, l)
        p = dict(zip(PROJ_NAMES, proj))
        mixers = [_gla(p["gla"], p["glr"], wa, gla_ba, gla_nw, l),
                  _gmlp(p["gm"], gm_lnw, gm_lnb, gm_w, gm_b, l),
                  _ssd(p["ssd_z"], p["ssd_xbc"], p["dt"], ssd_params, l),
                  _dilated([p["dil0"], p["dil1"], p["dil2"]])]
        (o_gla,), (o_gm,), (o_ssd,), dil_outs = _fused_call(
            mixers, (x.shape[0], x.shape[1] // T_MIX), ("parallel", "arbitrary"), 48, "mixers")
        o_dil, lse_dil = list(dil_outs[0::2]), list(dil_outs[1::2])
        x = _tail(x, mod, o_gla, o_dil, lse_dil, o_gm, o_ssd, p["gates"], w_gla, w_dil, w_gm, w_ssd, w_o,
                  n2, w1, w2, fw, l, final_norm=(l == DEPTH - 1))
    return x
```

```python
import functools
from typing import Callable, NamedTuple

import numpy as np
import jax
import jax.numpy as jnp
from jax import lax
from jax.experimental import pallas as pl
from jax.experimental.pallas import tpu as pltpu

F32 = jnp.float32
BF16 = jnp.bfloat16

D_MODEL = 1024
DEPTH = 4
EPS = 1e-6

GLA_HEADS, GLA_DK, GLA_DV, GLA_RANK, GLA_CHUNK = 4, 32, 64, 16, 64
GLA_GATE_NORM = 16.0
DIL_PATTERNS = ((128, 1), (512, 4), (2048, 16))
DIL_HEADS, DIL_DIM, DIL_STEPS = 4, 64, 128
GM_GROUPS, GM_DIM, GM_CHUNK = 4, 64, 128
SSD_HEADS, SSD_P, SSD_GROUPS, SSD_N, SSD_CONV, SSD_CHUNK = 8, 64, 2, 128, 4, 128
N_BRANCH = 4
MLP_HIDDEN = 4 * D_MODEL

GLA_QK = GLA_HEADS * GLA_DK
GLA_V = GLA_HEADS * GLA_DV
DIL_W = len(DIL_PATTERNS) * DIL_HEADS * DIL_DIM
DIL_OUT = DIL_HEADS * DIL_DIM
GM_W = GM_GROUPS * GM_DIM
SSD_INNER = SSD_HEADS * SSD_P
SSD_BC = SSD_GROUPS * SSD_N
SSD_XBC = SSD_INNER + 2 * SSD_BC
IN_SPLITS = (GLA_QK, GLA_QK, GLA_V, GLA_V, GLA_RANK, DIL_W, DIL_W, DIL_W, GM_W, GM_W,
             SSD_INNER, SSD_XBC, SSD_HEADS, N_BRANCH * D_MODEL)
_SP = tuple(int(v) for v in np.cumsum((0,) + IN_SPLITS))

LANE = 128
NEG = -1e30

PROJ_OUTS = (("gla", 2 * GLA_QK + 2 * GLA_V, BF16, 1), ("glr", LANE, F32, 1),
             ("dil0", 3 * DIL_OUT, BF16, DIL_PATTERNS[0][1]), ("dil1", 3 * DIL_OUT, BF16, DIL_PATTERNS[1][1]),
             ("dil2", 3 * DIL_OUT, BF16, DIL_PATTERNS[2][1]),
             ("gm", 2 * GM_W, BF16, 1), ("ssd_z", SSD_INNER, BF16, 1), ("ssd_xbc", SSD_XBC, BF16, 1),
             ("dt", LANE, F32, 1), ("gates", N_BRANCH * D_MODEL, BF16, 1))
PROJ_NAMES = tuple(o[0] for o in PROJ_OUTS)

TM_PROJ = 512
TM_MLP = 512
T_MIX = 1024
GLA_CUMSUM_ROWS = 256
COL_CHUNK = 512
MLP_CHUNK = 1024
CAST_BLOCK = (128, 1024)
DIL_QBLOCKS = 8
DIL_LOOKAHEAD = 2


V7X_VMEM_BYTES = 64 * 1024 * 1024


def _cparams(sem, vmem_mib):
    assert vmem_mib * 1024 * 1024 <= V7X_VMEM_BYTES
    return pltpu.CompilerParams(dimension_semantics=sem, vmem_limit_bytes=V7X_VMEM_BYTES)


def _const_spec(shape):
    nd = len(shape)
    return pl.BlockSpec(shape, lambda *_: (0,) * nd, pipeline_mode=pl.Buffered(1))


def _layer_spec(shape, layer):
    nd = len(shape)
    return pl.BlockSpec((None,) + tuple(shape), lambda *_: (layer,) + (0,) * nd,
                        pipeline_mode=pl.Buffered(1))


class _Part(NamedTuple):
    kernel: Callable
    in_specs: list
    out_specs: list
    out_shape: list
    scratch: list
    args: list


def _fused_call(parts, grid, sem, vmem_mib, name):
    n_in = [len(p.in_specs) for p in parts]
    n_out = [len(p.out_specs) for p in parts]
    n_scr = [len(p.scratch) for p in parts]

    def body(*refs):
        ins, outs, scr = refs[:sum(n_in)], refs[sum(n_in):sum(n_in) + sum(n_out)], refs[sum(n_in) + sum(n_out):]
        for k, p in enumerate(parts):
            take = lambda group, counts: group[sum(counts[:k]):sum(counts[:k + 1])]
            p.kernel(*take(ins, n_in), *take(outs, n_out), *take(scr, n_scr))

    flat = pl.pallas_call(
        body, grid=grid,
        in_specs=[s for p in parts for s in p.in_specs],
        out_specs=[s for p in parts for s in p.out_specs],
        out_shape=[s for p in parts for s in p.out_shape],
        scratch_shapes=[s for p in parts for s in p.scratch],
        compiler_params=_cparams(sem, vmem_mib), name=name,
    )(*[a for p in parts for a in p.args])
    return [flat[sum(n_out[:k]):sum(n_out[:k + 1])] for k in range(len(parts))]


def _dot(a, b):
    return jnp.dot(a.astype(BF16), b.astype(BF16), preferred_element_type=F32)


def _dot_nt(a, b):
    return lax.dot_general(a.astype(BF16), b.astype(BF16), (((1,), (1,)), ((), ())),
                           preferred_element_type=F32)


def _split(a):
    hi = a.astype(BF16)
    lo = (a - hi.astype(F32)).astype(BF16)
    return hi, lo


def _dot_sel_r(a, sel):
    hi, lo = _split(a)
    return (jnp.dot(hi, sel, preferred_element_type=F32) + jnp.dot(lo, sel, preferred_element_type=F32))


def _dot_sel_l(sel, a):
    hi, lo = _split(a)
    return (jnp.dot(sel, hi, preferred_element_type=F32) + jnp.dot(sel, lo, preferred_element_type=F32))


def _sigmoid(x):
    return 1.0 / (1.0 + jnp.exp(-x))


def _silu(x):
    return x * _sigmoid(x)


def _gelu(x):
    return 0.5 * x * (1.0 + lax.erf(x * (2.0 ** -0.5)))


def _softplus(x):
    return jnp.maximum(x, 0.0) + jnp.log1p(jnp.exp(-jnp.abs(x)))


def _modulated_norm(x, nw, sc, sh):
    y = x * lax.rsqrt(jnp.mean(x * x, axis=-1, keepdims=True) + EPS)
    return (y * nw) * (1.0 + sc) + sh


def _ada_kernel(c_ref, w_ref, b_ref, o_ref):
    o_ref[...] = _dot(_silu(c_ref[...]), w_ref[...]) + b_ref[...]


def _ada_mod(c, w_ada, b_ada):
    B, D = c.shape
    rows = 16
    c_pad = jnp.pad(c, ((0, rows - B), (0, 0)))
    n_col = w_ada.shape[-1] // D
    out = pl.pallas_call(
        _ada_kernel,
        grid=(DEPTH, n_col),
        in_specs=[pl.BlockSpec((rows, D), lambda l, j: (0, 0)),
                  pl.BlockSpec((None, D, D), lambda l, j: (l, 0, j)),
                  pl.BlockSpec((None, 1, D), lambda l, j: (l, 0, j))],
        out_specs=pl.BlockSpec((None, rows, D), lambda l, j: (l, 0, j)),
        out_shape=jax.ShapeDtypeStruct((DEPTH, rows, n_col * D), F32),
        compiler_params=_cparams(("arbitrary", "arbitrary"), 32),
        name="ada_mod",
    )(c_pad, w_ada, b_ada.reshape(DEPTH, 1, -1))
    return out[:, :B].reshape(DEPTH, B, n_col, 1, D)


def _inproj_kernel(x_ref, nw_ref, sc_ref, sh_ref, cw_ref, cb_ref, w1_ref, w2_ref, *refs):
    n_out = len(PROJ_OUTS)
    w_refs, o_refs = refs[:n_out], refs[n_out:2 * n_out]
    w1_bf_ref, w2_bf_ref, stage_ref, carry_ref, win_ref = refs[2 * n_out:]
    TM = x_ref.shape[0]
    w1_bf_ref[...] = w1_ref[...].astype(w1_bf_ref.dtype)
    w2_bf_ref[...] = w2_ref[...].astype(w2_bf_ref.dtype)
    KEEP = carry_ref.shape[0]

    @pl.when(pl.program_id(1) == 0)
    def _():
        carry_ref[...] = jnp.zeros_like(carry_ref)

    h = _modulated_norm(x_ref[...], nw_ref[...], sc_ref[...], sh_ref[...]).astype(BF16)
    for w_ref, o_ref, (name, width, dtype, dil) in zip(w_refs, o_refs, PROJ_OUTS):
        for c0 in range(0, width, COL_CHUNK):
            c1 = min(c0 + COL_CHUNK, width)
            res = jnp.dot(h, w_ref[:, c0:c1], preferred_element_type=F32)
            if name == "ssd_xbc":
                win_ref[0:KEEP, :] = carry_ref[:, c0:c1]
                win_ref[KEEP:, :] = res
                carry_ref[:, c0:c1] = res[TM - KEEP:]
                conv = cb_ref[:, c0:c1]
                for j in range(SSD_CONV):
                    off = KEEP - (SSD_CONV - 1) + j
                    conv = conv + cw_ref[j:j + 1, c0:c1] * win_ref[off:off + TM, :]
                o_ref[:, c0:c1] = _silu(conv).astype(dtype)
            elif name == "gates":
                o_ref[:, c0:c1] = (0.5 * jnp.tanh(0.5 * res) + 0.5).astype(dtype)
            elif dil == 1:
                o_ref[:, c0:c1] = res.astype(dtype)
            else:
                for t in range((c1 - c0) // LANE):
                    stage_ref[c0 // LANE + t] = res[:, t * LANE:(t + 1) * LANE]
        if dil > 1:
            for r in range(dil):
                for t in range(width // LANE):
                    o_ref[r, :, t * LANE:(t + 1) * LANE] = stage_ref[
                        t, pl.ds(r, TM // dil, stride=dil), :].astype(dtype)


def _inproj(x, nw, mod, conv_w, conv_b, w_mlp1, w_mlp2, w_parts, layer):
    B, S, D = x.shape
    TM = TM_PROJ
    out_specs, out_shape = [], []
    for _, w, dt, dil in PROJ_OUTS:
        if dil == 1:
            out_specs.append(pl.BlockSpec((None, TM, w), lambda b, i: (b, i, 0)))
            out_shape.append(jax.ShapeDtypeStruct((B, S, w), dt))
        else:
            out_specs.append(pl.BlockSpec((None, dil, TM // dil, w), lambda b, i: (b, 0, i, 0)))
            out_shape.append(jax.ShapeDtypeStruct((B, dil, S // dil, w), dt))
    n_i, (cr, cc) = S // TM, CAST_BLOCK
    cast_in, steps = [], B * n_i
    for w in (w_mlp1, w_mlp2):
        _, rows, cols = w.shape
        assert rows % cr == 0 and cols % cc == 0 and (rows // cr) * (cols // cc) == steps
        at = lambda b, i, nc=cols // cc: ((b * n_i + i) // nc, (b * n_i + i) % nc)
        cast_in.append(pl.BlockSpec((None, cr, cc), lambda b, i, at=at: (layer, *at(b, i))))
        out_specs.append(pl.BlockSpec((cr, cc), at))
        out_shape.append(jax.ShapeDtypeStruct((rows, cols), BF16))
    return pl.pallas_call(
        _inproj_kernel,
        grid=(B, S // TM),
        in_specs=[pl.BlockSpec((None, TM, D), lambda b, i: (b, i, 0)),
                  _layer_spec((1, D), layer), _mod_spec(layer, 1), _mod_spec(layer, 0),
                  _layer_spec((SSD_CONV, SSD_XBC), layer), _layer_spec((1, SSD_XBC), layer), *cast_in]
                 + [_layer_spec((D, o[1]), layer) for o in PROJ_OUTS],
        out_specs=out_specs,
        out_shape=out_shape,
        scratch_shapes=[pltpu.VMEM((3 * DIL_OUT // LANE, TM, LANE), F32),
                        pltpu.VMEM((8, SSD_XBC), F32),
                        pltpu.VMEM((TM + 8, COL_CHUNK), F32)],
        compiler_params=_cparams(("parallel", "arbitrary"), 58),
        name="inproj",
    )(x, nw, mod, mod, conv_w, conv_b, w_mlp1, w_mlp2, *w_parts)


def _gla_kernel(gla_ref, glr_ref, wa_ref, ba_ref, nw_ref, tri_ref, ones_ref, o_ref, st_ref):
    C, H, K, V = GLA_CHUNK, GLA_HEADS, GLA_DK, GLA_DV
    T = gla_ref.shape[0]

    @pl.when(pl.program_id(1) == 0)
    def _():
        st_ref[...] = jnp.zeros_like(st_ref)

    blk = gla_ref[...]
    q = blk[:, 0:GLA_QK].astype(F32) * (K ** -0.5)
    k = blk[:, GLA_QK:2 * GLA_QK].astype(F32)
    v = blk[:, 2 * GLA_QK:2 * GLA_QK + GLA_V]
    g = blk[:, 2 * GLA_QK + GLA_V:].astype(F32)
    pre = _dot(glr_ref[...], wa_ref[...]) + ba_ref[...]
    log_a = -_softplus(-pre) * (1.0 / GLA_GATE_NORM)
    tri = tri_ref[...]
    TB = tri.shape[0]
    b_all = jnp.concatenate([_dot_sel_l(tri, log_a[i * TB:(i + 1) * TB]) for i in range(T // TB)],
                            axis=0)

    k_head = lax.broadcasted_iota(jnp.int32, (C, GLA_QK), 1) // K
    v_head = lax.broadcasted_iota(jnp.int32, (C, GLA_V), 1) // V
    row = lax.broadcasted_iota(jnp.int32, (H * C, C), 0)
    causal = (row % C) >= lax.broadcasted_iota(jnp.int32, (H * C, C), 1)
    st_mask = (lax.broadcasted_iota(jnp.int32, (GLA_V, GLA_QK), 0) // V
               == lax.broadcasted_iota(jnp.int32, (GLA_V, GLA_QK), 1) // K)

    n_chunk = T // C
    chunks = [slice(c * C, (c + 1) * C) for c in range(n_chunk)]
    b_c = [b_all[rs] for rs in chunks]
    q_t = [q[rs] * jnp.exp(b) for rs, b in zip(chunks, b_c)]
    k_t = [k[rs] * jnp.exp(-b) for rs, b in zip(chunks, b_c)]
    k_dec = [k[rs] * jnp.exp(b[C - 1:C] - b) for rs, b in zip(chunks, b_c)]
    att = []
    for c in range(n_chunk):
        q_stack = jnp.concatenate([jnp.where(k_head == h, q_t[c], 0.0) for h in range(H)], axis=0)
        att.append(jnp.where(causal, _dot_nt(q_stack, k_t[c]), 0.0))
    kv_t = [jnp.where(st_mask, _dot(v[rs].astype(F32).T, k_dec[c]), 0.0)
            for c, rs in enumerate(chunks)]
    o_all = [_dot(att[c], v[rs]) for c, rs in enumerate(chunks)]
    st = st_ref[...]
    outs = []
    for c in range(n_chunk):
        o = _dot_nt(q_t[c], st)
        for h in range(H):
            o = o + jnp.where(v_head == h, o_all[c][h * C:(h + 1) * C], 0.0)
        st = st * jnp.exp(b_c[c][C - 1:C]) + kv_t[c]
        outs.append(o)
    st_ref[...] = st

    o = jnp.concatenate(outs, axis=0)
    ms = _dot_sel_r(o * o, ones_ref[...]) * (1.0 / V)
    o = o * lax.rsqrt(ms + EPS) * nw_ref[...]
    o_ref[...] = (o * _silu(g)).astype(o_ref.dtype)


def _gla(gla, glr, wa, ba, nw, layer):
    B, S, W = gla.shape
    T, C, TB = T_MIX, GLA_CHUNK, GLA_CUMSUM_ROWS
    idx = np.arange(TB)
    tri = ((idx[:, None] // C == idx[None, :] // C) & (idx[:, None] >= idx[None, :]))
    hv = np.arange(GLA_V) // GLA_DV
    ones = hv[:, None] == hv[None, :]
    return _Part(
        _gla_kernel,
        in_specs=[pl.BlockSpec((None, T, W), lambda b, i: (b, i, 0)),
                  pl.BlockSpec((None, T, LANE), lambda b, i: (b, i, 0)),
                  _layer_spec((LANE, GLA_QK), layer), _layer_spec((1, GLA_QK), layer),
                  _layer_spec((1, GLA_V), layer), _const_spec((TB, TB)), _const_spec((GLA_V, GLA_V))],
        out_specs=[pl.BlockSpec((None, T, GLA_V), lambda b, i: (b, i, 0))],
        out_shape=[jax.ShapeDtypeStruct((B, S, GLA_V), BF16)],
        scratch=[pltpu.VMEM((GLA_V, GLA_QK), F32)],
        args=[gla, glr, wa, ba, nw, jnp.asarray(tri, BF16), jnp.asarray(ones, BF16)])


def _dil_kernel(*refs, row_steps):
    n_pat = len(row_steps)
    for p in range(n_pat):
        first = jnp.minimum(pl.program_id(1) % row_steps[p], 1)
        _dil_planes(*refs[6 * p:6 * p + 6], *refs[6 * n_pat + 2 * p:6 * n_pat + 2 * p + 2], first)


def _dil_planes(q_ref, kp_ref, k_ref, vp_ref, v_ref, bias_ref, o_ref, stat_ref, first):
    H, E, BLK = DIL_HEADS, DIL_DIM, DIL_STEPS
    n_res, n_blk = q_ref.shape[0], q_ref.shape[1] // BLK
    head = lax.broadcasted_iota(jnp.int32, (BLK, H * E), 1) // E
    lane = lax.broadcasted_iota(jnp.int32, (BLK, LANE), 1)
    kcat = [jnp.concatenate([kp_ref[r], k_ref[r]], axis=0) for r in range(n_res)]
    vcat = [jnp.concatenate([vp_ref[r], v_ref[r]], axis=0) for r in range(n_res)]
    v_head = lax.broadcasted_iota(jnp.int32, vcat[0].shape, 1) // E
    v_masked = [[jnp.where(v_head == h, vc, jnp.zeros_like(vc)) for h in range(H)] for vc in vcat]
    blocks =[(r, i) for r in range(n_res) for i in range(n_blk)]

    def scores(r, i):
        q = q_ref[r, i * BLK:(i + 1) * BLK, :] * jnp.asarray(E ** -0.5, BF16)
        q_stack = jnp.concatenate([jnp.where(head == h, q, jnp.zeros_like(q)) for h in range(H)], axis=0)
        bias = bias_ref[first] if i == 0 else bias_ref[1]
        return _dot_nt(q_stack, kcat[r][i * BLK:(i + 2) * BLK]) + bias

    pending = [scores(*blk) for blk in blocks[:DIL_LOOKAHEAD]]
    for n, (r, i) in enumerate(blocks):
        s = pending.pop(0)
        if n + DIL_LOOKAHEAD < len(blocks):
            pending.append(scores(*blocks[n + DIL_LOOKAHEAD]))
        m = jnp.max(s, axis=-1, keepdims=True)
        p = jnp.exp(s - m)
        l = jnp.sum(p, axis=-1, keepdims=True)
        p = p.astype(BF16)
        p_heads = jnp.concatenate([p[h * BLK:(h + 1) * BLK] for h in range(H)], axis=1)
        v_heads = jnp.concatenate([vm[i * BLK:(i + 2) * BLK] for vm in v_masked[r]], axis=0)
        o = jnp.dot(p_heads, v_heads, preferred_element_type=F32)
        stat = jnp.zeros((BLK, LANE), F32)
        for h in range(H):
            hs = slice(h * BLK, (h + 1) * BLK)
            stat = jnp.where(lane == h, m[hs], jnp.where(lane == H + h, l[hs], stat))
        rows = slice(i * BLK, (i + 1) * BLK)
        o_ref[r, rows, :] = o.astype(o_ref.dtype)
        stat_ref[r, rows, :] = stat


def _dil_bias(window, dil, slopes):
    BLK = DIL_STEPS
    steps = np.arange(BLK)[:, None] + BLK - np.arange(2 * BLK)[None, :]
    ok = (steps >= 0) & (steps <= window // dil)
    bias = -slopes[:, None, None] * (steps * dil)[None].astype(np.float32)
    full = np.where(ok[None], bias, NEG).astype(np.float32)
    first = np.where((np.arange(2 * BLK) >= BLK)[None, None, :], full, NEG)
    return np.stack([first.reshape(-1, 2 * BLK), full.reshape(-1, 2 * BLK)]).astype(np.float32)


def _dil_pattern(qkv, window, dil, slopes):
    B, _, n, _ = qkv.shape
    HE, BLK = DIL_OUT, DIL_STEPS
    n_blk = min(DIL_QBLOCKS, n // BLK)
    n_res = min(dil, DIL_QBLOCKS // n_blk)
    rows = n_blk * BLK
    assert n % rows == 0 and dil % n_res == 0
    row_steps = n // rows
    at = lambda j: (j // row_steps, j % row_steps)
    cur = lambda t: pl.BlockSpec((None, n_res, rows, HE), lambda b, j: (b, *at(j), t))
    prev = lambda t: pl.BlockSpec((None, n_res, BLK, HE),
                                  lambda b, j: (b, at(j)[0], jnp.maximum(at(j)[1] * n_blk - 1, 0), t))
    in_specs = [cur(0), prev(1), cur(1), prev(2), cur(2), _const_spec((2, DIL_HEADS * BLK, 2 * BLK))]
    out_specs = [pl.BlockSpec((None, n_res, rows, HE), lambda b, j: (b, *at(j), 0)),
                 pl.BlockSpec((None, n_res, rows, LANE), lambda b, j: (b, *at(j), 0))]
    out_shape = [jax.ShapeDtypeStruct((B, dil, n, HE), BF16), jax.ShapeDtypeStruct((B, dil, n, LANE), F32)]
    args = [qkv, qkv, qkv, qkv, qkv, jnp.asarray(_dil_bias(window, dil, slopes))]
    return in_specs, out_specs, out_shape, args, row_steps, (dil // n_res) * row_steps


def _dilated(qkvs):
    n_pat = len(DIL_PATTERNS)
    n_h = n_pat * DIL_HEADS
    slopes = (2.0 ** (-8.0 * np.arange(1, n_h + 1) / n_h)).astype(np.float32).reshape(n_pat, DIL_HEADS)
    in_specs, out_specs, out_shape, args, row_steps, steps = [], [], [], [], [], set()
    for p, (w, d) in enumerate(DIL_PATTERNS):
        qkv = qkvs[p] if qkvs[p].ndim == 4 else qkvs[p][:, None]
        assert qkv.shape[1] == d
        i_s, o_s, o_sh, a, rs, n_steps = _dil_pattern(qkv, w, d, slopes[p])
        in_specs += i_s; out_specs += o_s; out_shape += o_sh; args += a
        row_steps.append(rs); steps.add(n_steps)
    assert len(steps) == 1, "patterns must split into equally many grid steps"
    assert steps.pop() == qkvs[0].shape[1] // T_MIX
    return _Part(functools.partial(_dil_kernel, row_steps=tuple(row_steps)),
                 in_specs=in_specs, out_specs=out_specs, out_shape=out_shape, scratch=[], args=args)


def _gm_kernel(gm_ref, lnw_ref, lnb_ref, w_ref, bs_ref, o_ref):
    G, E, C = GM_GROUPS, GM_DIM, GM_CHUNK
    T = gm_ref.shape[0]
    blk = gm_ref[...].astype(F32)
    u = _gelu(blk[:, :GM_W])
    v = _gelu(blk[:, GM_W:])
    mu = jnp.mean(v, axis=-1, keepdims=True)
    var = jnp.mean(jnp.square(v - mu), axis=-1, keepdims=True)
    v = (v - mu) * lax.rsqrt(var + EPS) * lnw_ref[...] + lnb_ref[...]
    group = lax.broadcasted_iota(jnp.int32, (C, GM_W), 1) // E
    w = w_ref[...]
    bs = bs_ref[...]
    for c in range(T // C):
        rs = slice(c * C, (c + 1) * C)
        r = _dot(w, v[rs])
        sv = bs
        for gi in range(G):
            sv = sv + jnp.where(group == gi, r[gi * C:(gi + 1) * C], 0.0)
        o_ref[rs, :] = (u[rs] * sv).astype(o_ref.dtype)


def _gmlp_params(ws, bs):
    C = GM_CHUNK
    tril = jnp.tril(jnp.ones((C, C), bool))
    w_stack = jnp.where(tril, ws, 0.0).reshape(-1, GM_GROUPS * C, C).astype(BF16)
    bs_exp = jnp.repeat(jnp.swapaxes(bs, 1, 2), GM_DIM, axis=2)
    return w_stack, bs_exp


def _gmlp(gm, lnw, lnb, w_stack, bs_exp, layer):
    B, S, W = gm.shape
    C = GM_CHUNK
    return _Part(
        _gm_kernel,
        in_specs=[pl.BlockSpec((None, T_MIX, W), lambda b, i: (b, i, 0)),
                  _layer_spec((1, GM_W), layer), _layer_spec((1, GM_W), layer),
                  _layer_spec((GM_GROUPS * C, C), layer), _layer_spec((C, GM_W), layer)],
        out_specs=[pl.BlockSpec((None, T_MIX, GM_W), lambda b, i: (b, i, 0))],
        out_shape=[jax.ShapeDtypeStruct((B, S, GM_W), BF16)],
        scratch=[], args=[gm, lnw, lnb, w_stack, bs_exp])


def _ssd_kernel(z_ref, xbc_ref, dt_ref, dtb_ref, alog_ref, dskip_ref, nw_ref, tri_ref, exp_ref,
                o_ref, st_ref):
    H, P, G, N, L = SSD_HEADS, SSD_P, SSD_GROUPS, SSD_N, SSD_CHUNK
    R = H // G
    GW = R * P
    T = z_ref.shape[0]

    @pl.when(pl.program_id(1) == 0)
    def _():
        st_ref[...] = jnp.zeros_like(st_ref)

    dt_all = _softplus(dt_ref[...] + dtb_ref[...])
    da_all = dt_all * (-jnp.exp(alog_ref[...]))
    tri = tri_ref[...]
    expand = exp_ref[...]
    lower = lax.broadcasted_iota(jnp.int32, (L, L), 0) >= lax.broadcasted_iota(jnp.int32, (L, L), 1)
    head_in_group = lax.broadcasted_iota(jnp.int32, (L, GW), 1) // P

    chunks = [slice(c * L, (c + 1) * L) for c in range(T // L)]
    groups = [slice(gi * GW, (gi + 1) * GW) for gi in range(G)]
    b_cols = [slice(SSD_INNER + gi * N, SSD_INNER + (gi + 1) * N) for gi in range(G)]
    c_cols = [slice(SSD_INNER + SSD_BC + gi * N, SSD_INNER + SSD_BC + (gi + 1) * N) for gi in range(G)]
    xs = [xbc_ref[rs, :SSD_INNER].astype(F32) for rs in chunks]
    cs = [_dot_sel_l(tri, da_all[rs]) for rs in chunks]
    dt_e = [_dot(dt_all[rs], expand) for rs in chunks]
    dec_e = [_dot(jnp.exp(cs_c[L - 1:L] - cs_c) * dt_all[rs], expand) for rs, cs_c in zip(chunks, cs)]
    ecs_e = [_dot_sel_r(jnp.exp(cs_c), expand) for cs_c in cs]
    x_dt = [(x * e).astype(BF16) for x, e in zip(xs, dt_e)]
    x_dec = [(x * e).astype(BF16) for x, e in zip(xs, dec_e)]
    cb = [[_dot_nt(xbc_ref[rs, c_cols[gi]], xbc_ref[rs, b_cols[gi]]) for gi in range(G)]
          for rs in chunks]
    bx = [[_dot(xbc_ref[rs, b_cols[gi]].astype(F32).T, x_dec[c][:, groups[gi]]) for gi in range(G)]
          for c, rs in enumerate(chunks)]
    y_intra = []
    for c in range(len(chunks)):
        cs_t = cs[c].T
        per_group = []
        for gi in range(G):
            acc = None
            for r in range(R):
                h = gi * R + r
                seg = cs[c][:, h:h + 1] - cs_t[h:h + 1, :]
                wgt = cb[c][gi] * jnp.exp(jnp.where(lower, seg, NEG))
                y_h = _dot(wgt, x_dt[c][:, groups[gi]])
                acc = y_h if r == 0 else jnp.where(head_in_group == r, y_h, acc)
            per_group.append(acc)
        y_intra.append(per_group)

    for c, rs in enumerate(chunks):
        ys = []
        for gi, gs in enumerate(groups):
            y_state = _dot(xbc_ref[rs, c_cols[gi]], st_ref[:, gs]) * ecs_e[c][:, gs]
            st_ref[:, gs] = st_ref[:, gs] * ecs_e[c][L - 1:L, gs] + bx[c][gi]
            ys.append(y_state + y_intra[c][gi])
        y = jnp.concatenate(ys, axis=1) + xs[c] * dskip_ref[...]
        y = y * _silu(z_ref[rs, :].astype(F32))
        normed = []
        for gs in groups:
            y_g = y[:, gs]
            normed.append(y_g * lax.rsqrt(jnp.mean(y_g * y_g, axis=-1, keepdims=True) + EPS))
        o_ref[rs, :] = (jnp.concatenate(normed, axis=1) * nw_ref[...]).astype(o_ref.dtype)


def _ssd_params(dt_bias, a_log, d_skip, norm_w):
    pad = lambda t: jnp.pad(t, ((0, 0), (0, LANE - t.shape[1])))[:, None, :]
    return (pad(dt_bias), pad(a_log), jnp.repeat(d_skip, SSD_P, axis=1)[:, None, :], norm_w[:, None, :])


def _ssd(z, xbc, dt, params, layer):
    B, S, _ = z.shape
    L = SSD_CHUNK
    tri = np.tril(np.ones((L, L), np.float32))
    expand = np.zeros((LANE, SSD_INNER), np.float32)
    for h in range(SSD_HEADS):
        expand[h, h * SSD_P:(h + 1) * SSD_P] = 1.0
    return _Part(
        _ssd_kernel,
        in_specs=[pl.BlockSpec((None, T_MIX, SSD_INNER), lambda b, i: (b, i, 0)),
                  pl.BlockSpec((None, T_MIX, SSD_XBC), lambda b, i: (b, i, 0)),
                  pl.BlockSpec((None, T_MIX, LANE), lambda b, i: (b, i, 0)),
                  _layer_spec((1, LANE), layer), _layer_spec((1, LANE), layer),
                  _layer_spec((1, SSD_INNER), layer), _layer_spec((1, SSD_INNER), layer),
                  _const_spec((L, L)), _const_spec((LANE, SSD_INNER))],
        out_specs=[pl.BlockSpec((None, T_MIX, SSD_INNER), lambda b, i: (b, i, 0))],
        out_shape=[jax.ShapeDtypeStruct((B, S, SSD_INNER), BF16)],
        scratch=[pltpu.VMEM((SSD_N, SSD_INNER), F32)],
        args=[z, xbc, dt, *params, jnp.asarray(tri, BF16), jnp.asarray(expand, BF16)])


def _token_order(src_ref, stage_ref):
    dil, per, w = src_ref.shape
    for r in range(dil):
        plane = src_ref[r].astype(F32)
        for t in range(w // LANE):
            stage_ref[t, pl.ds(r, per, stride=dil), :] = plane[:, t * LANE:(t + 1) * LANE]
    return jnp.concatenate([stage_ref[t] for t in range(w // LANE)], axis=1)


def _tail_kernel(x_ref, g1_ref, gla_ref, d0_ref, d1_ref, d2_ref, l0_ref, l1_ref, l2_ref, gm_ref, ssd_ref,
                 gates_ref, wg_ref, wd_ref, wm_ref, ws_ref, wo_ref, exp_ref,
                 nw_ref, sc_ref, sh_ref, g2_ref, w1_ref, w2_ref, fw_ref, o_ref,
                 od1_ref, od2_ref, ol1_ref, ol2_ref, *, final_norm):
    D = D_MODEL
    outs = (d0_ref[...].astype(F32), _token_order(d1_ref, od1_ref), _token_order(d2_ref, od2_ref))
    stats = (l0_ref[...], _token_order(l1_ref, ol1_ref), _token_order(l2_ref, ol2_ref))
    dens = [pltpu.roll(s, LANE - DIL_HEADS, axis=1) for s in stats]
    m = jnp.maximum(jnp.maximum(stats[0], stats[1]), stats[2])
    es = [jnp.exp(s - m) for s in stats]
    total = es[0] * dens[0] + es[1] * dens[1] + es[2] * dens[2]
    is_head = lax.broadcasted_iota(jnp.int32, m.shape, 1) < DIL_HEADS
    expand = exp_ref[...]
    o_dil = None
    for e, o_p in zip(es, outs):
        term = _dot_sel_r(jnp.where(is_head, e / total, 0.0), expand) * o_p
        o_dil = term if o_dil is None else o_dil + term
    branches = ((gla_ref[...], wg_ref), (o_dil, wd_ref), (gm_ref[...], wm_ref), (ssd_ref[...], ws_ref))
    merged = None
    for i, (o_b, w_ref) in enumerate(branches):
        gate = gates_ref[:, i * D:(i + 1) * D].astype(F32)
        term = gate * _dot(o_b, w_ref[...])
        merged = term if merged is None else merged + term
    x = x_ref[...] + g1_ref[...] * _dot(merged, wo_ref[...])

    h = _modulated_norm(x, nw_ref[...], sc_ref[...], sh_ref[...]).astype(BF16)
    acc = jnp.zeros(x.shape, F32)
    for c0 in range(0, MLP_HIDDEN, MLP_CHUNK):
        a = jnp.maximum(jnp.dot(h, w1_ref[:, c0:c0 + MLP_CHUNK], preferred_element_type=F32), 0.0)
        acc = acc + _dot(a * a, w2_ref[c0:c0 + MLP_CHUNK, :])
    y = x + g2_ref[...] * acc
    if final_norm:
        y = y * lax.rsqrt(jnp.mean(y * y, axis=-1, keepdims=True) + EPS) * fw_ref[...]
    o_ref[...] = y


def _mod_spec(layer, which):
    return pl.BlockSpec((None, None, None, 1, D_MODEL), lambda b, i: (layer, b, which, 0, 0))


def _tail(x, mod, o_gla, o_dil, lse_dil, o_gm, o_ssd, gates, w_gla, w_dil, w_gm, w_ssd, w_out,
          nw2, w1, w2, fw, layer, final_norm):
    B, S, D = x.shape
    TM = TM_MLP
    tok = lambda w: pl.BlockSpec((None, TM, w), lambda b, i: (b, i, 0))

    def plane(w, dil):
        if dil == 1:
            return pl.BlockSpec((None, None, TM, w), lambda b, i: (b, 0, i, 0))
        return pl.BlockSpec((None, dil, TM // dil, w), lambda b, i: (b, 0, i, 0))

    dils = [d for _, d in DIL_PATTERNS]
    assert dils[0] == 1 and len(dils) == 3
    expand = np.zeros((LANE, DIL_OUT), np.float32)
    for h in range(DIL_HEADS):
        expand[h, h * DIL_DIM:(h + 1) * DIL_DIM] = 1.0
    return pl.pallas_call(
        functools.partial(_tail_kernel, final_norm=final_norm),
        grid=(B, S // TM),
        in_specs=[tok(D), _mod_spec(layer, 2), tok(GLA_V)]
                 + [plane(DIL_OUT, d) for d in dils] + [plane(LANE, d) for d in dils]
                 + [tok(GM_W), tok(SSD_INNER), tok(N_BRANCH * D),
                    _layer_spec((GLA_V, D), layer), _layer_spec((DIL_OUT, D), layer),
                    _layer_spec((GM_W, D), layer), _layer_spec((SSD_INNER, D), layer),
                    _layer_spec((D, D), layer), _const_spec((LANE, DIL_OUT)),
                    _layer_spec((1, D), layer), _mod_spec(layer, 4), _mod_spec(layer, 3),
                    _mod_spec(layer, 5), _const_spec((D, MLP_HIDDEN)),
                    _const_spec((MLP_HIDDEN, D)), _const_spec((1, D))],
        out_specs=tok(D),
        out_shape=jax.ShapeDtypeStruct((B, S, D), F32),
        scratch_shapes=[pltpu.VMEM((DIL_OUT // LANE, TM, LANE), F32), pltpu.VMEM((DIL_OUT // LANE, TM, LANE), F32),
                        pltpu.VMEM((1, TM, LANE), F32), pltpu.VMEM((1, TM, LANE), F32)],
        compiler_params=_cparams(("parallel", "parallel"), 58),
        name="tail_final" if final_norm else "tail",
    )(x, mod, o_gla, *o_dil, *lse_dil, o_gm, o_ssd, gates, w_gla, w_dil, w_gm, w_ssd, w_out,
      jnp.asarray(expand, BF16), nw2, mod, mod, mod, w1, w2, fw)


def _repack_plan():
    whole = lambda a, b: [(_SP[a], _SP[b] - _SP[a], _SP[b] - _SP[a])]
    plan = {"gla": whole(0, 4), "glr": [(_SP[4], LANE, GLA_RANK)], "gm": whole(8, 10),
            "ssd_z": whole(10, 11), "ssd_xbc": whole(11, 12), "dt": [(_SP[12], LANE, SSD_HEADS)],
            "gates": whole(13, 14)}
    for p in range(len(DIL_PATTERNS)):
        plan[f"dil{p}"] = [(_SP[t] + p * DIL_OUT, DIL_OUT, DIL_OUT) for t in (5, 6, 7)]
    plan = [plan[name] for name in PROJ_NAMES]
    assert [sum(w for _, w, _ in pieces) for pieces in plan] == [o[1] for o in PROJ_OUTS]
    return plan


def _repack_kernel(wt_ref, o_ref, *, valid):
    v = wt_ref[0]
    if valid < v.shape[0]:
        v = jnp.where(lax.broadcasted_iota(jnp.int32, v.shape, 0) < valid, v, 0.0)
    o_ref[...] = v.T.astype(o_ref.dtype)


def _pack_w_in(w_in):
    L, D, W = w_in.shape
    w_t = jnp.swapaxes(w_in, 1, 2)
    outs = []
    for (name, width, _, _), pieces in zip(PROJ_OUTS, _repack_plan()):
        start, piece_w, valid = pieces[0]
        if len(pieces) > 1:
            tile, stride = piece_w, pieces[1][0] - start
            assert all(p == (start + i * stride, tile, tile) for i, p in enumerate(pieces))
        else:
            tile = next(t for t in (1024, 768, 512, 256, LANE) if piece_w % t == 0)
            stride = tile
        assert start % 8 == 0 and stride % 8 == 0 and width % tile == 0
        outs.append(pl.pallas_call(
            functools.partial(_repack_kernel, valid=min(valid, tile)),
            grid=(L, width // tile),
            in_specs=[pl.BlockSpec((pl.Element(1), pl.Element(tile), pl.Element(D)),
                                   lambda l, t, start=start, stride=stride: (
                                       l, pl.multiple_of(start + t * stride, 8), 0))],
            out_specs=pl.BlockSpec((None, D, tile), lambda l, t: (l, 0, t)),
            out_shape=jax.ShapeDtypeStruct((L, D, width), BF16),
            compiler_params=_cparams(("parallel", "parallel"), 32),
            name=f"repack_{name}",
        )(w_t))
    return outs


def kernel(x, c, w_ada, b_ada, norm1_w, norm2_w, w_in, gla_w_a2, gla_b_a, gla_norm_w, gm_ln_w, gm_ln_b, gm_ws, gm_bs, ssd_conv_w, ssd_conv_b, ssd_dt_bias, ssd_a_log, ssd_d, ssd_norm_w, w_br_gla, w_br_dil, w_br_gm, w_br_ssd, w_out, w_mlp1, w_mlp2, final_norm_w):
    mod = _ada_mod(c, w_ada, b_ada)
    w_parts = _pack_w_in(w_in)
    wa = jnp.pad(gla_w_a2, ((0, 0), (0, LANE - GLA_RANK), (0, 0))).astype(BF16)
    w_gla, w_dil, w_gm, w_ssd, w_o = (t.astype(BF16) for t in (
        w_br_gla, w_br_dil, w_br_gm, w_br_ssd, w_out))
    rows = lambda t: t[:, None, :]
    n1, n2 = rows(norm1_w), rows(norm2_w)
    gla_ba, gla_nw = rows(gla_b_a), rows(jnp.tile(gla_norm_w, (1, GLA_HEADS)))
    gm_lnw, gm_lnb = rows(gm_ln_w), rows(gm_ln_b)
    gm_w, gm_b = _gmlp_params(gm_ws, gm_bs)
    ssd_params = _ssd_params(ssd_dt_bias, ssd_a_log, ssd_d, ssd_norm_w)
    conv_b = rows(ssd_conv_b)
    fw = final_norm_w.reshape(1, -1)
    for l in range(DEPTH):
        *proj, w1, w2 = _inproj(x, n1, mod, ssd_conv_w, conv_b, w_mlp1, w_mlp2, w_parts, l)
        p = dict(zip(PROJ_NAMES, proj))
        mixers = [_gla(p["gla"], p["glr"], wa, gla_ba, gla_nw, l),
                  _gmlp(p["gm"], gm_lnw, gm_lnb, gm_w, gm_b, l),
                  _ssd(p["ssd_z"], p["ssd_xbc"], p["dt"], ssd_params, l),
                  _dilated([p["dil0"], p["dil1"], p["dil2"]])]
        (o_gla,), (o_gm,), (o_ssd,), dil_outs = _fused_call(
            mixers, (x.shape[0], x.shape[1] // T_MIX), ("parallel", "arbitrary"), 48, "mixers")
        o_dil, lse_dil = list(dil_outs[0::2]), list(dil_outs[1::2])
        x = _tail(x, mod, o_gla, o_dil, lse_dil, o_gm, o_ssd, p["gates"], w_gla, w_dil, w_gm, w_ssd, w_o,
                  n2, w1, w2, fw, l, final_norm=(l == DEPTH - 1))
    return x
```
